```python
import math
import jax
import jax.numpy as jnp
from jax import lax
import numpy as np

D_MODEL = 1024
BATCH = 2
SEQ = 8192
DEPTH = 4

GRID_W = 64
CTX_LEN = 256
N_MIXERS = 4
HEAD_DIM = 64
GQA_HEADS = 16
GQA_KV_HEADS = 4
CONV_WIDTH = 31
DIFF_HEADS = 8
DIFF_V_DIM = 2 * HEAD_DIM
SWA_HEADS = 16
SWA_KV_HEADS = 4
WINDOW = 128
Q_BLOCK = 128
D_FF = 4 * D_MODEL
ROPE_THETA = 10000.0
EPS = 1e-6
NEG_INF = -1e30

kernel_name = 'hybrid_interleaved_dit_prefix_trunk'


def _n_layers_of(m):
    return (DEPTH - m + N_MIXERS - 1) // N_MIXERS


def rms_norm(x, g):
    xf = x.astype(jnp.float32)
    y = xf * lax.rsqrt(jnp.mean(xf * xf, axis=-1, keepdims=True) + EPS)
    return (y * g.astype(jnp.float32)).astype(x.dtype)


def layer_norm(x, g, b):
    xf = x.astype(jnp.float32)
    mu = jnp.mean(xf, axis=-1, keepdims=True)
    var = jnp.mean(jnp.square(xf - mu), axis=-1, keepdims=True)
    y = (xf - mu) * lax.rsqrt(var + EPS)
    return (y * g.astype(jnp.float32) + b.astype(jnp.float32)).astype(x.dtype)


def modulate(x, g, shift, scale):
    return rms_norm(x, g) * (1 + scale) + shift


def rope_2d(rows, dim):
    row = jnp.repeat(jnp.arange(rows, dtype=jnp.float32), GRID_W)
    col = jnp.tile(jnp.arange(GRID_W, dtype=jnp.float32), rows)
    half = dim // 2
    inv = 1.0 / jnp.power(ROPE_THETA, jnp.arange(0, half, 2, dtype=jnp.float32) / half)
    ang = jnp.concatenate([row[:, None] * inv, col[:, None] * inv], axis=-1)
    return jnp.cos(ang), jnp.sin(ang)


def apply_rope(x, cos, sin):
    shp = x.shape
    extra = x.ndim - 3
    tab = (1, cos.shape[0]) + (1,) * extra + (cos.shape[1],)
    c, s = cos.reshape(tab), sin.reshape(tab)
    xf = x.astype(jnp.float32).reshape(shp[:-1] + (shp[-1] // 2, 2))
    x1, x2 = xf[..., 0], xf[..., 1]
    out = jnp.stack([x1 * c - x2 * s, x1 * s + x2 * c], axis=-1).reshape(shp)
    return out.astype(x.dtype)


def _to_blocks(x):
    b, t = x.shape[:2]
    return jnp.moveaxis(x.reshape((b, t // Q_BLOCK, Q_BLOCK) + x.shape[2:]), 1, 0)


def _from_blocks(x):
    x = jnp.moveaxis(x, 0, 1)
    return x.reshape((x.shape[0], x.shape[1] * x.shape[2]) + x.shape[3:])


def gqa_project(h, w_qkv, q_g, k_g, n_heads, n_kv):
    b, t = h.shape[:2]
    q, k, v = jnp.split(h @ w_qkv, [n_heads * HEAD_DIM, (n_heads + n_kv) * HEAD_DIM], axis=-1)
    q = rms_norm(q.reshape(b, t, n_kv, n_heads // n_kv, HEAD_DIM), q_g)
    k = rms_norm(k.reshape(b, t, n_kv, HEAD_DIM), k_g)
    return q, k, v.reshape(b, t, n_kv, HEAD_DIM)


def gqa_scores(q, k):
    return jnp.einsum('bqhgd,bkhd->bhgqk', q, k, preferred_element_type=jnp.float32) / math.sqrt(q.shape[-1])


def gqa_attend(q, k, v):
    p = jax.nn.softmax(gqa_scores(q, k), axis=-1)
    return jnp.einsum('bhgqk,bkhd->bqhgd', p.astype(v.dtype), v)


def mixer_dense_gqa(h, hc, w_qkv, q_g, k_g, w_o, cos, sin, need_ctx):
    b, t = h.shape[:2]
    q, k, v = gqa_project(h, w_qkv, q_g, k_g, GQA_HEADS, GQA_KV_HEADS)
    q, k = apply_rope(q, cos, sin), apply_rope(k, cos, sin)
    qc, kc, vc = gqa_project(hc, w_qkv, q_g, k_g, GQA_HEADS, GQA_KV_HEADS)
    k_all = jnp.concatenate([kc, k], axis=1)
    v_all = jnp.concatenate([vc, v], axis=1)
    o = _from_blocks(lax.map(lambda qb: gqa_attend(qb, k_all, v_all), _to_blocks(q)))
    y = o.reshape(b, t, -1) @ w_o
    yc = gqa_attend(qc, kc, vc).reshape(b, hc.shape[1], -1) @ w_o if need_ctx else None
    return y, yc


def conformer_conv(h, w_pw1, b_pw1, w_dw, b_dw, ln_g, ln_b, w_pw2, b_pw2):
    a, g = jnp.split(h @ w_pw1 + b_pw1, 2, axis=-1)
    u = a * jax.nn.sigmoid(g)
    u = lax.conv_general_dilated(u, w_dw[:, None, :].astype(u.dtype), window_strides=(1,),
                                 padding=[(CONV_WIDTH // 2, CONV_WIDTH // 2)],
                                 dimension_numbers=('NWC', 'WIO', 'NWC'),
                                 feature_group_count=u.shape[-1]) + b_dw
    u = jax.nn.silu(layer_norm(u, ln_g, ln_b))
    return u @ w_pw2 + b_pw2


def mixer_conformer(h, hc, w_pw1, b_pw1, w_dw, b_dw, ln_g, ln_b, w_pw2, b_pw2, need_ctx):
    y = conformer_conv(h, w_pw1, b_pw1, w_dw, b_dw, ln_g, ln_b, w_pw2, b_pw2)
    yc = conformer_conv(hc, w_pw1, b_pw1, w_dw, b_dw, ln_g, ln_b, w_pw2, b_pw2) if need_ctx else None
    return y, yc


def diff_project(h, w_qkv, q_g, k_g):
    b, t = h.shape[:2]
    q, k, v = jnp.split(h @ w_qkv, 3, axis=-1)
    q = rms_norm(q.reshape(b, t, DIFF_HEADS, 2, HEAD_DIM), q_g)
    k = rms_norm(k.reshape(b, t, DIFF_HEADS, 2, HEAD_DIM), k_g)
    return q, k, v.reshape(b, t, DIFF_HEADS, DIFF_V_DIM)


def diff_attend(q, k, v, lam):
    s = jnp.einsum('bqhcd,bkhcd->bchqk', q, k, preferred_element_type=jnp.float32) / math.sqrt(HEAD_DIM)
    p = jax.nn.softmax(s, axis=-1)
    a = p[:, 0] - lam * p[:, 1]
    return jnp.einsum('bhqk,bkhd->bqhd', a.astype(v.dtype), v)


def mixer_diff(h, hc, w_qkv, q_g, k_g, lq1, lk1, lq2, lk2, subln_g, w_o, lam_init, cos, sin, need_ctx):
    b, t = h.shape[:2]
    f32 = jnp.float32
    lam = (jnp.exp(jnp.sum(lq1.astype(f32) * lk1.astype(f32)))
           - jnp.exp(jnp.sum(lq2.astype(f32) * lk2.astype(f32))) + lam_init)
    q, k, v = diff_project(h, w_qkv, q_g, k_g)
    q, k = apply_rope(q, cos, sin), apply_rope(k, cos, sin)
    qc, kc, vc = diff_project(hc, w_qkv, q_g, k_g)
    k_all = jnp.concatenate([kc, k], axis=1)
    v_all = jnp.concatenate([vc, v], axis=1)

    def out(o):
        return (rms_norm(o, subln_g) * (1.0 - lam_init)).reshape(o.shape[0], o.shape[1], -1) @ w_o

    o = _from_blocks(lax.map(lambda qb: diff_attend(qb, k_all, v_all, lam), _to_blocks(q)))
    y = out(o)
    yc = out(diff_attend(qc, kc, vc, lam)) if need_ctx else None
    return y, yc


def mixer_window_gqa(h, hc, w_qkv, q_g, k_g, sink, w_o, cos, sin, need_ctx):
    b, t = h.shape[:2]
    q, k, v = gqa_project(h, w_qkv, q_g, k_g, SWA_HEADS, SWA_KV_HEADS)
    q, k = apply_rope(q, cos, sin), apply_rope(k, cos, sin)
    qc, kc, vc = gqa_project(hc, w_qkv, q_g, k_g, SWA_HEADS, SWA_KV_HEADS)
    sink_logit = sink.astype(jnp.float32).reshape(1, SWA_KV_HEADS, SWA_HEADS // SWA_KV_HEADS, 1, 1)
    span = Q_BLOCK + 2 * WINDOW
    pad = ((0, 0), (WINDOW, WINDOW), (0, 0), (0, 0))
    k_pad, v_pad = jnp.pad(k, pad), jnp.pad(v, pad)
    rel = jnp.arange(span)[None, :] - WINDOW - jnp.arange(Q_BLOCK)[:, None]
    in_band = jnp.abs(rel) <= WINDOW
    n_ctx = kc.shape[1]

    def with_sink(s):
        col = jnp.broadcast_to(sink_logit, s.shape[:-1] + (1,))
        return jax.nn.softmax(jnp.concatenate([s, col], axis=-1), axis=-1)[..., :-1]

    def block(args):
        i, qb = args
        start = i * Q_BLOCK
        kb = lax.dynamic_slice_in_dim(k_pad, start, span, axis=1)
        vb = lax.dynamic_slice_in_dim(v_pad, start, span, axis=1)
        kpos = start - WINDOW + jnp.arange(span)
        valid = in_band & ((kpos >= 0) & (kpos < t))[None, :]
        s = jnp.concatenate([jnp.where(valid, gqa_scores(qb, kb), NEG_INF), gqa_scores(qb, kc)], axis=-1)
        p = with_sink(s).astype(v.dtype)
        return (jnp.einsum('bhgqk,bkhd->bqhgd', p[..., :span], vb)
                + jnp.einsum('bhgqk,bkhd->bqhgd', p[..., span:span + n_ctx], vc))

    o = _from_blocks(lax.map(block, (jnp.arange(t // Q_BLOCK), _to_blocks(q))))
    y = o.reshape(b, t, -1) @ w_o
    yc = None
    if need_ctx:
        pc = with_sink(gqa_scores(qc, kc)).astype(vc.dtype)
        yc = jnp.einsum('bhgqk,bkhd->bqhgd', pc, vc).reshape(b, n_ctx, -1) @ w_o
    return y, yc


def squared_relu_mlp(h, w_up, w_down):
    return jnp.square(jax.nn.relu(h @ w_up)) @ w_down


def setup_inputs(seed: int = 0) -> dict:
    key = jax.random.key(seed)
    ks = iter(jax.random.split(key, 48))
    D = D_MODEL

    def nrm(shape, scale):
        return jax.random.normal(next(ks), shape, jnp.float32) * scale

    def gain(shape):
        return 1.0 + nrm(shape, 0.02)

    nA, nB, nC, nD = (_n_layers_of(m) for m in range(N_MIXERS))
    gqa_w = (GQA_HEADS + 2 * GQA_KV_HEADS) * HEAD_DIM
    swa_w = (SWA_HEADS + 2 * SWA_KV_HEADS) * HEAD_DIM
    return {
        'x': nrm((BATCH, SEQ, D), 1.0),
        'c': nrm((BATCH, D), 1.0),
        'ctx': nrm((BATCH, CTX_LEN, D), 1.0),
        'c_ctx': nrm((D,), 1.0),
        'norm1_g': gain((DEPTH, D)),
        'norm2_g': gain((DEPTH, D)),
        'mod_w': nrm((DEPTH, D, 6 * D), 0.5 * D ** -0.5),
        'mod_b': nrm((DEPTH, 6 * D), 0.02),
        'mlp_up': nrm((DEPTH, D, D_FF), D ** -0.5),
        'mlp_down': nrm((DEPTH, D_FF, D), D_FF ** -0.5),
        'gqa_w_qkv': nrm((nA, D, gqa_w), D ** -0.5),
        'gqa_q_g': gain((nA, HEAD_DIM)),
        'gqa_k_g': gain((nA, HEAD_DIM)),
        'gqa_w_o': nrm((nA, GQA_HEADS * HEAD_DIM, D), (GQA_HEADS * HEAD_DIM) ** -0.5),
        'conv_w_pw1': nrm((nB, D, 2 * D), D ** -0.5),
        'conv_b_pw1': nrm((nB, 2 * D), 0.02),
        'conv_w_dw': nrm((nB, CONV_WIDTH, D), CONV_WIDTH ** -0.5),
        'conv_b_dw': nrm((nB, D), 0.02),
        'conv_ln_g': gain((nB, D)),
        'conv_ln_b': nrm((nB, D), 0.02),
        'conv_w_pw2': nrm((nB, D, D), D ** -0.5),
        'conv_b_pw2': nrm((nB, D), 0.02),
        'diff_w_qkv': nrm((nC, D, 3 * DIFF_HEADS * DIFF_V_DIM), D ** -0.5),
        'diff_q_g': gain((nC, HEAD_DIM)),
        'diff_k_g': gain((nC, HEAD_DIM)),
        'diff_lam_q1': nrm((nC, HEAD_DIM), 0.1),
        'diff_lam_k1': nrm((nC, HEAD_DIM), 0.1),
        'diff_lam_q2': nrm((nC, HEAD_DIM), 0.1),
        'diff_lam_k2': nrm((nC, HEAD_DIM), 0.1),
        'diff_subln_g': gain((nC, DIFF_V_DIM)),
        'diff_w_o': nrm((nC, DIFF_HEADS * DIFF_V_DIM, D), (DIFF_HEADS * DIFF_V_DIM) ** -0.5),
        'swa_w_qkv': nrm((nD, D, swa_w), D ** -0.5),
        'swa_q_g': gain((nD, HEAD_DIM)),
        'swa_k_g': gain((nD, HEAD_DIM)),
        'swa_sink': nrm((nD, SWA_HEADS), 0.5),
        'swa_w_o': nrm((nD, SWA_HEADS * HEAD_DIM, D), (SWA_HEADS * HEAD_DIM) ** -0.5),
    }


def reference(x, c, ctx, c_ctx, norm1_g, norm2_g, mod_w, mod_b, mlp_up, mlp_down,
              gqa_w_qkv, gqa_q_g, gqa_k_g, gqa_w_o,
              conv_w_pw1, conv_b_pw1, conv_w_dw, conv_b_dw, conv_ln_g, conv_ln_b, conv_w_pw2, conv_b_pw2,
              diff_w_qkv, diff_q_g, diff_k_g, diff_lam_q1, diff_lam_k1, diff_lam_q2, diff_lam_k2,
              diff_subln_g, diff_w_o,
              swa_w_qkv, swa_q_g, swa_k_g, swa_sink, swa_w_o):
    rows = x.shape[1] // GRID_W
    cos, sin = rope_2d(rows, HEAD_DIM)
    s_lat = jax.nn.silu(c)
    s_ctx = jax.nn.silu(c_ctx)
    xc = ctx
    for i in range(DEPTH):
        m, j = i % N_MIXERS, i // N_MIXERS
        need_ctx = i < DEPTH - 1
        mod_l = (s_lat @ mod_w[i] + mod_b[i])[:, None, :]
        mod_c = (s_ctx @ mod_w[i] + mod_b[i])[None, None, :]
        sh1, sc1, g1, sh2, sc2, g2 = jnp.split(mod_l, 6, axis=-1)
        csh1, csc1, cg1, csh2, csc2, cg2 = jnp.split(mod_c, 6, axis=-1)
        h = modulate(x, norm1_g[i], sh1, sc1)
        hc = modulate(xc, norm1_g[i], csh1, csc1)
        if m == 0:
            y, yc = mixer_dense_gqa(h, hc, gqa_w_qkv[j], gqa_q_g[j], gqa_k_g[j], gqa_w_o[j], cos, sin, need_ctx)
        elif m == 1:
            y, yc = mixer_conformer(h, hc, conv_w_pw1[j], conv_b_pw1[j], conv_w_dw[j], conv_b_dw[j],
                                    conv_ln_g[j], conv_ln_b[j], conv_w_pw2[j], conv_b_pw2[j], need_ctx)
        elif m == 2:
            lam_init = 0.8 - 0.6 * math.exp(-0.3 * i)
            y, yc = mixer_diff(h, hc, diff_w_qkv[j], diff_q_g[j], diff_k_g[j], diff_lam_q1[j], diff_lam_k1[j],
                               diff_lam_q2[j], diff_lam_k2[j], diff_subln_g[j], diff_w_o[j], lam_init,
                               cos, sin, need_ctx)
        else:
            y, yc = mixer_window_gqa(h, hc, swa_w_qkv[j], swa_q_g[j], swa_k_g[j], swa_sink[j], swa_w_o[j],
                                     cos, sin, need_ctx)
        x = x + g1 * y
        x = x + g2 * squared_relu_mlp(modulate(x, norm2_g[i], sh2, sc2), mlp_up[i], mlp_down[i])
        if need_ctx:
            xc = xc + cg1 * yc
            xc = xc + cg2 * squared_relu_mlp(modulate(xc, norm2_g[i], csh2, csc2), mlp_up[i], mlp_down[i])
    return x
```

```python
import functools
import math

import jax
import jax.numpy as jnp
from jax import lax
from jax.experimental import pallas as pl
from jax.experimental.pallas import tpu as pltpu

F32 = jnp.float32
BF16 = jnp.bfloat16

GRID_W = 64
HEAD_DIM = 64
N_MIXERS = 4
CONV_WIDTH = 31
WINDOW = 128
ROPE_THETA = 10000.0
EPS = 1e-6
NEG_INF = -1e30

LANES = 128
MXU_DIM = 256
VMEM_LIMIT_BYTES = 56 * 1024 * 1024

CONV_HALO = 16
CONV_ROWS = 32


def _params(n_axes):
    return pltpu.CompilerParams(dimension_semantics=("parallel",) * n_axes,
                                vmem_limit_bytes=VMEM_LIMIT_BYTES)


def _resident(shape):
    nd = len(shape)
    return pl.BlockSpec(shape, lambda *_: (0,) * nd, pipeline_mode=pl.Buffered(1))


def _pick(n, candidates):
    for cand in candidates:
        if n % cand == 0:
            return cand
    return n


def _modulate(x, gain, shift, scale):
    ms = jnp.mean(x * x, axis=-1, keepdims=True)
    return (x * lax.rsqrt(ms + EPS) * gain) * (1.0 + scale) + shift


def _dot(a, b):
    return jnp.dot(a, b, preferred_element_type=F32)


def _dot_nt(a, b):
    return lax.dot_general(a, b, (((1,), (1,)), ((), ())), preferred_element_type=F32)


def _mod_kernel(cv_ref, w_ref, b_ref, o_ref):
    cv = cv_ref[...]
    s = cv * jax.nn.sigmoid(cv)
    o_ref[0] = _dot(s.astype(BF16), w_ref[0].astype(BF16)) + b_ref[0]


def _mod_call(cv, mod_w, mod_b):
    depth, d, n = mod_w.shape
    rows = cv.shape[0]
    tn = _pick(n, (1536, 1024, 512))
    return pl.pallas_call(
        _mod_kernel,
        out_shape=jax.ShapeDtypeStruct((depth, rows, n), F32),
        grid=(depth, n // tn),
        in_specs=[pl.BlockSpec((rows, d), lambda i, j: (0, 0)),
                  pl.BlockSpec((1, d, tn), lambda i, j: (i, 0, j)),
                  pl.BlockSpec((1, 1, tn), lambda i, j: (i, 0, j))],
        out_specs=pl.BlockSpec((1, rows, tn), lambda i, j: (i, 0, j)),
        compiler_params=_params(2),
        name="modulation",
    )(cv, mod_w, mod_b.reshape(depth, 1, n))


def _proj_kernel(*refs, n_q, n_k, q_half, use_rope):
    if use_rope:
        (x_ref, mod_ref, g_ref, w_ref, hg_ref, e_ref, cos_ref, sin_ref, q_ref, k_ref, v_ref) = refs
    else:
        (x_ref, mod_ref, g_ref, w_ref, hg_ref, e_ref, q_ref, k_ref, v_ref) = refs
    x = x_ref[0]
    mod = mod_ref[0]
    tm = x.shape[0]
    h = _modulate(x, g_ref[...], mod[0:1], mod[1:2]).astype(BF16)
    proj = _dot(h, w_ref[...])
    n_qk = n_q + n_k
    qk = proj[:, :n_qk]
    sq = qk * qk
    hi = sq.astype(BF16)
    lo = (sq - hi.astype(F32)).astype(BF16)
    ones_bd = e_ref[...]
    ss = jnp.concatenate(
        [_dot(hi[:, t:t + MXU_DIM], ones_bd) + _dot(lo[:, t:t + MXU_DIM], ones_bd)
         for t in range(0, n_qk, MXU_DIM)], axis=1)
    qk = qk * lax.rsqrt(ss * (1.0 / HEAD_DIM) + EPS) * hg_ref[...]

    lane = lax.broadcasted_iota(jnp.int32, (tm, LANES), 1)
    tiles = [qk[:, t:t + LANES] for t in range(0, n_qk, LANES)]
    if use_rope:
        first = (lane % HEAD_DIM) < (HEAD_DIM // 2)
        cos = cos_ref[...]
        sin = sin_ref[...]
        tiles = [t * cos + jnp.where(first, pltpu.roll(t, LANES - HEAD_DIM // 2, axis=1),
                                     pltpu.roll(t, HEAD_DIM // 2, axis=1)) * sin
                 for t in tiles]
    low = lane < HEAD_DIM
    for j in range(n_q // HEAD_DIM):
        t = tiles[j // 2]
        if j % 2 != q_half[j]:
            t = pltpu.roll(t, HEAD_DIM, axis=1)
        keep = low if q_half[j] == 0 else jnp.logical_not(low)
        q_ref[0, j] = jnp.where(keep, t, 0.0).astype(BF16)
    k_ref[0] = jnp.concatenate(tiles[n_q // LANES:], axis=1).astype(BF16)
    v_ref[0] = proj[:, n_qk:].astype(BF16)


def _proj_call(x, mod, gain, w, head_gain, ones_bd, rope, *, n_q, n_k, q_half, tm):
    b, t, d = x.shape
    n_tot = w.shape[1]
    n_v = n_tot - n_q - n_k
    hq = n_q // HEAD_DIM
    use_rope = rope is not None
    in_specs = [pl.BlockSpec((1, tm, d), lambda bi, j: (bi, j, 0)),
                pl.BlockSpec((1, 6, d), lambda bi, j: (bi, 0, 0)),
                _resident((1, d)),
                _resident((d, n_tot)),
                _resident((1, n_q + n_k)),
                _resident((MXU_DIM, MXU_DIM))]
    args = [x, mod, gain, w, head_gain, ones_bd]
    if use_rope:
        in_specs += [pl.BlockSpec((tm, LANES), lambda bi, j: (j, 0)),
                     pl.BlockSpec((tm, LANES), lambda bi, j: (j, 0))]
        args += list(rope)
    return pl.pallas_call(
        functools.partial(_proj_kernel, n_q=n_q, n_k=n_k, q_half=q_half, use_rope=use_rope),
        out_shape=(jax.ShapeDtypeStruct((b, hq, t, LANES), BF16),
                   jax.ShapeDtypeStruct((b, t, n_k), BF16),
                   jax.ShapeDtypeStruct((b, t, n_v), BF16)),
        grid=(b, t // tm),
        in_specs=in_specs,
        out_specs=(pl.BlockSpec((1, hq, tm, LANES), lambda bi, j: (bi, 0, j, 0)),
                   pl.BlockSpec((1, tm, n_k), lambda bi, j: (bi, j, 0)),
                   pl.BlockSpec((1, tm, n_v), lambda bi, j: (bi, j, 0))),
        compiler_params=_params(2),
        name="attn_proj",
    )(*args)


def _flash_loop(q_all, k_ref, v_ref, m_ref, l_ref, acc_ref, *, n_kv, tk):
    m_ref[...] = jnp.full(m_ref.shape, NEG_INF, F32)
    l_ref[...] = jnp.zeros(l_ref.shape, F32)
    acc_ref[...] = jnp.zeros(acc_ref.shape, F32)

    def step(j, carry):
        start = pl.multiple_of(j * tk, tk)
        kc = k_ref[0, pl.ds(start, tk), :]
        vc = v_ref[0, pl.ds(start, tk), :]
        s = _dot_nt(q_all, kc)
        m_prev = m_ref[...]
        m_new = jnp.maximum(m_prev, jnp.max(s, axis=1, keepdims=True))
        alpha = jnp.exp(m_prev - m_new)
        p = jnp.exp(s - m_new)
        l_ref[...] = alpha * l_ref[...] + jnp.sum(p, axis=1, keepdims=True)
        acc_ref[...] = alpha * acc_ref[...] + _dot(p.astype(BF16), vc)
        m_ref[...] = m_new
        return carry

    lax.fori_loop(0, n_kv, step, 0)


def _store_head_pairs(o_ref, o, par, tq, group):
    low = lax.broadcasted_iota(jnp.int32, (tq, LANES), 1) < HEAD_DIM
    for gg in range(group // 2):
        a = o[(2 * gg) * tq:(2 * gg + 1) * tq]
        b = o[(2 * gg + 1) * tq:(2 * gg + 2) * tq]
        if par == 0:
            col = jnp.where(low, a, pltpu.roll(b, HEAD_DIM, axis=1))
        else:
            col = jnp.where(low, pltpu.roll(a, HEAD_DIM, axis=1), b)
        c0 = (par * (group // 2) + gg) * LANES
        o_ref[0, :, c0:c0 + LANES] = col.astype(o_ref.dtype)


def _gqa_kernel(q_ref, k_ref, v_ref, o_ref, m_ref, l_ref, acc_ref, *, group, n_kv, tk):
    tq = q_ref.shape[2]
    for par in range(2):
        q_all = q_ref[0, par * group:(par + 1) * group].reshape(group * tq, LANES)
        _flash_loop(q_all, k_ref, v_ref, m_ref, l_ref, acc_ref, n_kv=n_kv, tk=tk)
        o = acc_ref[...] / l_ref[...]
        _store_head_pairs(o_ref, o, par, tq, group)


def _gqa_call(q, k_all, v_all, *, group, tq, tk):
    b, hq, t, _ = q.shape
    t_all, n_k = k_all.shape[1:]
    pairs = n_k // LANES
    heads = 2 * group
    return pl.pallas_call(
        functools.partial(_gqa_kernel, group=group, n_kv=t_all // tk, tk=tk),
        out_shape=jax.ShapeDtypeStruct((b, t, hq * HEAD_DIM), BF16),
        grid=(b, pairs, t // tq),
        in_specs=[pl.BlockSpec((1, heads, tq, LANES), lambda bi, p, i: (bi, p, i, 0)),
                  pl.BlockSpec((1, t_all, LANES), lambda bi, p, i: (bi, 0, p)),
                  pl.BlockSpec((1, t_all, LANES), lambda bi, p, i: (bi, 0, p))],
        out_specs=pl.BlockSpec((1, tq, heads * HEAD_DIM), lambda bi, p, i: (bi, i, p)),
        scratch_shapes=[pltpu.VMEM((group * tq, 1), F32),
                        pltpu.VMEM((group * tq, 1), F32),
                        pltpu.VMEM((group * tq, LANES), F32)],
        compiler_params=_params(3),
        name="gqa_attention",
    )(q, k_all, v_all)


def _diff_kernel(q_ref, k_ref, v_ref, lam_ref, sg_ref, o_ref, m_ref, l_ref, acc_ref, *,
                 lam_init, n_kv, tk):
    tq = q_ref.shape[2]
    q_all = q_ref[0].reshape(2 * tq, LANES)
    _flash_loop(q_all, k_ref, v_ref, m_ref, l_ref, acc_ref, n_kv=n_kv, tk=tk)
    o = acc_ref[...] / l_ref[...]
    lv = lam_ref[...]
    lam = (jnp.exp(jnp.sum(lv[0:1] * lv[1:2], axis=1, keepdims=True))
           - jnp.exp(jnp.sum(lv[2:3] * lv[3:4], axis=1, keepdims=True)) + lam_init)
    o = o[:tq] - lam * o[tq:]
    ms = jnp.mean(o * o, axis=-1, keepdims=True)
    o = (o * lax.rsqrt(ms + EPS) * sg_ref[...]) * (1.0 - lam_init)
    o_ref[0] = o.astype(o_ref.dtype)


def _diff_call(q, k_all, v_all, lam_vecs, subln_g, *, lam_init, tq, tk):
    b, hq, t, _ = q.shape
    t_all, n_k = k_all.shape[1:]
    heads = n_k // LANES
    return pl.pallas_call(
        functools.partial(_diff_kernel, lam_init=lam_init, n_kv=t_all // tk, tk=tk),
        out_shape=jax.ShapeDtypeStruct((b, t, heads * LANES), BF16),
        grid=(b, heads, t // tq),
        in_specs=[pl.BlockSpec((1, 2, tq, LANES), lambda bi, h, i: (bi, h, i, 0)),
                  pl.BlockSpec((1, t_all, LANES), lambda bi, h, i: (bi, 0, h)),
                  pl.BlockSpec((1, t_all, LANES), lambda bi, h, i: (bi, 0, h)),
                  pl.BlockSpec((4, HEAD_DIM), lambda bi, h, i: (0, 0)),
                  pl.BlockSpec((1, LANES), lambda bi, h, i: (0, 0))],
        out_specs=pl.BlockSpec((1, tq, LANES), lambda bi, h, i: (bi, i, h)),
        scratch_shapes=[pltpu.VMEM((2 * tq, 1), F32),
                        pltpu.VMEM((2 * tq, 1), F32),
                        pltpu.VMEM((2 * tq, LANES), F32)],
        compiler_params=_params(3),
        name="diff_attention",
    )(q, k_all, v_all, lam_vecs, subln_g)


def _window_kernel(q_ref, k_ref, v_ref, sink_ref, o_ref, *, group, n_ctx, t_lat):
    tq = q_ref.shape[2]
    span = tq + 2 * WINDOW
    i = pl.program_id(2)
    start = jnp.clip(i * tq - WINDOW, 0, t_lat - span)
    row0 = pl.multiple_of(n_ctx + start, math.gcd(WINDOW, n_ctx))
    k_lat = k_ref[0, pl.ds(row0, span), :]
    v_lat = v_ref[0, pl.ds(row0, span), :]
    k_ctx = k_ref[0, 0:n_ctx, :]
    v_ctx = v_ref[0, 0:n_ctx, :]
    qpos = i * tq + lax.broadcasted_iota(jnp.int32, (tq, span), 0)
    kpos = start + lax.broadcasted_iota(jnp.int32, (tq, span), 1)
    in_band = (jnp.abs(kpos - qpos) <= WINDOW)[None]
    for par in range(2):
        q_all = q_ref[0, par * group:(par + 1) * group].reshape(group * tq, LANES)
        s_lat = jnp.where(in_band, _dot_nt(q_all, k_lat).reshape(group, tq, span), NEG_INF)
        s_ctx = _dot_nt(q_all, k_ctx).reshape(group, tq, n_ctx)
        sink = sink_ref[par * group:(par + 1) * group][:, :, 0:1]
        m = jnp.maximum(jnp.maximum(jnp.max(s_lat, axis=-1, keepdims=True),
                                    jnp.max(s_ctx, axis=-1, keepdims=True)), sink)
        e_lat = jnp.exp(s_lat - m)
        e_ctx = jnp.exp(s_ctx - m)
        den = (jnp.sum(e_lat, axis=-1, keepdims=True) + jnp.sum(e_ctx, axis=-1, keepdims=True)
               + jnp.exp(sink - m))
        pv = (_dot(e_lat.astype(BF16).reshape(group * tq, span), v_lat)
              + _dot(e_ctx.astype(BF16).reshape(group * tq, n_ctx), v_ctx))
        o = pv / den.reshape(group * tq, 1)
        _store_head_pairs(o_ref, o, par, tq, group)


def _window_call(q, k_all, v_all, sink, *, group, n_ctx, tq):
    b, hq, t, _ = q.shape
    t_all, n_k = k_all.shape[1:]
    pairs = n_k // LANES
    heads = 2 * group
    return pl.pallas_call(
        functools.partial(_window_kernel, group=group, n_ctx=n_ctx, t_lat=t),
        out_shape=jax.ShapeDtypeStruct((b, t, hq * HEAD_DIM), BF16),
        grid=(b, pairs, t // tq),
        in_specs=[pl.BlockSpec((1, heads, tq, LANES), lambda bi, p, i: (bi, p, i, 0)),
                  pl.BlockSpec((1, t_all, LANES), lambda bi, p, i: (bi, 0, p)),
                  pl.BlockSpec((1, t_all, LANES), lambda bi, p, i: (bi, 0, p)),
                  pl.BlockSpec((heads, 1, LANES), lambda bi, p, i: (p, 0, 0))],
        out_specs=pl.BlockSpec((1, tq, heads * HEAD_DIM), lambda bi, p, i: (bi, i, p)),
        compiler_params=_params(3),
        name="window_attention",
    )(q, k_all, v_all, sink)


def _glu_kernel(x_ref, mod_ref, g_ref, w_ref, b_ref, u_ref):
    x = x_ref[0]
    mod = mod_ref[0]
    d = x.shape[1]
    h = _modulate(x, g_ref[...], mod[0:1], mod[1:2]).astype(BF16)
    ag = _dot(h, w_ref[...]) + b_ref[...]
    u_ref[0] = (ag[:, :d] * jax.nn.sigmoid(ag[:, d:])).astype(u_ref.dtype)


def _glu_call(x, mod, gain, w, bias, *, tm):
    b, t, d = x.shape
    return pl.pallas_call(
        _glu_kernel,
        out_shape=jax.ShapeDtypeStruct((b, t, d), BF16),
        grid=(b, t // tm),
        in_specs=[pl.BlockSpec((1, tm, d), lambda bi, j: (bi, j, 0)),
                  pl.BlockSpec((1, 6, d), lambda bi, j: (bi, 0, 0)),
                  _resident((1, d)),
                  _resident((d, 2 * d)),
                  _resident((1, 2 * d))],
        out_specs=pl.BlockSpec((1, tm, d), lambda bi, j: (bi, j, 0)),
        compiler_params=_params(2),
        name="conv_glu",
    )(x, mod, gain, w, bias)


def _depthwise_conv(ubuf_ref, cbuf_ref, wdw_ref, bdw_ref, tm, d):
    win_rows = CONV_ROWS + 2 * CONV_HALO
    base = CONV_HALO - CONV_WIDTH // 2
    lane_chunk = 2 * LANES

    def step(r, carry):
        r0 = pl.multiple_of(r * CONV_ROWS, CONV_ROWS)
        for c0 in range(0, d, lane_chunk):
            win = ubuf_ref[pl.ds(r0, win_rows), c0:c0 + lane_chunk]
            acc = jnp.zeros((CONV_ROWS, lane_chunk), F32) + bdw_ref[:, c0:c0 + lane_chunk]
            for sub in range(8):
                shifted = win if sub == 0 else pltpu.roll(win, win_rows - sub, axis=0)
                for al in range(0, win_rows - CONV_ROWS + 1, 8):
                    k = al + sub - base
                    if 0 <= k < CONV_WIDTH:
                        acc = acc + wdw_ref[k:k + 1, c0:c0 + lane_chunk] * shifted[al:al + CONV_ROWS]
            cbuf_ref[pl.ds(r0, CONV_ROWS), c0:c0 + lane_chunk] = acc
        return carry

    lax.fori_loop(0, tm // CONV_ROWS, step, 0)


def _post_kernel(*refs, conv, has_bias, d_ff_chunk):
    it = iter(refs)
    x_ref = next(it)
    if conv:
        up_ref, uc_ref, un_ref = next(it), next(it), next(it)
        wdw_ref, bdw_ref, lng_ref, lnb_ref = next(it), next(it), next(it), next(it)
    else:
        o_ref = next(it)
    mod_ref, wo_ref = next(it), next(it)
    bo_ref = next(it) if has_bias else None
    g2_ref, wup_ref, wdn_ref, out_ref = next(it), next(it), next(it), next(it)
    if conv:
        ubuf_ref, cbuf_ref = next(it), next(it)

    x = x_ref[0]
    mod = mod_ref[0]
    tm, d = x.shape
    if conv:
        j = pl.program_id(1)
        prev = jnp.where(j > 0, up_ref[0].astype(F32), 0.0)
        nxt = jnp.where(j < pl.num_programs(1) - 1, un_ref[0].astype(F32), 0.0)
        ubuf_ref[0:CONV_HALO] = prev
        ubuf_ref[CONV_HALO:CONV_HALO + tm] = uc_ref[0].astype(F32)
        ubuf_ref[CONV_HALO + tm:] = nxt
        _depthwise_conv(ubuf_ref, cbuf_ref, wdw_ref, bdw_ref, tm, d)
        cv = cbuf_ref[...]
        mu = jnp.mean(cv, axis=-1, keepdims=True)
        cc = cv - mu
        var = jnp.mean(cc * cc, axis=-1, keepdims=True)
        yn = cc * lax.rsqrt(var + EPS) * lng_ref[...] + lnb_ref[...]
        o = (yn * jax.nn.sigmoid(yn)).astype(BF16)
    else:
        o = o_ref[0]
    y = _dot(o, wo_ref[...])
    if has_bias:
        y = y + bo_ref[...]
    x1 = x + mod[2:3] * y
    h2 = _modulate(x1, g2_ref[...], mod[3:4], mod[4:5]).astype(BF16)
    acc = jnp.zeros((tm, d), F32)
    d_ff = wup_ref.shape[1]
    for c0 in range(0, d_ff, d_ff_chunk):
        up = jnp.maximum(_dot(h2, wup_ref[:, c0:c0 + d_ff_chunk]), 0.0)
        acc = acc + _dot((up * up).astype(BF16), wdn_ref[c0:c0 + d_ff_chunk, :])
    out_ref[0] = x1 + mod[5:6] * acc


def _post_call(x, mixed, mod, w_o, b_o, gain2, w_up, w_down, conv_params, *, tm):
    b, t, d = x.shape
    d_in = w_o.shape[0]
    d_ff = w_up.shape[1]
    conv = conv_params is not None
    has_bias = b_o is not None
    tok = lambda bi, j: (bi, j, 0)
    in_specs = [pl.BlockSpec((1, tm, d), tok)]
    args = [x]
    scratch = []
    if conv:
        hb = tm // CONV_HALO
        last = t // CONV_HALO - 1
        in_specs += [pl.BlockSpec((1, CONV_HALO, d), lambda bi, j: (bi, jnp.maximum(j * hb - 1, 0), 0)),
                     pl.BlockSpec((1, tm, d), tok),
                     pl.BlockSpec((1, CONV_HALO, d), lambda bi, j: (bi, jnp.minimum((j + 1) * hb, last), 0)),
                     _resident((CONV_WIDTH, d)), _resident((1, d)), _resident((1, d)), _resident((1, d))]
        args += [mixed, mixed, mixed] + list(conv_params)
        scratch = [pltpu.VMEM((tm + 2 * CONV_HALO, d), F32), pltpu.VMEM((tm, d), F32)]
    else:
        in_specs += [pl.BlockSpec((1, tm, d_in), tok)]
        args += [mixed]
    in_specs += [pl.BlockSpec((1, 6, d), lambda bi, j: (bi, 0, 0)), _resident((d_in, d))]
    args += [mod, w_o]
    if has_bias:
        in_specs += [_resident((1, d))]
        args += [b_o]
    in_specs += [_resident((1, d)), _resident((d, d_ff)), _resident((d_ff, d))]
    args += [gain2, w_up, w_down]
    return pl.pallas_call(
        functools.partial(_post_kernel, conv=conv, has_bias=has_bias, d_ff_chunk=min(d_ff, 1024)),
        out_shape=jax.ShapeDtypeStruct((b, t, d), F32),
        grid=(b, t // tm),
        in_specs=in_specs,
        out_specs=pl.BlockSpec((1, tm, d), tok),
        scratch_shapes=scratch,
        compiler_params=_params(2),
        name="post_mlp",
    )(*args)


def _rope_tables(t):
    rows = t // GRID_W
    row = jnp.repeat(jnp.arange(rows, dtype=F32), GRID_W)
    col = jnp.tile(jnp.arange(GRID_W, dtype=F32), rows)
    half = HEAD_DIM // 2
    inv = 1.0 / jnp.power(ROPE_THETA, jnp.arange(0, half, 2, dtype=F32) / half)
    ang = jnp.concatenate([row[:, None] * inv, col[:, None] * inv], axis=-1)
    cos, sin = jnp.cos(ang), jnp.sin(ang)
    reps = LANES // HEAD_DIM
    return (jnp.tile(jnp.concatenate([cos, cos], axis=1), (1, reps)),
            jnp.tile(jnp.concatenate([-sin, sin], axis=1), (1, reps)))


def _split_pairs_perm(n_heads):
    within = jnp.concatenate([jnp.arange(0, HEAD_DIM, 2), jnp.arange(1, HEAD_DIM, 2)])
    return (jnp.arange(n_heads)[:, None] * HEAD_DIM + within[None, :]).reshape(-1)


def _prep_qkv(w_qkv, q_g, k_g, n_q, n_k):
    hq, hk = n_q // HEAD_DIM, n_k // HEAD_DIM
    perm = jnp.concatenate([_split_pairs_perm(hq + hk), jnp.arange(n_q + n_k, w_qkv.shape[1])])
    w = jnp.take(w_qkv, perm, axis=1).astype(BF16)
    within = perm[:HEAD_DIM]
    scale = 1.0 / math.sqrt(HEAD_DIM)
    head_gain = jnp.concatenate([jnp.tile(q_g[within] * scale, hq), jnp.tile(k_g[within], hk)])
    return w, head_gain.reshape(1, n_q + n_k).astype(F32)


def _ones_block_diag():
    idx = jnp.arange(MXU_DIM) // HEAD_DIM
    return (idx[:, None] == idx[None, :]).astype(BF16)


def kernel(x, c, ctx, c_ctx, norm1_g, norm2_g, mod_w, mod_b, mlp_up, mlp_down,
           gqa_w_qkv, gqa_q_g, gqa_k_g, gqa_w_o,
           conv_w_pw1, conv_b_pw1, conv_w_dw, conv_b_dw, conv_ln_g, conv_ln_b, conv_w_pw2, conv_b_pw2,
           diff_w_qkv, diff_q_g, diff_k_g, diff_lam_q1, diff_lam_k1, diff_lam_q2, diff_lam_k2,
           diff_subln_g, diff_w_o,
           swa_w_qkv, swa_q_g, swa_k_g, swa_sink, swa_w_o):
    b, t, d = x.shape
    n_ctx = ctx.shape[1]
    depth = norm1_g.shape[0]
    assert depth == N_MIXERS and t % GRID_W == 0

    tm = _pick(t, (512, 256, 128))
    tq = _pick(t, (256, 128))
    t_all = n_ctx + t
    tk = _pick(t_all, (768, 512, 256, 128))
    tkc = _pick(n_ctx, (768, 512, 256, 128))

    cv = jnp.concatenate([c, c_ctx[None], jnp.zeros((8 - b - 1, d), F32)], axis=0)
    mod_all = _mod_call(cv, mod_w, mod_b).reshape(depth, 8, 6, d)
    rope = _rope_tables(t)
    ones_bd = _ones_block_diag()
    row = lambda v: v.reshape(1, -1).astype(F32)

    xc = ctx
    for i in range(depth):
        m, j = i % N_MIXERS, i // N_MIXERS
        need_ctx = i < depth - 1
        mod_l = mod_all[i, :b]
        mod_c = jnp.broadcast_to(mod_all[i, b], (b, 6, d))
        g1, g2 = row(norm1_g[i]), row(norm2_g[i])
        w_up, w_down = mlp_up[i].astype(BF16), mlp_down[i].astype(BF16)
        conv_params = None
        b_o = None
        if m == 1:
            w1, b1 = conv_w_pw1[j].astype(BF16), row(conv_b_pw1[j])
            mixed = _glu_call(x, mod_l, g1, w1, b1, tm=tm)
            mixed_c = _glu_call(xc, mod_c, g1, w1, b1, tm=n_ctx) if need_ctx else None
            conv_params = (conv_w_dw[j].astype(F32), row(conv_b_dw[j]), row(conv_ln_g[j]), row(conv_ln_b[j]))
            w_o, b_o = conv_w_pw2[j].astype(BF16), row(conv_b_pw2[j])
        else:
            if m == 2:
                w_qkv, q_g, k_g, w_o = diff_w_qkv[j], diff_q_g[j], diff_k_g[j], diff_w_o[j]
                n_q = n_k = w_qkv.shape[1] // 3
                q_half = tuple(h % 2 for h in range(n_q // HEAD_DIM))
            else:
                w_qkv, q_g, k_g, w_o = ((gqa_w_qkv[j], gqa_q_g[j], gqa_k_g[j], gqa_w_o[j]) if m == 0 else
                                        (swa_w_qkv[j], swa_q_g[j], swa_k_g[j], swa_w_o[j]))
                n_q = w_o.shape[0]
                n_k = (w_qkv.shape[1] - n_q) // 2
                group = n_q // n_k
                q_half = tuple((h // group) % 2 for h in range(n_q // HEAD_DIM))
            w_o = w_o.astype(BF16)
            w_p, head_gain = _prep_qkv(w_qkv, q_g, k_g, n_q, n_k)
            proj = functools.partial(_proj_call, gain=g1, w=w_p, head_gain=head_gain, ones_bd=ones_bd,
                                     n_q=n_q, n_k=n_k, q_half=q_half)
            q, k, v = proj(x, mod_l, rope=rope, tm=tm)
            qc, kc, vc = proj(xc, mod_c, rope=None, tm=n_ctx)
            k_all = jnp.concatenate([kc, k], axis=1)
            v_all = jnp.concatenate([vc, v], axis=1)
            mixed_c = None
            if m == 0:
                mixed = _gqa_call(q, k_all, v_all, group=group, tq=tq, tk=tk)
                if need_ctx:
                    mixed_c = _gqa_call(qc, kc, vc, group=group, tq=n_ctx, tk=tkc)
            elif m == 2:
                lam_init = 0.8 - 0.6 * math.exp(-0.3 * i)
                lam_vecs = jnp.stack([diff_lam_q1[j], diff_lam_k1[j], diff_lam_q2[j], diff_lam_k2[j]]).astype(F32)
                sg = row(diff_subln_g[j])
                mixed = _diff_call(q, k_all, v_all, lam_vecs, sg, lam_init=lam_init, tq=tq, tk=tk)
                if need_ctx:
                    mixed_c = _diff_call(qc, kc, vc, lam_vecs, sg, lam_init=lam_init, tq=n_ctx, tk=tkc)
            else:
                assert not need_ctx, "windowed layer with a context update is not supported"
                sink = jnp.broadcast_to(swa_sink[j].astype(F32)[:, None, None], (n_q // HEAD_DIM, 1, LANES))
                mixed = _window_call(q, k_all, v_all, sink, group=group, n_ctx=n_ctx, tq=tq)
        x = _post_call(x, mixed, mod_l, w_o, b_o, g2, w_up, w_down, conv_params, tm=tm)
        if need_ctx:
            xc = _post_call(xc, mixed_c, mod_c, w_o, b_o, g2, w_up, w_down, conv_params, tm=n_ctx)
    return x
```

```python
import functools
import math

import jax
import jax.numpy as jnp
from jax import lax
from jax.experimental import pallas as pl
from jax.experimental.pallas import tpu as pltpu

F32 = jnp.float32
BF16 = jnp.bfloat16

GRID_W = 64
HEAD_DIM = 64
N_MIXERS = 4
CONV_WIDTH = 31
WINDOW = 128
ROPE_THETA = 10000.0
EPS = 1e-6
NEG_INF = -1e30
LOG2E = math.log2(math.e)

LANES = 128
MXU_DIM = 256
VMEM_LIMIT_BYTES = 56 * 1024 * 1024

CONV_HALO = 16
CONV_ROWS = 32


def _params(n_axes):
    return pltpu.CompilerParams(dimension_semantics=("parallel",) * n_axes,
                                vmem_limit_bytes=VMEM_LIMIT_BYTES)


def _resident(shape):
    nd = len(shape)
    return pl.BlockSpec(shape, lambda *_: (0,) * nd, pipeline_mode=pl.Buffered(1))


def _pick(n, candidates):
    for cand in candidates:
        if n % cand == 0:
            return cand
    return n


def _modulate(x, gain, shift, scale):
    ms = jnp.mean(x * x, axis=-1, keepdims=True)
    return (x * lax.rsqrt(ms + EPS) * gain) * (1.0 + scale) + shift


def _dot(a, b):
    return jnp.dot(a, b, preferred_element_type=F32)


def _dot_nt(a, b):
    return lax.dot_general(a, b, (((1,), (1,)), ((), ())), preferred_element_type=F32)


def _mod_kernel(cv_ref, w_ref, b_ref, o_ref):
    cv = cv_ref[...]
    s = cv * jax.nn.sigmoid(cv)
    o_ref[0] = _dot(s.astype(BF16), w_ref[0].astype(BF16)) + b_ref[0]


def _mod_call(cv, mod_w, mod_b):
    depth, d, n = mod_w.shape
    rows = cv.shape[0]
    tn = _pick(n, (1536, 1024, 512))
    return pl.pallas_call(
        _mod_kernel,
        out_shape=jax.ShapeDtypeStruct((depth, rows, n), F32),
        grid=(depth, n // tn),
        in_specs=[pl.BlockSpec((rows, d), lambda i, j: (0, 0)),
                  pl.BlockSpec((1, d, tn), lambda i, j: (i, 0, j)),
                  pl.BlockSpec((1, 1, tn), lambda i, j: (i, 0, j))],
        out_specs=pl.BlockSpec((1, rows, tn), lambda i, j: (i, 0, j)),
        compiler_params=_params(2),
        name="modulation",
    )(cv, mod_w, mod_b.reshape(depth, 1, n))


def _proj_kernel(*refs, n_q, n_k, q_half, use_rope):
    if use_rope:
        (x_ref, mod_ref, g_ref, w_ref, hg_ref, e_ref, cos_ref, sin_ref, q_ref, k_ref, v_ref) = refs
    else:
        (x_ref, mod_ref, g_ref, w_ref, hg_ref, e_ref, q_ref, k_ref, v_ref) = refs
    x = x_ref[0]
    mod = mod_ref[0]
    tm = x.shape[0]
    h = _modulate(x, g_ref[...], mod[0:1], mod[1:2]).astype(BF16)
    proj = _dot(h, w_ref[...])
    n_qk = n_q + n_k
    qk = proj[:, :n_qk]
    sq = qk * qk
    hi = sq.astype(BF16)
    lo = (sq - hi.astype(F32)).astype(BF16)
    ones_bd = e_ref[...]
    ss = jnp.concatenate(
        [_dot(hi[:, t:t + MXU_DIM], ones_bd) + _dot(lo[:, t:t + MXU_DIM], ones_bd)
         for t in range(0, n_qk, MXU_DIM)], axis=1)
    qk = qk * lax.rsqrt(ss * (1.0 / HEAD_DIM) + EPS) * hg_ref[...]

    lane = lax.broadcasted_iota(jnp.int32, (tm, LANES), 1)
    tiles = [qk[:, t:t + LANES] for t in range(0, n_qk, LANES)]
    if use_rope:
        first = (lane % HEAD_DIM) < (HEAD_DIM // 2)
        cos = cos_ref[...]
        sin = sin_ref[...]
        tiles = [t * cos + jnp.where(first, pltpu.roll(t, LANES - HEAD_DIM // 2, axis=1),
                                     pltpu.roll(t, HEAD_DIM // 2, axis=1)) * sin
                 for t in tiles]
    low = lane < HEAD_DIM
    for j in range(n_q // HEAD_DIM):
        t = tiles[j // 2]
        if j % 2 != q_half[j]:
            t = pltpu.roll(t, HEAD_DIM, axis=1)
        keep = low if q_half[j] == 0 else jnp.logical_not(low)
        q_ref[0, j] = jnp.where(keep, t, 0.0).astype(BF16)
    k_ref[0] = jnp.concatenate(tiles[n_q // LANES:], axis=1).astype(BF16)
    v_ref[0] = proj[:, n_qk:].astype(BF16)


def _proj_call(x, mod, gain, w, head_gain, ones_bd, rope, *, n_q, n_k, q_half, tm):
    b, t, d = x.shape
    n_tot = w.shape[1]
    n_v = n_tot - n_q - n_k
    hq = n_q // HEAD_DIM
    use_rope = rope is not None
    in_specs = [pl.BlockSpec((1, tm, d), lambda bi, j: (bi, j, 0)),
                pl.BlockSpec((1, 6, d), lambda bi, j: (bi, 0, 0)),
                _resident((1, d)),
                _resident((d, n_tot)),
                _resident((1, n_q + n_k)),
                _resident((MXU_DIM, MXU_DIM))]
    args = [x, mod, gain, w, head_gain, ones_bd]
    if use_rope:
        in_specs += [pl.BlockSpec((tm, LANES), lambda bi, j: (j, 0)),
                     pl.BlockSpec((tm, LANES), lambda bi, j: (j, 0))]
        args += list(rope)
    return pl.pallas_call(
        functools.partial(_proj_kernel, n_q=n_q, n_k=n_k, q_half=q_half, use_rope=use_rope),
        out_shape=(jax.ShapeDtypeStruct((b, hq, t, LANES), BF16),
                   jax.ShapeDtypeStruct((b, t, n_k), BF16),
                   jax.ShapeDtypeStruct((b, t, n_v), BF16)),
        grid=(b, t // tm),
        in_specs=in_specs,
        out_specs=(pl.BlockSpec((1, hq, tm, LANES), lambda bi, j: (bi, 0, j, 0)),
                   pl.BlockSpec((1, tm, n_k), lambda bi, j: (bi, j, 0)),
                   pl.BlockSpec((1, tm, n_v), lambda bi, j: (bi, j, 0))),
        compiler_params=_params(2),
        name="attn_proj",
    )(*args)


def _flash_loop(load_q, k_ref, v_ref, s_ref, cm_ref, m_ref, acc_ref, *, n_kv, tk):
    m_ref[...] = jnp.full(m_ref.shape, NEG_INF, F32)
    acc_ref[...] = jnp.zeros(acc_ref.shape, F32)
    ones = jnp.ones((tk, LANES), BF16)

    def produce(j, slot):
        start = pl.multiple_of(j * tk, tk)
        s = _dot_nt(load_q(), k_ref[0, pl.ds(start, tk), :])
        s_ref[slot] = s
        cm = s[:, 0:LANES]
        for c0 in range(LANES, tk, LANES):
            cm = jnp.maximum(cm, s[:, c0:c0 + LANES])
        cm_ref[slot] = cm

    def consume(j, slot):
        start = pl.multiple_of(j * tk, tk)
        v1 = jnp.concatenate([v_ref[0, pl.ds(start, tk), :], ones], axis=1)
        m_prev = m_ref[...]
        m_new = jnp.maximum(m_prev, jnp.max(cm_ref[slot], axis=1, keepdims=True))
        alpha = jnp.exp2(m_prev - m_new)
        p = jnp.exp2(s_ref[slot] - m_new).astype(BF16)
        acc_ref[...] = alpha * acc_ref[...] + _dot(p, v1)
        m_ref[...] = m_new

    produce(0, 0)

    def pair(jj, carry):
        j = 2 * jj
        produce(j + 1, 1)
        consume(j, 0)
        produce(j + 2, 0)
        consume(j + 1, 1)
        return carry

    lax.fori_loop(0, (n_kv - 1) // 2, pair, 0)
    if n_kv % 2 == 0:
        produce(n_kv - 1, 1)
        consume(n_kv - 2, 0)
        consume(n_kv - 1, 1)
    else:
        consume(n_kv - 1, 0)


def _store_head_pairs(o_ref, o, par, tq, group):
    low = lax.broadcasted_iota(jnp.int32, (tq, LANES), 1) < HEAD_DIM
    for gg in range(group // 2):
        a = o[(2 * gg) * tq:(2 * gg + 1) * tq]
        b = o[(2 * gg + 1) * tq:(2 * gg + 2) * tq]
        if par == 0:
            col = jnp.where(low, a, pltpu.roll(b, HEAD_DIM, axis=1))
        else:
            col = jnp.where(low, pltpu.roll(a, HEAD_DIM, axis=1), b)
        c0 = (par * (group // 2) + gg) * LANES
        o_ref[0, :, c0:c0 + LANES] = col.astype(o_ref.dtype)


def _flash_scratch(m_rows, tk):
    return [pltpu.VMEM((2, m_rows, tk), F32),
            pltpu.VMEM((2, m_rows, LANES), F32),
            pltpu.VMEM((m_rows, 1), F32),
            pltpu.VMEM((m_rows, 2 * LANES), F32)]


def _gqa_kernel(q_ref, k_ref, v_ref, o_ref, s_ref, cm_ref, m_ref, acc_ref, *, group, n_kv, tk):
    tq = q_ref.shape[2]
    for par in range(2):
        load_q = lambda: q_ref[0, par * group:(par + 1) * group].reshape(group * tq, LANES)
        _flash_loop(load_q, k_ref, v_ref, s_ref, cm_ref, m_ref, acc_ref, n_kv=n_kv, tk=tk)
        o = acc_ref[:, :LANES] / acc_ref[:, LANES:]
        _store_head_pairs(o_ref, o, par, tq, group)


def _gqa_call(q, k_all, v_all, *, group, tq, tk):
    b, hq, t, _ = q.shape
    t_all, n_k = k_all.shape[1:]
    pairs = n_k // LANES
    heads = 2 * group
    return pl.pallas_call(
        functools.partial(_gqa_kernel, group=group, n_kv=t_all // tk, tk=tk),
        out_shape=jax.ShapeDtypeStruct((b, t, hq * HEAD_DIM), BF16),
        grid=(b, pairs, t // tq),
        in_specs=[pl.BlockSpec((1, heads, tq, LANES), lambda bi, p, i: (bi, p, i, 0)),
                  pl.BlockSpec((1, t_all, LANES), lambda bi, p, i: (bi, 0, p)),
                  pl.BlockSpec((1, t_all, LANES), lambda bi, p, i: (bi, 0, p))],
        out_specs=pl.BlockSpec((1, tq, heads * HEAD_DIM), lambda bi, p, i: (bi, i, p)),
        scratch_shapes=_flash_scratch(group * tq, tk),
        compiler_params=_params(3),
        name="gqa_attention",
    )(q, k_all, v_all)


def _diff_kernel(q_ref, k_ref, v_ref, lam_ref, sg_ref, o_ref, s_ref, cm_ref, m_ref, acc_ref, *,
                 lam_init, n_kv, tk):
    tq = q_ref.shape[2]
    load_q = lambda: q_ref[0].reshape(2 * tq, LANES)
    _flash_loop(load_q, k_ref, v_ref, s_ref, cm_ref, m_ref, acc_ref, n_kv=n_kv, tk=tk)
    o = acc_ref[:, :LANES] / acc_ref[:, LANES:]
    lv = lam_ref[...]
    lam = (jnp.exp(jnp.sum(lv[0:1] * lv[1:2], axis=1, keepdims=True))
           - jnp.exp(jnp.sum(lv[2:3] * lv[3:4], axis=1, keepdims=True)) + lam_init)
    o = o[:tq] - lam * o[tq:]
    ms = jnp.mean(o * o, axis=-1, keepdims=True)
    o = (o * lax.rsqrt(ms + EPS) * sg_ref[...]) * (1.0 - lam_init)
    o_ref[0] = o.astype(o_ref.dtype)


def _diff_call(q, k_all, v_all, lam_vecs, subln_g, *, lam_init, tq, tk):
    b, hq, t, _ = q.shape
    t_all, n_k = k_all.shape[1:]
    heads = n_k // LANES
    return pl.pallas_call(
        functools.partial(_diff_kernel, lam_init=lam_init, n_kv=t_all // tk, tk=tk),
        out_shape=jax.ShapeDtypeStruct((b, t, heads * LANES), BF16),
        grid=(b, heads, t // tq),
        in_specs=[pl.BlockSpec((1, 2, tq, LANES), lambda bi, h, i: (bi, h, i, 0)),
                  pl.BlockSpec((1, t_all, LANES), lambda bi, h, i: (bi, 0, h)),
                  pl.BlockSpec((1, t_all, LANES), lambda bi, h, i: (bi, 0, h)),
                  pl.BlockSpec((4, HEAD_DIM), lambda bi, h, i: (0, 0)),
                  pl.BlockSpec((1, LANES), lambda bi, h, i: (0, 0))],
        out_specs=pl.BlockSpec((1, tq, LANES), lambda bi, h, i: (bi, i, h)),
        scratch_shapes=_flash_scratch(2 * tq, tk),
        compiler_params=_params(3),
        name="diff_attention",
    )(q, k_all, v_all, lam_vecs, subln_g)


def _window_kernel(q_ref, k_ref, v_ref, sink_ref, o_ref, *, group, n_ctx, t_lat):
    tq = q_ref.shape[2]
    span = tq + 2 * WINDOW
    i = pl.program_id(2)
    start = jnp.clip(i * tq - WINDOW, 0, t_lat - span)
    row0 = pl.multiple_of(n_ctx + start, math.gcd(WINDOW, n_ctx))
    k_lat = k_ref[0, pl.ds(row0, span), :]
    v_lat = v_ref[0, pl.ds(row0, span), :]
    k_ctx = k_ref[0, 0:n_ctx, :]
    v_ctx = v_ref[0, 0:n_ctx, :]
    qpos = i * tq + lax.broadcasted_iota(jnp.int32, (tq, span), 0)
    kpos = start + lax.broadcasted_iota(jnp.int32, (tq, span), 1)
    in_band = (jnp.abs(kpos - qpos) <= WINDOW)[None]
    for par in range(2):
        q_all = q_ref[0, par * group:(par + 1) * group].reshape(group * tq, LANES)
        s_lat = jnp.where(in_band, _dot_nt(q_all, k_lat).reshape(group, tq, span), NEG_INF)
        s_ctx = _dot_nt(q_all, k_ctx).reshape(group, tq, n_ctx)
        sink = sink_ref[par * group:(par + 1) * group][:, :, 0:1] * LOG2E
        m = jnp.maximum(jnp.maximum(jnp.max(s_lat, axis=-1, keepdims=True),
                                    jnp.max(s_ctx, axis=-1, keepdims=True)), sink)
        e_lat = jnp.exp2(s_lat - m)
        e_ctx = jnp.exp2(s_ctx - m)
        den = (jnp.sum(e_lat, axis=-1, keepdims=True) + jnp.sum(e_ctx, axis=-1, keepdims=True)
               + jnp.exp2(sink - m))
        pv = (_dot(e_lat.astype(BF16).reshape(group * tq, span), v_lat)
              + _dot(e_ctx.astype(BF16).reshape(group * tq, n_ctx), v_ctx))
        o = pv / den.reshape(group * tq, 1)
        _store_head_pairs(o_ref, o, par, tq, group)


def _window_call(q, k_all, v_all, sink, *, group, n_ctx, tq):
    b, hq, t, _ = q.shape
    t_all, n_k = k_all.shape[1:]
    pairs = n_k // LANES
    heads = 2 * group
    return pl.pallas_call(
        functools.partial(_window_kernel, group=group, n_ctx=n_ctx, t_lat=t),
        out_shape=jax.ShapeDtypeStruct((b, t, hq * HEAD_DIM), BF16),
        grid=(b, pairs, t // tq),
        in_specs=[pl.BlockSpec((1, heads, tq, LANES), lambda bi, p, i: (bi, p, i, 0)),
                  pl.BlockSpec((1, t_all, LANES), lambda bi, p, i: (bi, 0, p)),
                  pl.BlockSpec((1, t_all, LANES), lambda bi, p, i: (bi, 0, p)),
                  pl.BlockSpec((heads, 1, LANES), lambda bi, p, i: (p, 0, 0))],
        out_specs=pl.BlockSpec((1, tq, heads * HEAD_DIM), lambda bi, p, i: (bi, i, p)),
        compiler_params=_params(3),
        name="window_attention",
    )(q, k_all, v_all, sink)


def _glu_kernel(x_ref, mod_ref, g_ref, w_ref, b_ref, u_ref):
    x = x_ref[0]
    mod = mod_ref[0]
    d = x.shape[1]
    h = _modulate(x, g_ref[...], mod[0:1], mod[1:2]).astype(BF16)
    ag = _dot(h, w_ref[...]) + b_ref[...]
    u_ref[0] = (ag[:, :d] * jax.nn.sigmoid(ag[:, d:])).astype(u_ref.dtype)


def _glu_call(x, mod, gain, w, bias, *, tm):
    b, t, d = x.shape
    return pl.pallas_call(
        _glu_kernel,
        out_shape=jax.ShapeDtypeStruct((b, t, d), BF16),
        grid=(b, t // tm),
        in_specs=[pl.BlockSpec((1, tm, d), lambda bi, j: (bi, j, 0)),
                  pl.BlockSpec((1, 6, d), lambda bi, j: (bi, 0, 0)),
                  _resident((1, d)),
                  _resident((d, 2 * d)),
                  _resident((1, 2 * d))],
        out_specs=pl.BlockSpec((1, tm, d), lambda bi, j: (bi, j, 0)),
        compiler_params=_params(2),
        name="conv_glu",
    )(x, mod, gain, w, bias)


def _depthwise_conv(ubuf_ref, cbuf_ref, wdw_ref, bdw_ref, tm, d):
    win_rows = CONV_ROWS + 2 * CONV_HALO
    base = CONV_HALO - CONV_WIDTH // 2
    lane_chunk = 2 * LANES

    def step(r, carry):
        r0 = pl.multiple_of(r * CONV_ROWS, CONV_ROWS)
        for c0 in range(0, d, lane_chunk):
            win = ubuf_ref[pl.ds(r0, win_rows), c0:c0 + lane_chunk]
            acc = jnp.zeros((CONV_ROWS, lane_chunk), F32) + bdw_ref[:, c0:c0 + lane_chunk]
            for sub in range(8):
                shifted = win if sub == 0 else pltpu.roll(win, win_rows - sub, axis=0)
                for al in range(0, win_rows - CONV_ROWS + 1, 8):
                    k = al + sub - base
                    if 0 <= k < CONV_WIDTH:
                        acc = acc + wdw_ref[k:k + 1, c0:c0 + lane_chunk] * shifted[al:al + CONV_ROWS]
            cbuf_ref[pl.ds(r0, CONV_ROWS), c0:c0 + lane_chunk] = acc
        return carry

    lax.fori_loop(0, tm // CONV_ROWS, step, 0)


def _post_kernel(*refs, conv, has_bias, d_ff_chunk):
    it = iter(refs)
    x_ref = next(it)
    if conv:
        up_ref, uc_ref, un_ref = next(it), next(it), next(it)
        wdw_ref, bdw_ref, lng_ref, lnb_ref = next(it), next(it), next(it), next(it)
    else:
        o_ref = next(it)
    mod_ref, wo_ref = next(it), next(it)
    bo_ref = next(it) if has_bias else None
    g2_ref, wup_ref, wdn_ref, out_ref = next(it), next(it), next(it), next(it)
    if conv:
        ubuf_ref, cbuf_ref = next(it), next(it)

    x = x_ref[0]
    mod = mod_ref[0]
    tm, d = x.shape
    if conv:
        j = pl.program_id(1)
        prev = jnp.where(j > 0, up_ref[0].astype(F32), 0.0)
        nxt = jnp.where(j < pl.num_programs(1) - 1, un_ref[0].astype(F32), 0.0)
        ubuf_ref[0:CONV_HALO] = prev
        ubuf_ref[CONV_HALO:CONV_HALO + tm] = uc_ref[0].astype(F32)
        ubuf_ref[CONV_HALO + tm:] = nxt
        _depthwise_conv(ubuf_ref, cbuf_ref, wdw_ref, bdw_ref, tm, d)
        cv = cbuf_ref[...]
        mu = jnp.mean(cv, axis=-1, keepdims=True)
        cc = cv - mu
        var = jnp.mean(cc * cc, axis=-1, keepdims=True)
        yn = cc * lax.rsqrt(var + EPS) * lng_ref[...] + lnb_ref[...]
        o = (yn * jax.nn.sigmoid(yn)).astype(BF16)
    else:
        o = o_ref[0]
    y = _dot(o, wo_ref[...])
    if has_bias:
        y = y + bo_ref[...]
    x1 = x + mod[2:3] * y
    h2 = _modulate(x1, g2_ref[...], mod[3:4], mod[4:5]).astype(BF16)
    acc = jnp.zeros((tm, d), F32)
    d_ff = wup_ref.shape[1]
    for c0 in range(0, d_ff, d_ff_chunk):
        up = jnp.maximum(_dot(h2, wup_ref[:, c0:c0 + d_ff_chunk]), 0.0)
        acc = acc + _dot((up * up).astype(BF16), wdn_ref[c0:c0 + d_ff_chunk, :])
    out_ref[0] = x1 + mod[5:6] * acc


def _post_call(x, mixed, mod, w_o, b_o, gain2, w_up, w_down, conv_params, *, tm):
    b, t, d = x.shape
    d_in = w_o.shape[0]
    d_ff = w_up.shape[1]
    conv = conv_params is not None
    has_bias = b_o is not None
    tok = lambda bi, j: (bi, j, 0)
    in_specs = [pl.BlockSpec((1, tm, d), tok)]
    args = [x]
    scratch = []
    if conv:
        hb = tm // CONV_HALO
        last = t // CONV_HALO - 1
        in_specs += [pl.BlockSpec((1, CONV_HALO, d), lambda bi, j: (bi, jnp.maximum(j * hb - 1, 0), 0)),
                     pl.BlockSpec((1, tm, d), tok),
                     pl.BlockSpec((1, CONV_HALO, d), lambda bi, j: (bi, jnp.minimum((j + 1) * hb, last), 0)),
                     _resident((CONV_WIDTH, d)), _resident((1, d)), _resident((1, d)), _resident((1, d))]
        args += [mixed, mixed, mixed] + list(conv_params)
        scratch = [pltpu.VMEM((tm + 2 * CONV_HALO, d), F32), pltpu.VMEM((tm, d), F32)]
    else:
        in_specs += [pl.BlockSpec((1, tm, d_in), tok)]
        args += [mixed]
    in_specs += [pl.BlockSpec((1, 6, d), lambda bi, j: (bi, 0, 0)), _resident((d_in, d))]
    args += [mod, w_o]
    if has_bias:
        in_specs += [_resident((1, d))]
        args += [b_o]
    in_specs += [_resident((1, d)), _resident((d, d_ff)), _resident((d_ff, d))]
    args += [gain2, w_up, w_down]
    return pl.pallas_call(
        functools.partial(_post_kernel, conv=conv, has_bias=has_bias, d_ff_chunk=min(d_ff, 1024)),
        out_shape=jax.ShapeDtypeStruct((b, t, d), F32),
        grid=(b, t // tm),
        in_specs=in_specs,
        out_specs=pl.BlockSpec((1, tm, d), tok),
        scratch_shapes=scratch,
        compiler_params=_params(2),
        name="post_mlp",
    )(*args)


def _rope_tables(t):
    rows = t // GRID_W
    row = jnp.repeat(jnp.arange(rows, dtype=F32), GRID_W)
    col = jnp.tile(jnp.arange(GRID_W, dtype=F32), rows)
    half = HEAD_DIM // 2
    inv = 1.0 / jnp.power(ROPE_THETA, jnp.arange(0, half, 2, dtype=F32) / half)
    ang = jnp.concatenate([row[:, None] * inv, col[:, None] * inv], axis=-1)
    cos, sin = jnp.cos(ang), jnp.sin(ang)
    reps = LANES // HEAD_DIM
    return (jnp.tile(jnp.concatenate([cos, cos], axis=1), (1, reps)),
            jnp.tile(jnp.concatenate([-sin, sin], axis=1), (1, reps)))


def _split_pairs_perm(n_heads):
    within = jnp.concatenate([jnp.arange(0, HEAD_DIM, 2), jnp.arange(1, HEAD_DIM, 2)])
    return (jnp.arange(n_heads)[:, None] * HEAD_DIM + within[None, :]).reshape(-1)


def _prep_qkv(w_qkv, q_g, k_g, n_q, n_k):
    hq, hk = n_q // HEAD_DIM, n_k // HEAD_DIM
    perm = jnp.concatenate([_split_pairs_perm(hq + hk), jnp.arange(n_q + n_k, w_qkv.shape[1])])
    w = jnp.take(w_qkv, perm, axis=1).astype(BF16)
    within = perm[:HEAD_DIM]
    scale = LOG2E / math.sqrt(HEAD_DIM)
    head_gain = jnp.concatenate([jnp.tile(q_g[within] * scale, hq), jnp.tile(k_g[within], hk)])
    return w, head_gain.reshape(1, n_q + n_k).astype(F32)


def _ones_block_diag():
    idx = jnp.arange(MXU_DIM) // HEAD_DIM
    return (idx[:, None] == idx[None, :]).astype(BF16)


def kernel(x, c, ctx, c_ctx, norm1_g, norm2_g, mod_w, mod_b, mlp_up, mlp_down,
           gqa_w_qkv, gqa_q_g, gqa_k_g, gqa_w_o,
           conv_w_pw1, conv_b_pw1, conv_w_dw, conv_b_dw, conv_ln_g, conv_ln_b, conv_w_pw2, conv_b_pw2,
           diff_w_qkv, diff_q_g, diff_k_g, diff_lam_q1, diff_lam_k1, diff_lam_q2, diff_lam_k2,
           diff_subln_g, diff_w_o,
           swa_w_qkv, swa_q_g, swa_k_g, swa_sink, swa_w_o):
    b, t, d = x.shape
    n_ctx = ctx.shape[1]
    depth = norm1_g.shape[0]
    assert depth == N_MIXERS and t % GRID_W == 0

    tm = _pick(t, (512, 256, 128))
    tq = _pick(t, (256, 128))
    tq_diff = _pick(t, (512, 256, 128))
    t_all = n_ctx + t
    tk = _pick(t_all, (768, 512, 256, 128))
    tkc = _pick(n_ctx, (768, 512, 256, 128))

    cv = jnp.concatenate([c, c_ctx[None], jnp.zeros((8 - b - 1, d), F32)], axis=0)
    mod_all = _mod_call(cv, mod_w, mod_b).reshape(depth, 8, 6, d)
    rope = _rope_tables(t)
    ones_bd = _ones_block_diag()
    row = lambda v: v.reshape(1, -1).astype(F32)

    xc = ctx
    for i in range(depth):
        m, j = i % N_MIXERS, i // N_MIXERS
        need_ctx = i < depth - 1
        mod_l = mod_all[i, :b]
        mod_c = jnp.broadcast_to(mod_all[i, b], (b, 6, d))
        g1, g2 = row(norm1_g[i]), row(norm2_g[i])
        w_up, w_down = mlp_up[i].astype(BF16), mlp_down[i].astype(BF16)
        conv_params = None
        b_o = None
        if m == 1:
            w1, b1 = conv_w_pw1[j].astype(BF16), row(conv_b_pw1[j])
            mixed = _glu_call(x, mod_l, g1, w1, b1, tm=tm)
            mixed_c = _glu_call(xc, mod_c, g1, w1, b1, tm=n_ctx) if need_ctx else None
            conv_params = (conv_w_dw[j].astype(F32), row(conv_b_dw[j]), row(conv_ln_g[j]), row(conv_ln_b[j]))
            w_o, b_o = conv_w_pw2[j].astype(BF16), row(conv_b_pw2[j])
        else:
            if m == 2:
                w_qkv, q_g, k_g, w_o = diff_w_qkv[j], diff_q_g[j], diff_k_g[j], diff_w_o[j]
                n_q = n_k = w_qkv.shape[1] // 3
                q_half = tuple(h % 2 for h in range(n_q // HEAD_DIM))
            else:
                w_qkv, q_g, k_g, w_o = ((gqa_w_qkv[j], gqa_q_g[j], gqa_k_g[j], gqa_w_o[j]) if m == 0 else
                                        (swa_w_qkv[j], swa_q_g[j], swa_k_g[j], swa_w_o[j]))
                n_q = w_o.shape[0]
                n_k = (w_qkv.shape[1] - n_q) // 2
                group = n_q // n_k
                q_half = tuple((h // group) % 2 for h in range(n_q // HEAD_DIM))
            w_o = w_o.astype(BF16)
            w_p, head_gain = _prep_qkv(w_qkv, q_g, k_g, n_q, n_k)
            proj = functools.partial(_proj_call, gain=g1, w=w_p, head_gain=head_gain, ones_bd=ones_bd,
                                     n_q=n_q, n_k=n_k, q_half=q_half)
            q, k, v = proj(x, mod_l, rope=rope, tm=tm)
            qc, kc, vc = proj(xc, mod_c, rope=None, tm=n_ctx)
            k_all = jnp.concatenate([kc, k], axis=1)
            v_all = jnp.concatenate([vc, v], axis=1)
            mixed_c = None
            if m == 0:
                mixed = _gqa_call(q, k_all, v_all, group=group, tq=tq, tk=tk)
                if need_ctx:
                    mixed_c = _gqa_call(qc, kc, vc, group=group, tq=n_ctx, tk=tkc)
            elif m == 2:
                lam_init = 0.8 - 0.6 * math.exp(-0.3 * i)
                lam_vecs = jnp.stack([diff_lam_q1[j], diff_lam_k1[j], diff_lam_q2[j], diff_lam_k2[j]]).astype(F32)
                sg = row(diff_subln_g[j])
                mixed = _diff_call(q, k_all, v_all, lam_vecs, sg, lam_init=lam_init, tq=tq_diff, tk=tk)
                if need_ctx:
                    mixed_c = _diff_call(qc, kc, vc, lam_vecs, sg, lam_init=lam_init, tq=n_ctx, tk=tkc)
            else:
                assert not need_ctx, "windowed layer with a context update is not supported"
                sink = jnp.broadcast_to(swa_sink[j].astype(F32)[:, None, None], (n_q // HEAD_DIM, 1, LANES))
                mixed = _window_call(q, k_all, v_all, sink, group=group, n_ctx=n_ctx, tq=tq)
        x = _post_call(x, mixed, mod_l, w_o, b_o, g2, w_up, w_down, conv_params, tm=tm)
        if need_ctx:
            xc = _post_call(xc, mixed_c, mod_c, w_o, b_o, g2, w_up, w_down, conv_params, tm=n_ctx)
    return x
```

```python
import functools
import math

import jax
import jax.numpy as jnp
from jax import lax
from jax.experimental import pallas as pl
from jax.experimental.pallas import tpu as pltpu

F32 = jnp.float32
BF16 = jnp.bfloat16

GRID_W = 64
HEAD_DIM = 64
N_MIXERS = 4
CONV_WIDTH = 31
WINDOW = 128
ROPE_THETA = 10000.0
EPS = 1e-6
NEG_INF = -1e30
LOG2E = math.log2(math.e)

LANES = 128
MXU_DIM = 256
VMEM_LIMIT_BYTES = 56 * 1024 * 1024

CONV_HALO = 16
CONV_ROWS = 32


def _params(n_axes):
    return pltpu.CompilerParams(dimension_semantics=("parallel",) * n_axes,
                                vmem_limit_bytes=VMEM_LIMIT_BYTES)


def _resident(shape):
    nd = len(shape)
    return pl.BlockSpec(shape, lambda *_: (0,) * nd, pipeline_mode=pl.Buffered(1))


def _pick(n, candidates):
    for cand in candidates:
        if n % cand == 0:
            return cand
    return n


def _modulate(x, gain, shift, scale):
    ms = jnp.mean(x * x, axis=-1, keepdims=True)
    return (x * lax.rsqrt(ms + EPS) * gain) * (1.0 + scale) + shift


def _dot(a, b):
    return jnp.dot(a, b, preferred_element_type=F32)


def _dot_nt(a, b):
    return lax.dot_general(a, b, (((1,), (1,)), ((), ())), preferred_element_type=F32)


def _mod_kernel(cv_ref, w_ref, b_ref, o_ref):
    cv = cv_ref[...]
    s = cv * jax.nn.sigmoid(cv)
    o_ref[0] = _dot(s.astype(BF16), w_ref[0].astype(BF16)) + b_ref[0]


def _mod_call(cv, mod_w, mod_b):
    depth, d, n = mod_w.shape
    rows = cv.shape[0]
    tn = _pick(n, (1536, 1024, 512))
    return pl.pallas_call(
        _mod_kernel,
        out_shape=jax.ShapeDtypeStruct((depth, rows, n), F32),
        grid=(depth, n // tn),
        in_specs=[pl.BlockSpec((rows, d), lambda i, j: (0, 0)),
                  pl.BlockSpec((1, d, tn), lambda i, j: (i, 0, j)),
                  pl.BlockSpec((1, 1, tn), lambda i, j: (i, 0, j))],
        out_specs=pl.BlockSpec((1, rows, tn), lambda i, j: (i, 0, j)),
        compiler_params=_params(2),
        name="modulation",
    )(cv, mod_w, mod_b.reshape(depth, 1, n))


def _proj_kernel(*refs, n_q, n_k, q_half, use_rope):
    if use_rope:
        (x_ref, mod_ref, g_ref, w_ref, wvt_ref, hg_ref, e_ref, cos_ref, sin_ref, q_ref, k_ref, vt_ref) = refs
    else:
        (x_ref, mod_ref, g_ref, w_ref, wvt_ref, hg_ref, e_ref, q_ref, k_ref, vt_ref) = refs
    x = x_ref[0]
    mod = mod_ref[0]
    tm = x.shape[0]
    h = _modulate(x, g_ref[...], mod[0:1], mod[1:2]).astype(BF16)
    qk = _dot(h, w_ref[...])
    vt_ref[0] = _dot_nt(wvt_ref[...], h).astype(BF16)
    n_qk = n_q + n_k
    sq = qk * qk
    hi = sq.astype(BF16)
    lo = (sq - hi.astype(F32)).astype(BF16)
    ones_bd = e_ref[...]
    ss = jnp.concatenate(
        [_dot(hi[:, t:t + MXU_DIM], ones_bd) + _dot(lo[:, t:t + MXU_DIM], ones_bd)
         for t in range(0, n_qk, MXU_DIM)], axis=1)
    qk = qk * lax.rsqrt(ss * (1.0 / HEAD_DIM) + EPS) * hg_ref[...]

    lane = lax.broadcasted_iota(jnp.int32, (tm, LANES), 1)
    tiles = [qk[:, t:t + LANES] for t in range(0, n_qk, LANES)]
    if use_rope:
        first = (lane % HEAD_DIM) < (HEAD_DIM // 2)
        cos = cos_ref[...]
        sin = sin_ref[...]
        tiles = [t * cos + jnp.where(first, pltpu.roll(t, LANES - HEAD_DIM // 2, axis=1),
                                     pltpu.roll(t, HEAD_DIM // 2, axis=1)) * sin
                 for t in tiles]
    low = lane < HEAD_DIM
    for j in range(n_q // HEAD_DIM):
        t = tiles[j // 2]
        if j % 2 != q_half[j]:
            t = pltpu.roll(t, HEAD_DIM, axis=1)
        keep = low if q_half[j] == 0 else jnp.logical_not(low)
        q_ref[0, j] = jnp.where(keep, t, 0.0).astype(BF16)
    k_ref[0] = jnp.concatenate(tiles[n_q // LANES:], axis=1).astype(BF16)


def _proj_call(x, mod, gain, w, wvt, head_gain, ones_bd, rope, *, n_q, n_k, q_half, tm):
    b, t, d = x.shape
    n_v = wvt.shape[0]
    hq = n_q // HEAD_DIM
    use_rope = rope is not None
    in_specs = [pl.BlockSpec((1, tm, d), lambda bi, j: (bi, j, 0)),
                pl.BlockSpec((1, 6, d), lambda bi, j: (bi, 0, 0)),
                _resident((1, d)),
                _resident((d, n_q + n_k)),
                _resident((n_v, d)),
                _resident((1, n_q + n_k)),
                _resident((MXU_DIM, MXU_DIM))]
    args = [x, mod, gain, w, wvt, head_gain, ones_bd]
    if use_rope:
        in_specs += [pl.BlockSpec((tm, LANES), lambda bi, j: (j, 0)),
                     pl.BlockSpec((tm, LANES), lambda bi, j: (j, 0))]
        args += list(rope)
    return pl.pallas_call(
        functools.partial(_proj_kernel, n_q=n_q, n_k=n_k, q_half=q_half, use_rope=use_rope),
        out_shape=(jax.ShapeDtypeStruct((b, hq, t, LANES), BF16),
                   jax.ShapeDtypeStruct((b, t, n_k), BF16),
                   jax.ShapeDtypeStruct((b, n_v, t), BF16)),
        grid=(b, t // tm),
        in_specs=in_specs,
        out_specs=(pl.BlockSpec((1, hq, tm, LANES), lambda bi, j: (bi, 0, j, 0)),
                   pl.BlockSpec((1, tm, n_k), lambda bi, j: (bi, j, 0)),
                   pl.BlockSpec((1, n_v, tm), lambda bi, j: (bi, 0, j))),
        compiler_params=_params(2),
        name="attn_proj",
    )(*args)


ONES_ROWS = 16


def _flash_loop(load_q, k_ref, load_vt, s_ref, cm_ref, m_ref, acc_ref, *, n_kv, tk):
    m_ref[...] = jnp.full(m_ref.shape, NEG_INF, F32)
    acc_ref[...] = jnp.zeros(acc_ref.shape, F32)
    ones = jnp.ones((ONES_ROWS, tk), BF16)
    m_cols = s_ref.shape[2]

    def produce(j, slot):
        start = pl.multiple_of(j * tk, tk)
        s = _dot_nt(k_ref[0, pl.ds(start, tk), :], load_q())
        s_ref[slot] = s
        cm_ref[slot] = jnp.max(s.reshape(tk // 8, 8, m_cols), axis=0)

    def consume(j, slot):
        vt1 = jnp.concatenate([load_vt(pl.multiple_of(j * tk, tk), tk), ones], axis=0)
        m_prev = m_ref[...]
        m_new = jnp.maximum(m_prev, jnp.max(cm_ref[slot], axis=0, keepdims=True))
        alpha = jnp.exp2(m_prev - m_new)
        p = jnp.exp2(s_ref[slot] - m_new).astype(BF16)
        acc_ref[...] = alpha * acc_ref[...] + _dot(vt1, p)
        m_ref[...] = m_new

    produce(0, 0)

    def pair(jj, carry):
        j = 2 * jj
        produce(j + 1, 1)
        consume(j, 0)
        produce(j + 2, 0)
        consume(j + 1, 1)
        return carry

    lax.fori_loop(0, (n_kv - 1) // 2, pair, 0)
    if n_kv % 2 == 0:
        produce(n_kv - 1, 1)
        consume(n_kv - 2, 0)
        consume(n_kv - 1, 1)
    else:
        consume(n_kv - 1, 0)


def _store_heads_t(o_ref, ot, par, tq, group):
    for gg in range(group // 2):
        slab = jnp.concatenate([ot[:, (2 * gg) * tq:(2 * gg + 1) * tq],
                                ot[:, (2 * gg + 1) * tq:(2 * gg + 2) * tq]], axis=0)
        c0 = (par * (group // 2) + gg) * LANES
        o_ref[0, :, c0:c0 + LANES] = slab.T.astype(o_ref.dtype)


def _flash_scratch(m_cols, tk, v_rows):
    return [pltpu.VMEM((2, tk, m_cols), F32),
            pltpu.VMEM((2, 8, m_cols), F32),
            pltpu.VMEM((1, m_cols), F32),
            pltpu.VMEM((v_rows + ONES_ROWS, m_cols), F32)]


def _gqa_kernel(q_ref, k_ref, vt_ref, o_ref, s_ref, cm_ref, m_ref, acc_ref, *, group, n_kv, tk):
    tq = q_ref.shape[2]
    for par in range(2):
        load_q = lambda: q_ref[0, par * group:(par + 1) * group].reshape(group * tq, LANES)
        load_vt = lambda start, size: vt_ref[0, par * HEAD_DIM:(par + 1) * HEAD_DIM, pl.ds(start, size)]
        _flash_loop(load_q, k_ref, load_vt, s_ref, cm_ref, m_ref, acc_ref, n_kv=n_kv, tk=tk)
        ot = acc_ref[0:HEAD_DIM, :] / acc_ref[HEAD_DIM:HEAD_DIM + 1, :]
        _store_heads_t(o_ref, ot, par, tq, group)


def _gqa_call(q, k_all, vt_all, *, group, tq, tk):
    b, hq, t, _ = q.shape
    t_all, n_k = k_all.shape[1:]
    pairs = n_k // LANES
    heads = 2 * group
    return pl.pallas_call(
        functools.partial(_gqa_kernel, group=group, n_kv=t_all // tk, tk=tk),
        out_shape=jax.ShapeDtypeStruct((b, t, hq * HEAD_DIM), BF16),
        grid=(b, pairs, t // tq),
        in_specs=[pl.BlockSpec((1, heads, tq, LANES), lambda bi, p, i: (bi, p, i, 0)),
                  pl.BlockSpec((1, t_all, LANES), lambda bi, p, i: (bi, 0, p)),
                  pl.BlockSpec((1, LANES, t_all), lambda bi, p, i: (bi, p, 0))],
        out_specs=pl.BlockSpec((1, tq, heads * HEAD_DIM), lambda bi, p, i: (bi, i, p)),
        scratch_shapes=_flash_scratch(group * tq, tk, HEAD_DIM),
        compiler_params=_params(3),
        name="gqa_attention",
    )(q, k_all, vt_all)


def _diff_kernel(q_ref, k_ref, vt_ref, lam_ref, sg_ref, o_ref, s_ref, cm_ref, m_ref, acc_ref, *,
                 lam_init, n_kv, tk):
    tq = q_ref.shape[2]
    dv = vt_ref.shape[1]
    load_q = lambda: q_ref[0].reshape(2 * tq, LANES)
    load_vt = lambda start, size: vt_ref[0, :, pl.ds(start, size)]
    _flash_loop(load_q, k_ref, load_vt, s_ref, cm_ref, m_ref, acc_ref, n_kv=n_kv, tk=tk)
    ot = acc_ref[0:dv, :] / acc_ref[dv:dv + 1, :]
    lv = lam_ref[...]
    lam = (jnp.exp(jnp.sum(lv[0:1] * lv[1:2], axis=1, keepdims=True))
           - jnp.exp(jnp.sum(lv[2:3] * lv[3:4], axis=1, keepdims=True)) + lam_init)
    ot = ot[:, :tq] - lam * ot[:, tq:]
    ms = jnp.mean(ot * ot, axis=0, keepdims=True)
    ot = (ot * lax.rsqrt(ms + EPS) * sg_ref[...]) * (1.0 - lam_init)
    o_ref[0] = ot.T.astype(o_ref.dtype)


def _diff_call(q, k_all, vt_all, lam_vecs, subln_g, *, lam_init, tq, tk):
    b, hq, t, _ = q.shape
    t_all, n_k = k_all.shape[1:]
    heads = n_k // LANES
    dv = vt_all.shape[1] // heads
    return pl.pallas_call(
        functools.partial(_diff_kernel, lam_init=lam_init, n_kv=t_all // tk, tk=tk),
        out_shape=jax.ShapeDtypeStruct((b, t, heads * dv), BF16),
        grid=(b, heads, t // tq),
        in_specs=[pl.BlockSpec((1, 2, tq, LANES), lambda bi, h, i: (bi, h, i, 0)),
                  pl.BlockSpec((1, t_all, LANES), lambda bi, h, i: (bi, 0, h)),
                  pl.BlockSpec((1, dv, t_all), lambda bi, h, i: (bi, h, 0)),
                  pl.BlockSpec((4, HEAD_DIM), lambda bi, h, i: (0, 0)),
                  pl.BlockSpec((dv, 1), lambda bi, h, i: (0, 0))],
        out_specs=pl.BlockSpec((1, tq, dv), lambda bi, h, i: (bi, i, h)),
        scratch_shapes=_flash_scratch(2 * tq, tk, dv),
        compiler_params=_params(3),
        name="diff_attention",
    )(q, k_all, vt_all, lam_vecs, subln_g)


def _window_kernel(q_ref, k_ref, vt_ref, sink_ref, o_ref, *, group, n_ctx, t_lat):
    tq = q_ref.shape[2]
    span = tq + 2 * WINDOW
    i = pl.program_id(2)
    start = jnp.clip(i * tq - WINDOW, 0, t_lat - span)
    row0 = pl.multiple_of(n_ctx + start, math.gcd(WINDOW, n_ctx))
    k_lat = k_ref[0, pl.ds(row0, span), :]
    k_ctx = k_ref[0, 0:n_ctx, :]
    kpos = start + lax.broadcasted_iota(jnp.int32, (span, tq), 0)
    qpos = i * tq + lax.broadcasted_iota(jnp.int32, (span, tq), 1)
    in_band = jnp.abs(kpos - qpos) <= WINDOW
    in_band = jnp.concatenate([in_band] * group, axis=1)
    ones_lat = jnp.ones((ONES_ROWS, span), BF16)
    ones_ctx = jnp.ones((ONES_ROWS, n_ctx), BF16)
    for par in range(2):
        rows = slice(par * HEAD_DIM, (par + 1) * HEAD_DIM)
        q_all = q_ref[0, par * group:(par + 1) * group].reshape(group * tq, LANES)
        s_lat = jnp.where(in_band, _dot_nt(k_lat, q_all), NEG_INF)
        s_ctx = _dot_nt(k_ctx, q_all)
        sink = sink_ref[par] * LOG2E
        m = jnp.maximum(jnp.maximum(jnp.max(s_lat, axis=0, keepdims=True),
                                    jnp.max(s_ctx, axis=0, keepdims=True)), sink)
        e_lat = jnp.exp2(s_lat - m).astype(BF16)
        e_ctx = jnp.exp2(s_ctx - m).astype(BF16)
        vt_lat = jnp.concatenate([vt_ref[0, rows, pl.ds(row0, span)], ones_lat], axis=0)
        vt_ctx = jnp.concatenate([vt_ref[0, rows, 0:n_ctx], ones_ctx], axis=0)
        acc = _dot(vt_lat, e_lat) + _dot(vt_ctx, e_ctx)
        den = acc[HEAD_DIM:HEAD_DIM + 1, :] + jnp.exp2(sink - m)
        _store_heads_t(o_ref, acc[0:HEAD_DIM, :] / den, par, tq, group)


def _window_call(q, k_all, vt_all, sink_rows, *, group, n_ctx, tq):
    b, hq, t, _ = q.shape
    t_all, n_k = k_all.shape[1:]
    pairs = n_k // LANES
    heads = 2 * group
    return pl.pallas_call(
        functools.partial(_window_kernel, group=group, n_ctx=n_ctx, t_lat=t),
        out_shape=jax.ShapeDtypeStruct((b, t, hq * HEAD_DIM), BF16),
        grid=(b, pairs, t // tq),
        in_specs=[pl.BlockSpec((1, heads, tq, LANES), lambda bi, p, i: (bi, p, i, 0)),
                  pl.BlockSpec((1, t_all, LANES), lambda bi, p, i: (bi, 0, p)),
                  pl.BlockSpec((1, LANES, t_all), lambda bi, p, i: (bi, p, 0)),
                  pl.BlockSpec((2, 1, group * tq), lambda bi, p, i: (p, 0, 0))],
        out_specs=pl.BlockSpec((1, tq, heads * HEAD_DIM), lambda bi, p, i: (bi, i, p)),
        compiler_params=_params(3),
        name="window_attention",
    )(q, k_all, vt_all, sink_rows)


def _glu_kernel(x_ref, mod_ref, g_ref, w_ref, b_ref, u_ref):
    x = x_ref[0]
    mod = mod_ref[0]
    d = x.shape[1]
    h = _modulate(x, g_ref[...], mod[0:1], mod[1:2]).astype(BF16)
    ag = _dot(h, w_ref[...]) + b_ref[...]
    u_ref[0] = (ag[:, :d] * jax.nn.sigmoid(ag[:, d:])).astype(u_ref.dtype)


def _glu_call(x, mod, gain, w, bias, *, tm):
    b, t, d = x.shape
    return pl.pallas_call(
        _glu_kernel,
        out_shape=jax.ShapeDtypeStruct((b, t, d), BF16),
        grid=(b, t // tm),
        in_specs=[pl.BlockSpec((1, tm, d), lambda bi, j: (bi, j, 0)),
                  pl.BlockSpec((1, 6, d), lambda bi, j: (bi, 0, 0)),
                  _resident((1, d)),
                  _resident((d, 2 * d)),
                  _resident((1, 2 * d))],
        out_specs=pl.BlockSpec((1, tm, d), lambda bi, j: (bi, j, 0)),
        compiler_params=_params(2),
        name="conv_glu",
    )(x, mod, gain, w, bias)


def _depthwise_conv(ubuf_ref, cbuf_ref, wdw_ref, bdw_ref, tm, d):
    win_rows = CONV_ROWS + 2 * CONV_HALO
    base = CONV_HALO - CONV_WIDTH // 2
    lane_chunk = 2 * LANES

    def step(r, carry):
        r0 = pl.multiple_of(r * CONV_ROWS, CONV_ROWS)
        for c0 in range(0, d, lane_chunk):
            win = ubuf_ref[pl.ds(r0, win_rows), c0:c0 + lane_chunk]
            acc = jnp.zeros((CONV_ROWS, lane_chunk), F32) + bdw_ref[:, c0:c0 + lane_chunk]
            for sub in range(8):
                shifted = win if sub == 0 else pltpu.roll(win, win_rows - sub, axis=0)
                for al in range(0, win_rows - CONV_ROWS + 1, 8):
                    k = al + sub - base
                    if 0 <= k < CONV_WIDTH:
                        acc = acc + wdw_ref[k:k + 1, c0:c0 + lane_chunk] * shifted[al:al + CONV_ROWS]
            cbuf_ref[pl.ds(r0, CONV_ROWS), c0:c0 + lane_chunk] = acc
        return carry

    lax.fori_loop(0, tm // CONV_ROWS, step, 0)


def _post_kernel(*refs, conv, has_bias, d_ff_chunk):
    it = iter(refs)
    x_ref = next(it)
    if conv:
        up_ref, uc_ref, un_ref = next(it), next(it), next(it)
        wdw_ref, bdw_ref, lng_ref, lnb_ref = next(it), next(it), next(it), next(it)
    else:
        o_ref = next(it)
    mod_ref, wo_ref = next(it), next(it)
    bo_ref = next(it) if has_bias else None
    g2_ref, wup_ref, wdn_ref, out_ref = next(it), next(it), next(it), next(it)
    if conv:
        ubuf_ref, cbuf_ref = next(it), next(it)

    x = x_ref[0]
    mod = mod_ref[0]
    tm, d = x.shape
    if conv:
        j = pl.program_id(1)
        prev = jnp.where(j > 0, up_ref[0].astype(F32), 0.0)
        nxt = jnp.where(j < pl.num_programs(1) - 1, un_ref[0].astype(F32), 0.0)
        ubuf_ref[0:CONV_HALO] = prev
        ubuf_ref[CONV_HALO:CONV_HALO + tm] = uc_ref[0].astype(F32)
        ubuf_ref[CONV_HALO + tm:] = nxt
        _depthwise_conv(ubuf_ref, cbuf_ref, wdw_ref, bdw_ref, tm, d)
        cv = cbuf_ref[...]
        mu = jnp.mean(cv, axis=-1, keepdims=True)
        cc = cv - mu
        var = jnp.mean(cc * cc, axis=-1, keepdims=True)
        yn = cc * lax.rsqrt(var + EPS) * lng_ref[...] + lnb_ref[...]
        o = (yn * jax.nn.sigmoid(yn)).astype(BF16)
    else:
        o = o_ref[0]
    y = _dot(o, wo_ref[...])
    if has_bias:
        y = y + bo_ref[...]
    x1 = x + mod[2:3] * y
    h2 = _modulate(x1, g2_ref[...], mod[3:4], mod[4:5]).astype(BF16)
    acc = jnp.zeros((tm, d), F32)
    d_ff = wup_ref.shape[1]
    for c0 in range(0, d_ff, d_ff_chunk):
        up = jnp.maximum(_dot(h2, wup_ref[:, c0:c0 + d_ff_chunk]), 0.0)
        acc = acc + _dot((up * up).astype(BF16), wdn_ref[c0:c0 + d_ff_chunk, :])
    out_ref[0] = x1 + mod[5:6] * acc


def _post_call(x, mixed, mod, w_o, b_o, gain2, w_up, w_down, conv_params, *, tm):
    b, t, d = x.shape
    d_in = w_o.shape[0]
    d_ff = w_up.shape[1]
    conv = conv_params is not None
    has_bias = b_o is not None
    tok = lambda bi, j: (bi, j, 0)
    in_specs = [pl.BlockSpec((1, tm, d), tok)]
    args = [x]
    scratch = []
    if conv:
        hb = tm // CONV_HALO
        last = t // CONV_HALO - 1
        in_specs += [pl.BlockSpec((1, CONV_HALO, d), lambda bi, j: (bi, jnp.maximum(j * hb - 1, 0), 0)),
                     pl.BlockSpec((1, tm, d), tok),
                     pl.BlockSpec((1, CONV_HALO, d), lambda bi, j: (bi, jnp.minimum((j + 1) * hb, last), 0)),
                     _resident((CONV_WIDTH, d)), _resident((1, d)), _resident((1, d)), _resident((1, d))]
        args += [mixed, mixed, mixed] + list(conv_params)
        scratch = [pltpu.VMEM((tm + 2 * CONV_HALO, d), F32), pltpu.VMEM((tm, d), F32)]
    else:
        in_specs += [pl.BlockSpec((1, tm, d_in), tok)]
        args += [mixed]
    in_specs += [pl.BlockSpec((1, 6, d), lambda bi, j: (bi, 0, 0)), _resident((d_in, d))]
    args += [mod, w_o]
    if has_bias:
        in_specs += [_resident((1, d))]
        args += [b_o]
    in_specs += [_resident((1, d)), _resident((d, d_ff)), _resident((d_ff, d))]
    args += [gain2, w_up, w_down]
    return pl.pallas_call(
        functools.partial(_post_kernel, conv=conv, has_bias=has_bias, d_ff_chunk=min(d_ff, 1024)),
        out_shape=jax.ShapeDtypeStruct((b, t, d), F32),
        grid=(b, t // tm),
        in_specs=in_specs,
        out_specs=pl.BlockSpec((1, tm, d), tok),
        scratch_shapes=scratch,
        compiler_params=_params(2),
        name="post_mlp",
    )(*args)


def _rope_tables(t):
    rows = t // GRID_W
    row = jnp.repeat(jnp.arange(rows, dtype=F32), GRID_W)
    col = jnp.tile(jnp.arange(GRID_W, dtype=F32), rows)
    half = HEAD_DIM // 2
    inv = 1.0 / jnp.power(ROPE_THETA, jnp.arange(0, half, 2, dtype=F32) / half)
    ang = jnp.concatenate([row[:, None] * inv, col[:, None] * inv], axis=-1)
    cos, sin = jnp.cos(ang), jnp.sin(ang)
    reps = LANES // HEAD_DIM
    return (jnp.tile(jnp.concatenate([cos, cos], axis=1), (1, reps)),
            jnp.tile(jnp.concatenate([-sin, sin], axis=1), (1, reps)))


def _split_pairs_perm(n_heads):
    within = jnp.concatenate([jnp.arange(0, HEAD_DIM, 2), jnp.arange(1, HEAD_DIM, 2)])
    return (jnp.arange(n_heads)[:, None] * HEAD_DIM + within[None, :]).reshape(-1)


def _prep_qkv(w_qkv, q_g, k_g, n_q, n_k):
    hq, hk = n_q // HEAD_DIM, n_k // HEAD_DIM
    perm = _split_pairs_perm(hq + hk)
    w = jnp.take(w_qkv, perm, axis=1).astype(BF16)
    wvt = w_qkv[:, n_q + n_k:].T.astype(BF16)
    within = perm[:HEAD_DIM]
    scale = LOG2E / math.sqrt(HEAD_DIM)
    head_gain = jnp.concatenate([jnp.tile(q_g[within] * scale, hq), jnp.tile(k_g[within], hk)])
    return w, wvt, head_gain.reshape(1, n_q + n_k).astype(F32)


def _ones_block_diag():
    idx = jnp.arange(MXU_DIM) // HEAD_DIM
    return (idx[:, None] == idx[None, :]).astype(BF16)


def kernel(x, c, ctx, c_ctx, norm1_g, norm2_g, mod_w, mod_b, mlp_up, mlp_down,
           gqa_w_qkv, gqa_q_g, gqa_k_g, gqa_w_o,
           conv_w_pw1, conv_b_pw1, conv_w_dw, conv_b_dw, conv_ln_g, conv_ln_b, conv_w_pw2, conv_b_pw2,
           diff_w_qkv, diff_q_g, diff_k_g, diff_lam_q1, diff_lam_k1, diff_lam_q2, diff_lam_k2,
           diff_subln_g, diff_w_o,
           swa_w_qkv, swa_q_g, swa_k_g, swa_sink, swa_w_o):
    b, t, d = x.shape
    n_ctx = ctx.shape[1]
    depth = norm1_g.shape[0]
    assert depth == N_MIXERS and t % GRID_W == 0

    tm = _pick(t, (512, 256, 128))
    tq = _pick(t, (256, 128))
    tq_diff = _pick(t, (512, 256, 128))
    t_all = n_ctx + t
    tk = _pick(t_all, (768, 512, 256, 128))
    tkc = _pick(n_ctx, (768, 512, 256, 128))

    cv = jnp.concatenate([c, c_ctx[None], jnp.zeros((8 - b - 1, d), F32)], axis=0)
    mod_all = _mod_call(cv, mod_w, mod_b).reshape(depth, 8, 6, d)
    rope = _rope_tables(t)
    ones_bd = _ones_block_diag()
    row = lambda v: v.reshape(1, -1).astype(F32)

    xc = ctx
    for i in range(depth):
        m, j = i % N_MIXERS, i // N_MIXERS
        need_ctx = i < depth - 1
        mod_l = mod_all[i, :b]
        mod_c = jnp.broadcast_to(mod_all[i, b], (b, 6, d))
        g1, g2 = row(norm1_g[i]), row(norm2_g[i])
        w_up, w_down = mlp_up[i].astype(BF16), mlp_down[i].astype(BF16)
        conv_params = None
        b_o = None
        if m == 1:
            w1, b1 = conv_w_pw1[j].astype(BF16), row(conv_b_pw1[j])
            mixed = _glu_call(x, mod_l, g1, w1, b1, tm=tm)
            mixed_c = _glu_call(xc, mod_c, g1, w1, b1, tm=n_ctx) if need_ctx else None
            conv_params = (conv_w_dw[j].astype(F32), row(conv_b_dw[j]), row(conv_ln_g[j]), row(conv_ln_b[j]))
            w_o, b_o = conv_w_pw2[j].astype(BF16), row(conv_b_pw2[j])
        else:
            if m == 2:
                w_qkv, q_g, k_g, w_o = diff_w_qkv[j], diff_q_g[j], diff_k_g[j], diff_w_o[j]
                n_q = n_k = w_qkv.shape[1] // 3
                q_half = tuple(h % 2 for h in range(n_q // HEAD_DIM))
            else:
                w_qkv, q_g, k_g, w_o = ((gqa_w_qkv[j], gqa_q_g[j], gqa_k_g[j], gqa_w_o[j]) if m == 0 else
                                        (swa_w_qkv[j], swa_q_g[j], swa_k_g[j], swa_w_o[j]))
                n_q = w_o.shape[0]
                n_k = (w_qkv.shape[1] - n_q) // 2
                group = n_q // n_k
                q_half = tuple((h // group) % 2 for h in range(n_q // HEAD_DIM))
            w_o = w_o.astype(BF16)
            w_p, wvt, head_gain = _prep_qkv(w_qkv, q_g, k_g, n_q, n_k)
            proj = functools.partial(_proj_call, gain=g1, w=w_p, wvt=wvt, head_gain=head_gain,
                                     ones_bd=ones_bd, n_q=n_q, n_k=n_k, q_half=q_half)
            q, k, vt = proj(x, mod_l, rope=rope, tm=tm)
            qc, kc, vtc = proj(xc, mod_c, rope=None, tm=n_ctx)
            k_all = jnp.concatenate([kc, k], axis=1)
            vt_all = jnp.concatenate([vtc, vt], axis=2)
            mixed_c = None
            if m == 0:
                mixed = _gqa_call(q, k_all, vt_all, group=group, tq=tq, tk=tk)
                if need_ctx:
                    mixed_c = _gqa_call(qc, kc, vtc, group=group, tq=n_ctx, tk=tkc)
            elif m == 2:
                lam_init = 0.8 - 0.6 * math.exp(-0.3 * i)
                lam_vecs = jnp.stack([diff_lam_q1[j], diff_lam_k1[j], diff_lam_q2[j], diff_lam_k2[j]]).astype(F32)
                sg = diff_subln_g[j].reshape(-1, 1).astype(F32)
                mixed = _diff_call(q, k_all, vt_all, lam_vecs, sg, lam_init=lam_init, tq=tq_diff, tk=tk)
                if need_ctx:
                    mixed_c = _diff_call(qc, kc, vtc, lam_vecs, sg, lam_init=lam_init, tq=n_ctx, tk=tkc)
            else:
                assert not need_ctx, "windowed layer with a context update is not supported"
                sink_rows = jnp.repeat(swa_sink[j].astype(F32).reshape(n_k // HEAD_DIM, 1, group), tq, axis=2)
                mixed = _window_call(q, k_all, vt_all, sink_rows, group=group, n_ctx=n_ctx, tq=tq)
        x = _post_call(x, mixed, mod_l, w_o, b_o, g2, w_up, w_down, conv_params, tm=tm)
        if need_ctx:
            xc = _post_call(xc, mixed_c, mod_c, w_o, b_o, g2, w_up, w_down, conv_params, tm=n_ctx)
    return x
```

```python
import functools
import math

import jax
import jax.numpy as jnp
from jax import lax
from jax.experimental import pallas as pl
from jax.experimental.pallas import tpu as pltpu

F32 = jnp.float32
BF16 = jnp.bfloat16

GRID_W = 64
HEAD_DIM = 64
N_MIXERS = 4
CONV_WIDTH = 31
WINDOW = 128
ROPE_THETA = 10000.0
EPS = 1e-6
NEG_INF = -1e30
LOG2E = math.log2(math.e)

LANES = 128
MXU_DIM = 256
VMEM_LIMIT_BYTES = 56 * 1024 * 1024

CONV_HALO = 16
CONV_ROWS = 32


def _params(n_axes):
    return pltpu.CompilerParams(dimension_semantics=("parallel",) * n_axes,
                                vmem_limit_bytes=VMEM_LIMIT_BYTES)


def _resident(shape):
    nd = len(shape)
    return pl.BlockSpec(shape, lambda *_: (0,) * nd, pipeline_mode=pl.Buffered(1))


def _pick(n, candidates):
    for cand in candidates:
        if n % cand == 0:
            return cand
    return n


def _modulate(x, gain, shift, scale):
    ms = jnp.mean(x * x, axis=-1, keepdims=True)
    return (x * lax.rsqrt(ms + EPS) * gain) * (1.0 + scale) + shift


def _dot(a, b):
    return jnp.dot(a, b, preferred_element_type=F32)


def _dot_nt(a, b):
    return lax.dot_general(a, b, (((1,), (1,)), ((), ())), preferred_element_type=F32)


def _mod_kernel(cv_ref, w_ref, b_ref, o_ref):
    cv = cv_ref[...]
    s = cv * jax.nn.sigmoid(cv)
    o_ref[0] = _dot(s.astype(BF16), w_ref[0].astype(BF16)) + b_ref[0]


def _mod_call(cv, mod_w, mod_b):
    depth, d, n = mod_w.shape
    rows = cv.shape[0]
    tn = _pick(n, (1536, 1024, 512))
    return pl.pallas_call(
        _mod_kernel,
        out_shape=jax.ShapeDtypeStruct((depth, rows, n), F32),
        grid=(depth, n // tn),
        in_specs=[pl.BlockSpec((rows, d), lambda i, j: (0, 0)),
                  pl.BlockSpec((1, d, tn), lambda i, j: (i, 0, j)),
                  pl.BlockSpec((1, 1, tn), lambda i, j: (i, 0, j))],
        out_specs=pl.BlockSpec((1, rows, tn), lambda i, j: (i, 0, j)),
        compiler_params=_params(2),
        name="modulation",
    )(cv, mod_w, mod_b.reshape(depth, 1, n))


def _proj_kernel(*refs, n_q, n_k, q_half, use_rope):
    if use_rope:
        (x_ref, mod_ref, g_ref, w_ref, wvt_ref, hg_ref, e_ref, cos_ref, sin_ref, q_ref, k_ref, vt_ref) = refs
    else:
        (x_ref, mod_ref, g_ref, w_ref, wvt_ref, hg_ref, e_ref, q_ref, k_ref, vt_ref) = refs
    x = x_ref[0]
    mod = mod_ref[0]
    tm = x.shape[0]
    h = _modulate(x, g_ref[...], mod[0:1], mod[1:2]).astype(BF16)
    qk = _dot(h, w_ref[...])
    vt_ref[0] = _dot_nt(wvt_ref[...], h).astype(BF16)
    n_qk = n_q + n_k
    sq = qk * qk
    hi = sq.astype(BF16)
    lo = (sq - hi.astype(F32)).astype(BF16)
    ones_bd = e_ref[...]
    ss = jnp.concatenate(
        [_dot(hi[:, t:t + MXU_DIM], ones_bd) + _dot(lo[:, t:t + MXU_DIM], ones_bd)
         for t in range(0, n_qk, MXU_DIM)], axis=1)
    qk = qk * lax.rsqrt(ss * (1.0 / HEAD_DIM) + EPS) * hg_ref[...]

    lane = lax.broadcasted_iota(jnp.int32, (tm, LANES), 1)
    tiles = [qk[:, t:t + LANES] for t in range(0, n_qk, LANES)]
    if use_rope:
        first = (lane % HEAD_DIM) < (HEAD_DIM // 2)
        cos = cos_ref[...]
        sin = sin_ref[...]
        tiles = [t * cos + jnp.where(first, pltpu.roll(t, LANES - HEAD_DIM // 2, axis=1),
                                     pltpu.roll(t, HEAD_DIM // 2, axis=1)) * sin
                 for t in tiles]
    low = lane < HEAD_DIM
    for j in range(n_q // HEAD_DIM):
        t = tiles[j // 2]
        if j % 2 != q_half[j]:
            t = pltpu.roll(t, HEAD_DIM, axis=1)
        keep = low if q_half[j] == 0 else jnp.logical_not(low)
        q_ref[0, j] = jnp.where(keep, t, 0.0).astype(BF16)
    k_ref[0] = jnp.concatenate(tiles[n_q // LANES:], axis=1).astype(BF16)


def _proj_call(x, mod, gain, w, wvt, head_gain, ones_bd, rope, *, n_q, n_k, q_half, tm):
    b, t, d = x.shape
    n_v = wvt.shape[0]
    hq = n_q // HEAD_DIM
    use_rope = rope is not None
    in_specs = [pl.BlockSpec((1, tm, d), lambda bi, j: (bi, j, 0)),
                pl.BlockSpec((1, 6, d), lambda bi, j: (bi, 0, 0)),
                _resident((1, d)),
                _resident((d, n_q + n_k)),
                _resident((n_v, d)),
                _resident((1, n_q + n_k)),
                _resident((MXU_DIM, MXU_DIM))]
    args = [x, mod, gain, w, wvt, head_gain, ones_bd]
    if use_rope:
        in_specs += [pl.BlockSpec((tm, LANES), lambda bi, j: (j, 0)),
                     pl.BlockSpec((tm, LANES), lambda bi, j: (j, 0))]
        args += list(rope)
    return pl.pallas_call(
        functools.partial(_proj_kernel, n_q=n_q, n_k=n_k, q_half=q_half, use_rope=use_rope),
        out_shape=(jax.ShapeDtypeStruct((b, hq, t, LANES), BF16),
                   jax.ShapeDtypeStruct((b, t, n_k), BF16),
                   jax.ShapeDtypeStruct((b, n_v, t), BF16)),
        grid=(b, t // tm),
        in_specs=in_specs,
        out_specs=(pl.BlockSpec((1, hq, tm, LANES), lambda bi, j: (bi, 0, j, 0)),
                   pl.BlockSpec((1, tm, n_k), lambda bi, j: (bi, j, 0)),
                   pl.BlockSpec((1, n_v, tm), lambda bi, j: (bi, 0, j))),
        compiler_params=_params(2),
        name="attn_proj",
    )(*args)


ONES_ROWS = 16
FLASH_UNROLL = 4


def _flash_loop(load_q, k_ref, load_vt, s_ref, cm_ref, m_ref, acc_ref, *, n_kv, tk):
    m_ref[...] = jnp.full(m_ref.shape, NEG_INF, F32)
    acc_ref[...] = jnp.zeros(acc_ref.shape, F32)
    ones = jnp.ones((ONES_ROWS, tk), BF16)
    m_cols = s_ref.shape[2]
    col_blocks = [slice(c0, c0 + MXU_DIM) for c0 in range(0, m_cols, MXU_DIM)]

    def produce(kc, q, slot, cols):
        s = _dot_nt(kc, q[cols])
        s_ref[slot, :, cols] = s
        cm_ref[slot, :, cols] = jnp.max(s.reshape(tk // 8, 8, MXU_DIM), axis=0)

    def consume(vt1, slot, cols):
        m_prev = m_ref[:, cols]
        m_new = jnp.maximum(m_prev, jnp.max(cm_ref[slot, :, cols], axis=0, keepdims=True))
        alpha = jnp.exp2(m_prev - m_new)
        p = jnp.exp2(s_ref[slot, :, cols] - m_new).astype(BF16)
        acc_ref[:, cols] = alpha * acc_ref[:, cols] + _dot(vt1, p)
        m_ref[:, cols] = m_new

    def step(j_prod, slot_prod, j_cons, slot_cons):
        chunk_start = lambda j: j * tk if isinstance(j, int) else pl.multiple_of(j * tk, tk)
        if j_prod is not None:
            kc = k_ref[0, pl.ds(chunk_start(j_prod), tk), :]
            q = load_q()
        if j_cons is not None:
            vt1 = jnp.concatenate([load_vt(chunk_start(j_cons), tk), ones], axis=0)
        for cols in col_blocks:
            if j_prod is not None:
                produce(kc, q, slot_prod, cols)
            if j_cons is not None:
                consume(vt1, slot_cons, cols)

    step(0, 0, None, None)
    trips = (n_kv - 1) // FLASH_UNROLL if n_kv - 1 > FLASH_UNROLL else 0

    def body(trip, carry):
        j0 = trip * FLASH_UNROLL
        for u in range(FLASH_UNROLL):
            step(j0 + u + 1, (u + 1) % 2, j0 + u, u % 2)
        return carry

    if trips:
        lax.fori_loop(0, trips, body, 0)
    for j in range(trips * FLASH_UNROLL, n_kv - 1):
        step(j + 1, (j + 1) % 2, j, j % 2)
    step(None, None, n_kv - 1, (n_kv - 1) % 2)


def _store_heads_t(o_ref, ot, par, tq, group):
    for gg in range(group // 2):
        slab = jnp.concatenate([ot[:, (2 * gg) * tq:(2 * gg + 1) * tq],
                                ot[:, (2 * gg + 1) * tq:(2 * gg + 2) * tq]], axis=0)
        c0 = (par * (group // 2) + gg) * LANES
        o_ref[0, :, c0:c0 + LANES] = slab.T.astype(o_ref.dtype)


def _flash_scratch(m_cols, tk, v_rows):
    return [pltpu.VMEM((2, tk, m_cols), F32),
            pltpu.VMEM((2, 8, m_cols), F32),
            pltpu.VMEM((1, m_cols), F32),
            pltpu.VMEM((v_rows + ONES_ROWS, m_cols), F32)]


def _gqa_kernel(q_ref, k_ref, vt_ref, o_ref, s_ref, cm_ref, m_ref, acc_ref, *, group, n_kv, tk):
    tq = q_ref.shape[2]
    for par in range(2):
        load_q = lambda: q_ref[0, par * group:(par + 1) * group].reshape(group * tq, LANES)
        load_vt = lambda start, size: vt_ref[0, par * HEAD_DIM:(par + 1) * HEAD_DIM, pl.ds(start, size)]
        _flash_loop(load_q, k_ref, load_vt, s_ref, cm_ref, m_ref, acc_ref, n_kv=n_kv, tk=tk)
        ot = acc_ref[0:HEAD_DIM, :] / acc_ref[HEAD_DIM:HEAD_DIM + 1, :]
        _store_heads_t(o_ref, ot, par, tq, group)


def _gqa_call(q, k_all, vt_all, *, group, tq, tk):
    b, hq, t, _ = q.shape
    t_all, n_k = k_all.shape[1:]
    pairs = n_k // LANES
    heads = 2 * group
    return pl.pallas_call(
        functools.partial(_gqa_kernel, group=group, n_kv=t_all // tk, tk=tk),
        out_shape=jax.ShapeDtypeStruct((b, t, hq * HEAD_DIM), BF16),
        grid=(b, pairs, t // tq),
        in_specs=[pl.BlockSpec((1, heads, tq, LANES), lambda bi, p, i: (bi, p, i, 0)),
                  pl.BlockSpec((1, t_all, LANES), lambda bi, p, i: (bi, 0, p)),
                  pl.BlockSpec((1, LANES, t_all), lambda bi, p, i: (bi, p, 0))],
        out_specs=pl.BlockSpec((1, tq, heads * HEAD_DIM), lambda bi, p, i: (bi, i, p)),
        scratch_shapes=_flash_scratch(group * tq, tk, HEAD_DIM),
        compiler_params=_params(3),
        name="gqa_attention",
    )(q, k_all, vt_all)


def _diff_kernel(q_ref, k_ref, vt_ref, lam_ref, sg_ref, o_ref, s_ref, cm_ref, m_ref, acc_ref, *,
                 lam_init, n_kv, tk):
    tq = q_ref.shape[2]
    dv = vt_ref.shape[1]
    load_q = lambda: q_ref[0].reshape(2 * tq, LANES)
    load_vt = lambda start, size: vt_ref[0, :, pl.ds(start, size)]
    _flash_loop(load_q, k_ref, load_vt, s_ref, cm_ref, m_ref, acc_ref, n_kv=n_kv, tk=tk)
    ot = acc_ref[0:dv, :] / acc_ref[dv:dv + 1, :]
    lv = lam_ref[...]
    lam = (jnp.exp(jnp.sum(lv[0:1] * lv[1:2], axis=1, keepdims=True))
           - jnp.exp(jnp.sum(lv[2:3] * lv[3:4], axis=1, keepdims=True)) + lam_init)
    ot = ot[:, :tq] - lam * ot[:, tq:]
    ms = jnp.mean(ot * ot, axis=0, keepdims=True)
    ot = (ot * lax.rsqrt(ms + EPS) * sg_ref[...]) * (1.0 - lam_init)
    o_ref[0] = ot.T.astype(o_ref.dtype)


def _diff_call(q, k_all, vt_all, lam_vecs, subln_g, *, lam_init, tq, tk):
    b, hq, t, _ = q.shape
    t_all, n_k = k_all.shape[1:]
    heads = n_k // LANES
    dv = vt_all.shape[1] // heads
    return pl.pallas_call(
        functools.partial(_diff_kernel, lam_init=lam_init, n_kv=t_all // tk, tk=tk),
        out_shape=jax.ShapeDtypeStruct((b, t, heads * dv), BF16),
        grid=(b, heads, t // tq),
        in_specs=[pl.BlockSpec((1, 2, tq, LANES), lambda bi, h, i: (bi, h, i, 0)),
                  pl.BlockSpec((1, t_all, LANES), lambda bi, h, i: (bi, 0, h)),
                  pl.BlockSpec((1, dv, t_all), lambda bi, h, i: (bi, h, 0)),
                  pl.BlockSpec((4, HEAD_DIM), lambda bi, h, i: (0, 0)),
                  pl.BlockSpec((dv, 1), lambda bi, h, i: (0, 0))],
        out_specs=pl.BlockSpec((1, tq, dv), lambda bi, h, i: (bi, i, h)),
        scratch_shapes=_flash_scratch(2 * tq, tk, dv),
        compiler_params=_params(3),
        name="diff_attention",
    )(q, k_all, vt_all, lam_vecs, subln_g)


def _window_kernel(q_ref, k_ref, vt_ref, sink_ref, o_ref, *, group, n_ctx, t_lat):
    tq = q_ref.shape[2]
    span = tq + 2 * WINDOW
    i = pl.program_id(2)
    start = jnp.clip(i * tq - WINDOW, 0, t_lat - span)
    row0 = pl.multiple_of(n_ctx + start, math.gcd(WINDOW, n_ctx))
    k_lat = k_ref[0, pl.ds(row0, span), :]
    k_ctx = k_ref[0, 0:n_ctx, :]
    kpos = start + lax.broadcasted_iota(jnp.int32, (span, tq), 0)
    qpos = i * tq + lax.broadcasted_iota(jnp.int32, (span, tq), 1)
    in_band = jnp.abs(kpos - qpos) <= WINDOW
    in_band = jnp.concatenate([in_band] * group, axis=1)
    ones_lat = jnp.ones((ONES_ROWS, span), BF16)
    ones_ctx = jnp.ones((ONES_ROWS, n_ctx), BF16)
    for par in range(2):
        rows = slice(par * HEAD_DIM, (par + 1) * HEAD_DIM)
        q_all = q_ref[0, par * group:(par + 1) * group].reshape(group * tq, LANES)
        s_lat = jnp.where(in_band, _dot_nt(k_lat, q_all), NEG_INF)
        s_ctx = _dot_nt(k_ctx, q_all)
        sink = sink_ref[par] * LOG2E
        m = jnp.maximum(jnp.maximum(jnp.max(s_lat, axis=0, keepdims=True),
                                    jnp.max(s_ctx, axis=0, keepdims=True)), sink)
        e_lat = jnp.exp2(s_lat - m).astype(BF16)
        e_ctx = jnp.exp2(s_ctx - m).astype(BF16)
        vt_lat = jnp.concatenate([vt_ref[0, rows, pl.ds(row0, span)], ones_lat], axis=0)
        vt_ctx = jnp.concatenate([vt_ref[0, rows, 0:n_ctx], ones_ctx], axis=0)
        acc = _dot(vt_lat, e_lat) + _dot(vt_ctx, e_ctx)
        den = acc[HEAD_DIM:HEAD_DIM + 1, :] + jnp.exp2(sink - m)
        _store_heads_t(o_ref, acc[0:HEAD_DIM, :] / den, par, tq, group)


def _window_call(q, k_all, vt_all, sink_rows, *, group, n_ctx, tq):
    b, hq, t, _ = q.shape
    t_all, n_k = k_all.shape[1:]
    pairs = n_k // LANES
    heads = 2 * group
    return pl.pallas_call(
        functools.partial(_window_kernel, group=group, n_ctx=n_ctx, t_lat=t),
        out_shape=jax.ShapeDtypeStruct((b, t, hq * HEAD_DIM), BF16),
        grid=(b, pairs, t // tq),
        in_specs=[pl.BlockSpec((1, heads, tq, LANES), lambda bi, p, i: (bi, p, i, 0)),
                  pl.BlockSpec((1, t_all, LANES), lambda bi, p, i: (bi, 0, p)),
                  pl.BlockSpec((1, LANES, t_all), lambda bi, p, i: (bi, p, 0)),
                  pl.BlockSpec((2, 1, group * tq), lambda bi, p, i: (p, 0, 0))],
        out_specs=pl.BlockSpec((1, tq, heads * HEAD_DIM), lambda bi, p, i: (bi, i, p)),
        compiler_params=_params(3),
        name="window_attention",
    )(q, k_all, vt_all, sink_rows)


def _glu_kernel(x_ref, mod_ref, g_ref, w_ref, b_ref, u_ref):
    x = x_ref[0]
    mod = mod_ref[0]
    d = x.shape[1]
    h = _modulate(x, g_ref[...], mod[0:1], mod[1:2]).astype(BF16)
    ag = _dot(h, w_ref[...]) + b_ref[...]
    u_ref[0] = (ag[:, :d] * jax.nn.sigmoid(ag[:, d:])).astype(u_ref.dtype)


def _glu_call(x, mod, gain, w, bias, *, tm):
    b, t, d = x.shape
    return pl.pallas_call(
        _glu_kernel,
        out_shape=jax.ShapeDtypeStruct((b, t, d), BF16),
        grid=(b, t // tm),
        in_specs=[pl.BlockSpec((1, tm, d), lambda bi, j: (bi, j, 0)),
                  pl.BlockSpec((1, 6, d), lambda bi, j: (bi, 0, 0)),
                  _resident((1, d)),
                  _resident((d, 2 * d)),
                  _resident((1, 2 * d))],
        out_specs=pl.BlockSpec((1, tm, d), lambda bi, j: (bi, j, 0)),
        compiler_params=_params(2),
        name="conv_glu",
    )(x, mod, gain, w, bias)


def _depthwise_conv(ubuf_ref, cbuf_ref, wdw_ref, bdw_ref, tm, d):
    win_rows = CONV_ROWS + 2 * CONV_HALO
    base = CONV_HALO - CONV_WIDTH // 2
    lane_chunk = 2 * LANES

    def step(r, carry):
        r0 = pl.multiple_of(r * CONV_ROWS, CONV_ROWS)
        for c0 in range(0, d, lane_chunk):
            win = ubuf_ref[pl.ds(r0, win_rows), c0:c0 + lane_chunk]
            acc = jnp.zeros((CONV_ROWS, lane_chunk), F32) + bdw_ref[:, c0:c0 + lane_chunk]
            for sub in range(8):
                shifted = win if sub == 0 else pltpu.roll(win, win_rows - sub, axis=0)
                for al in range(0, win_rows - CONV_ROWS + 1, 8):
                    k = al + sub - base
                    if 0 <= k < CONV_WIDTH:
                        acc = acc + wdw_ref[k:k + 1, c0:c0 + lane_chunk] * shifted[al:al + CONV_ROWS]
            cbuf_ref[pl.ds(r0, CONV_ROWS), c0:c0 + lane_chunk] = acc
        return carry

    lax.fori_loop(0, tm // CONV_ROWS, step, 0)


def _post_kernel(*refs, conv, has_bias, d_ff_chunk):
    it = iter(refs)
    x_ref = next(it)
    if conv:
        up_ref, uc_ref, un_ref = next(it), next(it), next(it)
        wdw_ref, bdw_ref, lng_ref, lnb_ref = next(it), next(it), next(it), next(it)
    else:
        o_ref = next(it)
    mod_ref, wo_ref = next(it), next(it)
    bo_ref = next(it) if has_bias else None
    g2_ref, wup_ref, wdn_ref, out_ref = next(it), next(it), next(it), next(it)
    if conv:
        ubuf_ref, cbuf_ref = next(it), next(it)

    x = x_ref[0]
    mod = mod_ref[0]
    tm, d = x.shape
    if conv:
        j = pl.program_id(1)
        prev = jnp.where(j > 0, up_ref[0].astype(F32), 0.0)
        nxt = jnp.where(j < pl.num_programs(1) - 1, un_ref[0].astype(F32), 0.0)
        ubuf_ref[0:CONV_HALO] = prev
        ubuf_ref[CONV_HALO:CONV_HALO + tm] = uc_ref[0].astype(F32)
        ubuf_ref[CONV_HALO + tm:] = nxt
        _depthwise_conv(ubuf_ref, cbuf_ref, wdw_ref, bdw_ref, tm, d)
        cv = cbuf_ref[...]
        mu = jnp.mean(cv, axis=-1, keepdims=True)
        cc = cv - mu
        var = jnp.mean(cc * cc, axis=-1, keepdims=True)
        yn = cc * lax.rsqrt(var + EPS) * lng_ref[...] + lnb_ref[...]
        o = (yn * jax.nn.sigmoid(yn)).astype(BF16)
    else:
        o = o_ref[0]
    y = _dot(o, wo_ref[...])
    if has_bias:
        y = y + bo_ref[...]
    x1 = x + mod[2:3] * y
    h2 = _modulate(x1, g2_ref[...], mod[3:4], mod[4:5]).astype(BF16)
    acc = jnp.zeros((tm, d), F32)
    d_ff = wup_ref.shape[1]
    for c0 in range(0, d_ff, d_ff_chunk):
        up = jnp.maximum(_dot(h2, wup_ref[:, c0:c0 + d_ff_chunk]), 0.0)
        acc = acc + _dot((up * up).astype(BF16), wdn_ref[c0:c0 + d_ff_chunk, :])
    out_ref[0] = x1 + mod[5:6] * acc


def _post_call(x, mixed, mod, w_o, b_o, gain2, w_up, w_down, conv_params, *, tm):
    b, t, d = x.shape
    d_in = w_o.shape[0]
    d_ff = w_up.shape[1]
    conv = conv_params is not None
    has_bias = b_o is not None
    tok = lambda bi, j: (bi, j, 0)
    in_specs = [pl.BlockSpec((1, tm, d), tok)]
    args = [x]
    scratch = []
    if conv:
        hb = tm // CONV_HALO
        last = t // CONV_HALO - 1
        in_specs += [pl.BlockSpec((1, CONV_HALO, d), lambda bi, j: (bi, jnp.maximum(j * hb - 1, 0), 0)),
                     pl.BlockSpec((1, tm, d), tok),
                     pl.BlockSpec((1, CONV_HALO, d), lambda bi, j: (bi, jnp.minimum((j + 1) * hb, last), 0)),
                     _resident((CONV_WIDTH, d)), _resident((1, d)), _resident((1, d)), _resident((1, d))]
        args += [mixed, mixed, mixed] + list(conv_params)
        scratch = [pltpu.VMEM((tm + 2 * CONV_HALO, d), F32), pltpu.VMEM((tm, d), F32)]
    else:
        in_specs += [pl.BlockSpec((1, tm, d_in), tok)]
        args += [mixed]
    in_specs += [pl.BlockSpec((1, 6, d), lambda bi, j: (bi, 0, 0)), _resident((d_in, d))]
    args += [mod, w_o]
    if has_bias:
        in_specs += [_resident((1, d))]
        args += [b_o]
    in_specs += [_resident((1, d)), _resident((d, d_ff)), _resident((d_ff, d))]
    args += [gain2, w_up, w_down]
    return pl.pallas_call(
        functools.partial(_post_kernel, conv=conv, has_bias=has_bias, d_ff_chunk=min(d_ff, 1024)),
        out_shape=jax.ShapeDtypeStruct((b, t, d), F32),
        grid=(b, t // tm),
        in_specs=in_specs,
        out_specs=pl.BlockSpec((1, tm, d), tok),
        scratch_shapes=scratch,
        compiler_params=_params(2),
        name="post_mlp",
    )(*args)


def _rope_tables(t):
    rows = t // GRID_W
    row = jnp.repeat(jnp.arange(rows, dtype=F32), GRID_W)
    col = jnp.tile(jnp.arange(GRID_W, dtype=F32), rows)
    half = HEAD_DIM // 2
    inv = 1.0 / jnp.power(ROPE_THETA, jnp.arange(0, half, 2, dtype=F32) / half)
    ang = jnp.concatenate([row[:, None] * inv, col[:, None] * inv], axis=-1)
    cos, sin = jnp.cos(ang), jnp.sin(ang)
    reps = LANES // HEAD_DIM
    return (jnp.tile(jnp.concatenate([cos, cos], axis=1), (1, reps)),
            jnp.tile(jnp.concatenate([-sin, sin], axis=1), (1, reps)))


def _split_pairs_perm(n_heads):
    within = jnp.concatenate([jnp.arange(0, HEAD_DIM, 2), jnp.arange(1, HEAD_DIM, 2)])
    return (jnp.arange(n_heads)[:, None] * HEAD_DIM + within[None, :]).reshape(-1)


def _prep_qkv(w_qkv, q_g, k_g, n_q, n_k):
    hq, hk = n_q // HEAD_DIM, n_k // HEAD_DIM
    perm = _split_pairs_perm(hq + hk)
    w = jnp.take(w_qkv, perm, axis=1).astype(BF16)
    wvt = w_qkv[:, n_q + n_k:].T.astype(BF16)
    within = perm[:HEAD_DIM]
    scale = LOG2E / math.sqrt(HEAD_DIM)
    head_gain = jnp.concatenate([jnp.tile(q_g[within] * scale, hq), jnp.tile(k_g[within], hk)])
    return w, wvt, head_gain.reshape(1, n_q + n_k).astype(F32)


def _ones_block_diag():
    idx = jnp.arange(MXU_DIM) // HEAD_DIM
    return (idx[:, None] == idx[None, :]).astype(BF16)


def kernel(x, c, ctx, c_ctx, norm1_g, norm2_g, mod_w, mod_b, mlp_up, mlp_down,
           gqa_w_qkv, gqa_q_g, gqa_k_g, gqa_w_o,
           conv_w_pw1, conv_b_pw1, conv_w_dw, conv_b_dw, conv_ln_g, conv_ln_b, conv_w_pw2, conv_b_pw2,
           diff_w_qkv, diff_q_g, diff_k_g, diff_lam_q1, diff_lam_k1, diff_lam_q2, diff_lam_k2,
           diff_subln_g, diff_w_o,
           swa_w_qkv, swa_q_g, swa_k_g, swa_sink, swa_w_o):
    b, t, d = x.shape
    n_ctx = ctx.shape[1]
    depth = norm1_g.shape[0]
    assert depth == N_MIXERS and t % GRID_W == 0

    tm = _pick(t, (512, 256, 128))
    tq = _pick(t, (256, 128))
    tq_diff = _pick(t, (512, 256, 128))
    t_all = n_ctx + t
    tk = _pick(t_all, (768, 512, 256, 128))
    tkc = _pick(n_ctx, (768, 512, 256, 128))

    cv = jnp.concatenate([c, c_ctx[None], jnp.zeros((8 - b - 1, d), F32)], axis=0)
    mod_all = _mod_call(cv, mod_w, mod_b).reshape(depth, 8, 6, d)
    rope = _rope_tables(t)
    ones_bd = _ones_block_diag()
    row = lambda v: v.reshape(1, -1).astype(F32)

    xc = ctx
    for i in range(depth):
        m, j = i % N_MIXERS, i // N_MIXERS
        need_ctx = i < depth - 1
        mod_l = mod_all[i, :b]
        mod_c = jnp.broadcast_to(mod_all[i, b], (b, 6, d))
        g1, g2 = row(norm1_g[i]), row(norm2_g[i])
        w_up, w_down = mlp_up[i].astype(BF16), mlp_down[i].astype(BF16)
        conv_params = None
        b_o = None
        if m == 1:
            w1, b1 = conv_w_pw1[j].astype(BF16), row(conv_b_pw1[j])
            mixed = _glu_call(x, mod_l, g1, w1, b1, tm=tm)
            mixed_c = _glu_call(xc, mod_c, g1, w1, b1, tm=n_ctx) if need_ctx else None
            conv_params = (conv_w_dw[j].astype(F32), row(conv_b_dw[j]), row(conv_ln_g[j]), row(conv_ln_b[j]))
            w_o, b_o = conv_w_pw2[j].astype(BF16), row(conv_b_pw2[j])
        else:
            if m == 2:
                w_qkv, q_g, k_g, w_o = diff_w_qkv[j], diff_q_g[j], diff_k_g[j], diff_w_o[j]
                n_q = n_k = w_qkv.shape[1] // 3
                q_half = tuple(h % 2 for h in range(n_q // HEAD_DIM))
            else:
                w_qkv, q_g, k_g, w_o = ((gqa_w_qkv[j], gqa_q_g[j], gqa_k_g[j], gqa_w_o[j]) if m == 0 else
                                        (swa_w_qkv[j], swa_q_g[j], swa_k_g[j], swa_w_o[j]))
                n_q = w_o.shape[0]
                n_k = (w_qkv.shape[1] - n_q) // 2
                group = n_q // n_k
                q_half = tuple((h // group) % 2 for h in range(n_q // HEAD_DIM))
            w_o = w_o.astype(BF16)
            w_p, wvt, head_gain = _prep_qkv(w_qkv, q_g, k_g, n_q, n_k)
            proj = functools.partial(_proj_call, gain=g1, w=w_p, wvt=wvt, head_gain=head_gain,
                                     ones_bd=ones_bd, n_q=n_q, n_k=n_k, q_half=q_half)
            q, k, vt = proj(x, mod_l, rope=rope, tm=tm)
            qc, kc, vtc = proj(xc, mod_c, rope=None, tm=n_ctx)
            k_all = jnp.concatenate([kc, k], axis=1)
            vt_all = jnp.concatenate([vtc, vt], axis=2)
            mixed_c = None
            if m == 0:
                mixed = _gqa_call(q, k_all, vt_all, group=group, tq=tq, tk=tk)
                if need_ctx:
                    mixed_c = _gqa_call(qc, kc, vtc, group=group, tq=n_ctx, tk=tkc)
            elif m == 2:
                lam_init = 0.8 - 0.6 * math.exp(-0.3 * i)
                lam_vecs = jnp.stack([diff_lam_q1[j], diff_lam_k1[j], diff_lam_q2[j], diff_lam_k2[j]]).astype(F32)
                sg = diff_subln_g[j].reshape(-1, 1).astype(F32)
                mixed = _diff_call(q, k_all, vt_all, lam_vecs, sg, lam_init=lam_init, tq=tq_diff, tk=tk)
                if need_ctx:
                    mixed_c = _diff_call(qc, kc, vtc, lam_vecs, sg, lam_init=lam_init, tq=n_ctx, tk=tkc)
            else:
                assert not need_ctx, "windowed layer with a context update is not supported"
                sink_rows = jnp.repeat(swa_sink[j].astype(F32).reshape(n_k // HEAD_DIM, 1, group), tq, axis=2)
                mixed = _window_call(q, k_all, vt_all, sink_rows, group=group, n_ctx=n_ctx, tq=tq)
        x = _post_call(x, mixed, mod_l, w_o, b_o, g2, w_up, w_down, conv_params, tm=tm)
        if need_ctx:
            xc = _post_call(xc, mixed_c, mod_c, w_o, b_o, g2, w_up, w_down, conv_params, tm=n_ctx)
    return x
```

```python
import functools
import math

import jax
import jax.numpy as jnp
from jax import lax
from jax.experimental import pallas as pl
from jax.experimental.pallas import tpu as pltpu

F32 = jnp.float32
BF16 = jnp.bfloat16

GRID_W = 64
HEAD_DIM = 64
N_MIXERS = 4
CONV_WIDTH = 31
WINDOW = 128
ROPE_THETA = 10000.0
EPS = 1e-6
NEG_INF = -1e30
LOG2E = math.log2(math.e)

LANES = 128
MXU_DIM = 256
VMEM_LIMIT_BYTES = 56 * 1024 * 1024

CONV_HALO = 16
CONV_ROWS = 32


def _params(n_axes):
    return pltpu.CompilerParams(dimension_semantics=("parallel",) * n_axes,
                                vmem_limit_bytes=VMEM_LIMIT_BYTES)


def _resident(shape):
    nd = len(shape)
    return pl.BlockSpec(shape, lambda *_: (0,) * nd, pipeline_mode=pl.Buffered(1))


def _pick(n, candidates):
    for cand in candidates:
        if n % cand == 0:
            return cand
    return n


def _modulate(x, gain, shift, scale):
    ms = jnp.mean(x * x, axis=-1, keepdims=True)
    return (x * lax.rsqrt(ms + EPS) * gain) * (1.0 + scale) + shift


def _dot(a, b):
    return jnp.dot(a, b, preferred_element_type=F32)


def _dot_nt(a, b):
    return lax.dot_general(a, b, (((1,), (1,)), ((), ())), preferred_element_type=F32)


def _mod_kernel(cv_ref, w_ref, b_ref, o_ref):
    cv = cv_ref[...]
    s = cv * jax.nn.sigmoid(cv)
    o_ref[0] = _dot(s.astype(BF16), w_ref[0].astype(BF16)) + b_ref[0]


def _mod_call(cv, mod_w, mod_b):
    depth, d, n = mod_w.shape
    rows = cv.shape[0]
    tn = _pick(n, (1536, 1024, 512))
    return pl.pallas_call(
        _mod_kernel,
        out_shape=jax.ShapeDtypeStruct((depth, rows, n), F32),
        grid=(depth, n // tn),
        in_specs=[pl.BlockSpec((rows, d), lambda i, j: (0, 0)),
                  pl.BlockSpec((1, d, tn), lambda i, j: (i, 0, j)),
                  pl.BlockSpec((1, 1, tn), lambda i, j: (i, 0, j))],
        out_specs=pl.BlockSpec((1, rows, tn), lambda i, j: (i, 0, j)),
        compiler_params=_params(2),
        name="modulation",
    )(cv, mod_w, mod_b.reshape(depth, 1, n))


def _proj_kernel(*refs, n_q, n_k, q_half, use_rope, n_alias):
    q_ref, k_ref, vt_ref = refs[-3:]
    refs = refs[:len(refs) - 3 - n_alias]
    if use_rope:
        (x_ref, mod_ref, g_ref, w_ref, wvt_ref, hg_ref, e_ref, cos_ref, sin_ref) = refs
    else:
        (x_ref, mod_ref, g_ref, w_ref, wvt_ref, hg_ref, e_ref) = refs
    x = x_ref[0]
    mod = mod_ref[0]
    tm = x.shape[0]
    h = _modulate(x, g_ref[...], mod[0:1], mod[1:2]).astype(BF16)
    qk = _dot(h, w_ref[...])
    vt_ref[0] = _dot_nt(wvt_ref[...], h).astype(BF16)
    n_qk = n_q + n_k
    sq = qk * qk
    hi = sq.astype(BF16)
    lo = (sq - hi.astype(F32)).astype(BF16)
    ones_bd = e_ref[...]
    ss = jnp.concatenate(
        [_dot(hi[:, t:t + MXU_DIM], ones_bd) + _dot(lo[:, t:t + MXU_DIM], ones_bd)
         for t in range(0, n_qk, MXU_DIM)], axis=1)
    qk = qk * lax.rsqrt(ss * (1.0 / HEAD_DIM) + EPS) * hg_ref[...]

    lane = lax.broadcasted_iota(jnp.int32, (tm, LANES), 1)
    tiles = [qk[:, t:t + LANES] for t in range(0, n_qk, LANES)]
    if use_rope:
        first = (lane % HEAD_DIM) < (HEAD_DIM // 2)
        cos = cos_ref[...]
        sin = sin_ref[...]
        tiles = [t * cos + jnp.where(first, pltpu.roll(t, LANES - HEAD_DIM // 2, axis=1),
                                     pltpu.roll(t, HEAD_DIM // 2, axis=1)) * sin
                 for t in tiles]
    low = lane < HEAD_DIM
    for j in range(n_q // HEAD_DIM):
        t = tiles[j // 2]
        if j % 2 != q_half[j]:
            t = pltpu.roll(t, HEAD_DIM, axis=1)
        keep = low if q_half[j] == 0 else jnp.logical_not(low)
        q_ref[0, j] = jnp.where(keep, t, 0.0).astype(BF16)
    k_ref[0] = jnp.concatenate(tiles[n_q // LANES:], axis=1).astype(BF16)


def _proj_call(x, mod, gain, w, wvt, head_gain, ones_bd, rope, *, n_q, n_k, q_half, tm, t_keys,
               kv_buffers=None):
    b, t, d = x.shape
    n_v = wvt.shape[0]
    hq = n_q // HEAD_DIM
    use_rope = rope is not None
    in_specs = [pl.BlockSpec((1, tm, d), lambda bi, j: (bi, j, 0)),
                pl.BlockSpec((1, 6, d), lambda bi, j: (bi, 0, 0)),
                _resident((1, d)),
                _resident((d, n_q + n_k)),
                _resident((n_v, d)),
                _resident((1, n_q + n_k)),
                _resident((MXU_DIM, MXU_DIM))]
    args = [x, mod, gain, w, wvt, head_gain, ones_bd]
    if use_rope:
        in_specs += [pl.BlockSpec((tm, LANES), lambda bi, j: (j, 0)),
                     pl.BlockSpec((tm, LANES), lambda bi, j: (j, 0))]
        args += list(rope)
    aliases = {}
    off = 0
    if kv_buffers is not None:
        off = (t_keys - t) // tm
        assert off * tm + t == t_keys
        aliases = {len(args): 1, len(args) + 1: 2}
        in_specs += [pl.BlockSpec(memory_space=pl.ANY), pl.BlockSpec(memory_space=pl.ANY)]
        args += list(kv_buffers)
    return pl.pallas_call(
        functools.partial(_proj_kernel, n_q=n_q, n_k=n_k, q_half=q_half, use_rope=use_rope,
                          n_alias=len(aliases)),
        out_shape=(jax.ShapeDtypeStruct((b, hq, t, LANES), BF16),
                   jax.ShapeDtypeStruct((b, t_keys, n_k), BF16),
                   jax.ShapeDtypeStruct((b, n_v, t_keys), BF16)),
        grid=(b, t // tm),
        in_specs=in_specs,
        out_specs=(pl.BlockSpec((1, hq, tm, LANES), lambda bi, j: (bi, 0, j, 0)),
                   pl.BlockSpec((1, tm, n_k), lambda bi, j: (bi, j + off, 0)),
                   pl.BlockSpec((1, n_v, tm), lambda bi, j: (bi, 0, j + off))),
        input_output_aliases=aliases,
        compiler_params=_params(2),
        name="attn_proj",
    )(*args)


ONES_ROWS = 16
FLASH_UNROLL = 4


def _flash_loop(load_q, k_ref, load_vt, s_ref, cm_ref, m_ref, acc_ref, *, n_kv, tk):
    m_ref[...] = jnp.full(m_ref.shape, NEG_INF, F32)
    acc_ref[...] = jnp.zeros(acc_ref.shape, F32)
    ones = jnp.ones((ONES_ROWS, tk), BF16)
    m_cols = s_ref.shape[2]
    col_blocks = [slice(c0, c0 + MXU_DIM) for c0 in range(0, m_cols, MXU_DIM)]

    def produce(kc, q, slot, cols):
        s = _dot_nt(kc, q[cols])
        s_ref[slot, :, cols] = s
        cm_ref[slot, :, cols] = jnp.max(s.reshape(tk // 8, 8, MXU_DIM), axis=0)

    def consume(vt1, slot, cols):
        m_prev = m_ref[:, cols]
        m_new = jnp.maximum(m_prev, jnp.max(cm_ref[slot, :, cols], axis=0, keepdims=True))
        alpha = jnp.exp2(m_prev - m_new)
        p = jnp.exp2(s_ref[slot, :, cols] - m_new).astype(BF16)
        acc_ref[:, cols] = alpha * acc_ref[:, cols] + _dot(vt1, p)
        m_ref[:, cols] = m_new

    def step(j_prod, slot_prod, j_cons, slot_cons):
        chunk_start = lambda j: j * tk if isinstance(j, int) else pl.multiple_of(j * tk, tk)
        if j_prod is not None:
            kc = k_ref[0, pl.ds(chunk_start(j_prod), tk), :]
            q = load_q()
        if j_cons is not None:
            vt1 = jnp.concatenate([load_vt(chunk_start(j_cons), tk), ones], axis=0)
        for cols in col_blocks:
            if j_prod is not None:
                produce(kc, q, slot_prod, cols)
            if j_cons is not None:
                consume(vt1, slot_cons, cols)

    step(0, 0, None, None)
    trips = (n_kv - 1) // FLASH_UNROLL if n_kv - 1 > FLASH_UNROLL else 0

    def body(trip, carry):
        j0 = trip * FLASH_UNROLL
        for u in range(FLASH_UNROLL):
            step(j0 + u + 1, (u + 1) % 2, j0 + u, u % 2)
        return carry

    if trips:
        lax.fori_loop(0, trips, body, 0)
    for j in range(trips * FLASH_UNROLL, n_kv - 1):
        step(j + 1, (j + 1) % 2, j, j % 2)
    step(None, None, n_kv - 1, (n_kv - 1) % 2)


def _store_heads_t(o_ref, ot, tq, group):
    for gg in range(group // 2):
        slab = jnp.concatenate([ot[:, (2 * gg) * tq:(2 * gg + 1) * tq],
                                ot[:, (2 * gg + 1) * tq:(2 * gg + 2) * tq]], axis=0)
        o_ref[0, :, gg * LANES:(gg + 1) * LANES] = slab.T.astype(o_ref.dtype)


def _kv_specs(key_rows, key_block, v_rows, kv_per_tile):
    return [pl.BlockSpec((1, key_rows, LANES), lambda bi, h, i: (bi, key_block, h // kv_per_tile)),
            pl.BlockSpec((1, v_rows, key_rows), lambda bi, h, i: (bi, h, key_block))]


def _flash_scratch(m_cols, tk, v_rows):
    return [pltpu.VMEM((2, tk, m_cols), F32),
            pltpu.VMEM((2, 8, m_cols), F32),
            pltpu.VMEM((1, m_cols), F32),
            pltpu.VMEM((v_rows + ONES_ROWS, m_cols), F32)]


def _gqa_kernel(q_ref, k_ref, vt_ref, o_ref, s_ref, cm_ref, m_ref, acc_ref, *, n_kv, tk):
    group, tq = q_ref.shape[1:3]
    load_q = lambda: q_ref[0].reshape(group * tq, LANES)
    load_vt = lambda start, size: vt_ref[0, :, pl.ds(start, size)]
    _flash_loop(load_q, k_ref, load_vt, s_ref, cm_ref, m_ref, acc_ref, n_kv=n_kv, tk=tk)
    ot = acc_ref[0:HEAD_DIM, :] / acc_ref[HEAD_DIM:HEAD_DIM + 1, :]
    _store_heads_t(o_ref, ot, tq, group)


def _gqa_call(q, k_all, vt_all, *, group, tq, tk, key_rows, key_block):
    b, hq, t, _ = q.shape
    return pl.pallas_call(
        functools.partial(_gqa_kernel, n_kv=key_rows // tk, tk=tk),
        out_shape=jax.ShapeDtypeStruct((b, t, hq * HEAD_DIM), BF16),
        grid=(b, hq // group, t // tq),
        in_specs=[pl.BlockSpec((1, group, tq, LANES), lambda bi, h, i: (bi, h, i, 0))]
                 + _kv_specs(key_rows, key_block, HEAD_DIM, LANES // HEAD_DIM),
        out_specs=pl.BlockSpec((1, tq, group * HEAD_DIM), lambda bi, h, i: (bi, i, h)),
        scratch_shapes=_flash_scratch(group * tq, tk, HEAD_DIM),
        compiler_params=_params(3),
        name="gqa_attention",
    )(q, k_all, vt_all)


def _diff_kernel(q_ref, k_ref, vt_ref, lam_ref, sg_ref, o_ref, s_ref, cm_ref, m_ref, acc_ref, *,
                 lam_init, n_kv, tk):
    tq = q_ref.shape[2]
    dv = vt_ref.shape[1]
    load_q = lambda: q_ref[0].reshape(2 * tq, LANES)
    load_vt = lambda start, size: vt_ref[0, :, pl.ds(start, size)]
    _flash_loop(load_q, k_ref, load_vt, s_ref, cm_ref, m_ref, acc_ref, n_kv=n_kv, tk=tk)
    ot = acc_ref[0:dv, :] / acc_ref[dv:dv + 1, :]
    lv = lam_ref[...]
    lam = (jnp.exp(jnp.sum(lv[0:1] * lv[1:2], axis=1, keepdims=True))
           - jnp.exp(jnp.sum(lv[2:3] * lv[3:4], axis=1, keepdims=True)) + lam_init)
    ot = ot[:, :tq] - lam * ot[:, tq:]
    ms = jnp.mean(ot * ot, axis=0, keepdims=True)
    ot = (ot * lax.rsqrt(ms + EPS) * sg_ref[...]) * (1.0 - lam_init)
    o_ref[0] = ot.T.astype(o_ref.dtype)


def _diff_call(q, k_all, vt_all, lam_vecs, subln_g, *, lam_init, tq, tk, key_rows, key_block):
    b, hq, t, _ = q.shape
    heads = k_all.shape[2] // LANES
    dv = vt_all.shape[1] // heads
    return pl.pallas_call(
        functools.partial(_diff_kernel, lam_init=lam_init, n_kv=key_rows // tk, tk=tk),
        out_shape=jax.ShapeDtypeStruct((b, t, heads * dv), BF16),
        grid=(b, heads, t // tq),
        in_specs=[pl.BlockSpec((1, 2, tq, LANES), lambda bi, h, i: (bi, h, i, 0))]
                 + _kv_specs(key_rows, key_block, dv, 1)
                 + [pl.BlockSpec((4, HEAD_DIM), lambda bi, h, i: (0, 0)),
                    pl.BlockSpec((dv, 1), lambda bi, h, i: (0, 0))],
        out_specs=pl.BlockSpec((1, tq, dv), lambda bi, h, i: (bi, i, h)),
        scratch_shapes=_flash_scratch(2 * tq, tk, dv),
        compiler_params=_params(3),
        name="diff_attention",
    )(q, k_all, vt_all, lam_vecs, subln_g)


def _window_kernel(q_ref, k_ref, vt_ref, sink_ref, o_ref, *, n_ctx, t_lat):
    group, tq = q_ref.shape[1:3]
    span = tq + 2 * WINDOW
    i = pl.program_id(2)
    start = pl.multiple_of(jnp.clip(i * tq - WINDOW, 0, t_lat - span), WINDOW)
    k_lat = k_ref[0, pl.ds(start, span), :]
    k_ctx = k_ref[0, t_lat:t_lat + n_ctx, :]
    kpos = start + lax.broadcasted_iota(jnp.int32, (span, tq), 0)
    qpos = i * tq + lax.broadcasted_iota(jnp.int32, (span, tq), 1)
    in_band = jnp.abs(kpos - qpos) <= WINDOW
    in_band = jnp.concatenate([in_band] * group, axis=1)
    q_all = q_ref[0].reshape(group * tq, LANES)
    s_lat = jnp.where(in_band, _dot_nt(k_lat, q_all), NEG_INF)
    s_ctx = _dot_nt(k_ctx, q_all)
    sink = sink_ref[0] * LOG2E
    m = jnp.maximum(jnp.maximum(jnp.max(s_lat, axis=0, keepdims=True),
                                jnp.max(s_ctx, axis=0, keepdims=True)), sink)
    e_lat = jnp.exp2(s_lat - m).astype(BF16)
    e_ctx = jnp.exp2(s_ctx - m).astype(BF16)
    vt_lat = jnp.concatenate([vt_ref[0, :, pl.ds(start, span)], jnp.ones((ONES_ROWS, span), BF16)], axis=0)
    vt_ctx = jnp.concatenate([vt_ref[0, :, t_lat:t_lat + n_ctx], jnp.ones((ONES_ROWS, n_ctx), BF16)], axis=0)
    acc = _dot(vt_lat, e_lat) + _dot(vt_ctx, e_ctx)
    den = acc[HEAD_DIM:HEAD_DIM + 1, :] + jnp.exp2(sink - m)
    _store_heads_t(o_ref, acc[0:HEAD_DIM, :] / den, tq, group)


def _window_call(q, k_all, vt_all, sink_rows, *, group, n_ctx, tq):
    b, hq, t, _ = q.shape
    t_all = k_all.shape[1]
    return pl.pallas_call(
        functools.partial(_window_kernel, n_ctx=n_ctx, t_lat=t),
        out_shape=jax.ShapeDtypeStruct((b, t, hq * HEAD_DIM), BF16),
        grid=(b, hq // group, t // tq),
        in_specs=[pl.BlockSpec((1, group, tq, LANES), lambda bi, h, i: (bi, h, i, 0))]
                 + _kv_specs(t_all, 0, HEAD_DIM, LANES // HEAD_DIM)
                 + [pl.BlockSpec((1, 1, group * tq), lambda bi, h, i: (h, 0, 0))],
        out_specs=pl.BlockSpec((1, tq, group * HEAD_DIM), lambda bi, h, i: (bi, i, h)),
        compiler_params=_params(3),
        name="window_attention",
    )(q, k_all, vt_all, sink_rows)


def _glu_kernel(x_ref, mod_ref, g_ref, w_ref, b_ref, u_ref):
    x = x_ref[0]
    mod = mod_ref[0]
    d = x.shape[1]
    h = _modulate(x, g_ref[...], mod[0:1], mod[1:2]).astype(BF16)
    ag = _dot(h, w_ref[...]) + b_ref[...]
    u_ref[0] = (ag[:, :d] * jax.nn.sigmoid(ag[:, d:])).astype(u_ref.dtype)


def _glu_call(x, mod, gain, w, bias, *, tm):
    b, t, d = x.shape
    return pl.pallas_call(
        _glu_kernel,
        out_shape=jax.ShapeDtypeStruct((b, t, d), BF16),
        grid=(b, t // tm),
        in_specs=[pl.BlockSpec((1, tm, d), lambda bi, j: (bi, j, 0)),
                  pl.BlockSpec((1, 6, d), lambda bi, j: (bi, 0, 0)),
                  _resident((1, d)),
                  _resident((d, 2 * d)),
                  _resident((1, 2 * d))],
        out_specs=pl.BlockSpec((1, tm, d), lambda bi, j: (bi, j, 0)),
        compiler_params=_params(2),
        name="conv_glu",
    )(x, mod, gain, w, bias)


def _depthwise_conv(ubuf_ref, cbuf_ref, wdw_ref, bdw_ref, tm, d):
    win_rows = CONV_ROWS + 2 * CONV_HALO
    base = CONV_HALO - CONV_WIDTH // 2
    lane_chunk = 2 * LANES

    def step(r, carry):
        r0 = pl.multiple_of(r * CONV_ROWS, CONV_ROWS)
        for c0 in range(0, d, lane_chunk):
            win = ubuf_ref[pl.ds(r0, win_rows), c0:c0 + lane_chunk]
            acc = jnp.zeros((CONV_ROWS, lane_chunk), F32) + bdw_ref[:, c0:c0 + lane_chunk]
            for sub in range(8):
                shifted = win if sub == 0 else pltpu.roll(win, win_rows - sub, axis=0)
                for al in range(0, win_rows - CONV_ROWS + 1, 8):
                    k = al + sub - base
                    if 0 <= k < CONV_WIDTH:
                        acc = acc + wdw_ref[k:k + 1, c0:c0 + lane_chunk] * shifted[al:al + CONV_ROWS]
            cbuf_ref[pl.ds(r0, CONV_ROWS), c0:c0 + lane_chunk] = acc
        return carry

    lax.fori_loop(0, tm // CONV_ROWS, step, 0)


def _post_kernel(*refs, conv, has_bias, d_ff_chunk):
    it = iter(refs)
    x_ref = next(it)
    if conv:
        up_ref, uc_ref, un_ref = next(it), next(it), next(it)
        wdw_ref, bdw_ref, lng_ref, lnb_ref = next(it), next(it), next(it), next(it)
    else:
        o_ref = next(it)
    mod_ref, wo_ref = next(it), next(it)
    bo_ref = next(it) if has_bias else None
    g2_ref, wup_ref, wdn_ref, out_ref = next(it), next(it), next(it), next(it)
    if conv:
        ubuf_ref, cbuf_ref = next(it), next(it)

    x = x_ref[0]
    mod = mod_ref[0]
    tm, d = x.shape
    if conv:
        j = pl.program_id(1)
        prev = jnp.where(j > 0, up_ref[0].astype(F32), 0.0)
        nxt = jnp.where(j < pl.num_programs(1) - 1, un_ref[0].astype(F32), 0.0)
        ubuf_ref[0:CONV_HALO] = prev
        ubuf_ref[CONV_HALO:CONV_HALO + tm] = uc_ref[0].astype(F32)
        ubuf_ref[CONV_HALO + tm:] = nxt
        _depthwise_conv(ubuf_ref, cbuf_ref, wdw_ref, bdw_ref, tm, d)
        cv = cbuf_ref[...]
        mu = jnp.mean(cv, axis=-1, keepdims=True)
        cc = cv - mu
        var = jnp.mean(cc * cc, axis=-1, keepdims=True)
        yn = cc * lax.rsqrt(var + EPS) * lng_ref[...] + lnb_ref[...]
        o = (yn * jax.nn.sigmoid(yn)).astype(BF16)
    else:
        o = o_ref[0]
    y = _dot(o, wo_ref[...])
    if has_bias:
        y = y + bo_ref[...]
    x1 = x + mod[2:3] * y
    h2 = _modulate(x1, g2_ref[...], mod[3:4], mod[4:5]).astype(BF16)
    acc = jnp.zeros((tm, d), F32)
    d_ff = wup_ref.shape[1]
    for c0 in range(0, d_ff, d_ff_chunk):
        up = jnp.maximum(_dot(h2, wup_ref[:, c0:c0 + d_ff_chunk]), 0.0)
        acc = acc + _dot((up * up).astype(BF16), wdn_ref[c0:c0 + d_ff_chunk, :])
    out_ref[0] = x1 + mod[5:6] * acc


def _post_call(x, mixed, mod, w_o, b_o, gain2, w_up, w_down, conv_params, *, tm):
    b, t, d = x.shape
    d_in = w_o.shape[0]
    d_ff = w_up.shape[1]
    conv = conv_params is not None
    has_bias = b_o is not None
    tok = lambda bi, j: (bi, j, 0)
    in_specs = [pl.BlockSpec((1, tm, d), tok)]
    args = [x]
    scratch = []
    if conv:
        hb = tm // CONV_HALO
        last = t // CONV_HALO - 1
        in_specs += [pl.BlockSpec((1, CONV_HALO, d), lambda bi, j: (bi, jnp.maximum(j * hb - 1, 0), 0)),
                     pl.BlockSpec((1, tm, d), tok),
                     pl.BlockSpec((1, CONV_HALO, d), lambda bi, j: (bi, jnp.minimum((j + 1) * hb, last), 0)),
                     _resident((CONV_WIDTH, d)), _resident((1, d)), _resident((1, d)), _resident((1, d))]
        args += [mixed, mixed, mixed] + list(conv_params)
        scratch = [pltpu.VMEM((tm + 2 * CONV_HALO, d), F32), pltpu.VMEM((tm, d), F32)]
    else:
        in_specs += [pl.BlockSpec((1, tm, d_in), tok)]
        args += [mixed]
    in_specs += [pl.BlockSpec((1, 6, d), lambda bi, j: (bi, 0, 0)), _resident((d_in, d))]
    args += [mod, w_o]
    if has_bias:
        in_specs += [_resident((1, d))]
        args += [b_o]
    in_specs += [_resident((1, d)), _resident((d, d_ff)), _resident((d_ff, d))]
    args += [gain2, w_up, w_down]
    return pl.pallas_call(
        functools.partial(_post_kernel, conv=conv, has_bias=has_bias, d_ff_chunk=min(d_ff, 1024)),
        out_shape=jax.ShapeDtypeStruct((b, t, d), F32),
        grid=(b, t // tm),
        in_specs=in_specs,
        out_specs=pl.BlockSpec((1, tm, d), tok),
        scratch_shapes=scratch,
        compiler_params=_params(2),
        name="post_mlp",
    )(*args)


def _rope_tables(t):
    rows = t // GRID_W
    row = jnp.repeat(jnp.arange(rows, dtype=F32), GRID_W)
    col = jnp.tile(jnp.arange(GRID_W, dtype=F32), rows)
    half = HEAD_DIM // 2
    inv = 1.0 / jnp.power(ROPE_THETA, jnp.arange(0, half, 2, dtype=F32) / half)
    ang = jnp.concatenate([row[:, None] * inv, col[:, None] * inv], axis=-1)
    cos, sin = jnp.cos(ang), jnp.sin(ang)
    reps = LANES // HEAD_DIM
    return (jnp.tile(jnp.concatenate([cos, cos], axis=1), (1, reps)),
            jnp.tile(jnp.concatenate([-sin, sin], axis=1), (1, reps)))


def _split_pairs(a):
    lead = a.shape[:-1]
    a = a.reshape(lead + (-1, HEAD_DIM // 2, 2))
    return jnp.swapaxes(a, -1, -2).reshape(lead + (-1,))


def _prep_qkv(w_qkv, q_g, k_g, n_q, n_k):
    hq, hk = n_q // HEAD_DIM, n_k // HEAD_DIM
    w = _split_pairs(w_qkv[:, :n_q + n_k]).astype(BF16)
    wvt = w_qkv[:, n_q + n_k:].T.astype(BF16)
    scale = LOG2E / math.sqrt(HEAD_DIM)
    head_gain = jnp.concatenate([jnp.tile(_split_pairs(q_g) * scale, hq), jnp.tile(_split_pairs(k_g), hk)])
    return w, wvt, head_gain.reshape(1, n_q + n_k).astype(F32)


def _ones_block_diag():
    idx = jnp.arange(MXU_DIM) // HEAD_DIM
    return (idx[:, None] == idx[None, :]).astype(BF16)


def kernel(x, c, ctx, c_ctx, norm1_g, norm2_g, mod_w, mod_b, mlp_up, mlp_down,
           gqa_w_qkv, gqa_q_g, gqa_k_g, gqa_w_o,
           conv_w_pw1, conv_b_pw1, conv_w_dw, conv_b_dw, conv_ln_g, conv_ln_b, conv_w_pw2, conv_b_pw2,
           diff_w_qkv, diff_q_g, diff_k_g, diff_lam_q1, diff_lam_k1, diff_lam_q2, diff_lam_k2,
           diff_subln_g, diff_w_o,
           swa_w_qkv, swa_q_g, swa_k_g, swa_sink, swa_w_o):
    b, t, d = x.shape
    n_ctx = ctx.shape[1]
    depth = norm1_g.shape[0]
    assert depth == N_MIXERS and t % GRID_W == 0 and t % n_ctx == 0

    tm = _pick(t, (512, 256, 128))
    tq = _pick(t, (256, 128))
    tq_diff = _pick(t, (512, 256, 128))
    t_all = n_ctx + t
    tk = _pick(t_all, (768, 512, 256, 128))
    tkc = _pick(n_ctx, (768, 512, 256, 128))

    cv = jnp.concatenate([c, c_ctx[None], jnp.zeros((8 - b - 1, d), F32)], axis=0)
    mod_all = _mod_call(cv, mod_w, mod_b).reshape(depth, 8, 6, d)
    rope = _rope_tables(t)
    ones_bd = _ones_block_diag()
    row = lambda v: v.reshape(1, -1).astype(F32)

    xc = ctx
    for i in range(depth):
        m, j = i % N_MIXERS, i // N_MIXERS
        need_ctx = i < depth - 1
        mod_l = mod_all[i, :b]
        mod_c = jnp.broadcast_to(mod_all[i, b], (b, 6, d))
        g1, g2 = row(norm1_g[i]), row(norm2_g[i])
        w_up, w_down = mlp_up[i].astype(BF16), mlp_down[i].astype(BF16)
        conv_params = None
        b_o = None
        if m == 1:
            w1, b1 = conv_w_pw1[j].astype(BF16), row(conv_b_pw1[j])
            mixed = _glu_call(x, mod_l, g1, w1, b1, tm=tm)
            mixed_c = _glu_call(xc, mod_c, g1, w1, b1, tm=n_ctx) if need_ctx else None
            conv_params = (conv_w_dw[j].astype(F32), row(conv_b_dw[j]), row(conv_ln_g[j]), row(conv_ln_b[j]))
            w_o, b_o = conv_w_pw2[j].astype(BF16), row(conv_b_pw2[j])
        else:
            if m == 2:
                w_qkv, q_g, k_g, w_o = diff_w_qkv[j], diff_q_g[j], diff_k_g[j], diff_w_o[j]
                n_q = n_k = w_qkv.shape[1] // 3
                q_half = tuple(h % 2 for h in range(n_q // HEAD_DIM))
            else:
                w_qkv, q_g, k_g, w_o = ((gqa_w_qkv[j], gqa_q_g[j], gqa_k_g[j], gqa_w_o[j]) if m == 0 else
                                        (swa_w_qkv[j], swa_q_g[j], swa_k_g[j], swa_w_o[j]))
                n_q = w_o.shape[0]
                n_k = (w_qkv.shape[1] - n_q) // 2
                group = n_q // n_k
                q_half = tuple((h // group) % 2 for h in range(n_q // HEAD_DIM))
            w_o = w_o.astype(BF16)
            w_p, wvt, head_gain = _prep_qkv(w_qkv, q_g, k_g, n_q, n_k)
            proj = functools.partial(_proj_call, gain=g1, w=w_p, wvt=wvt, head_gain=head_gain,
                                     ones_bd=ones_bd, n_q=n_q, n_k=n_k, q_half=q_half)
            q, k_all, vt_all = proj(x, mod_l, rope=rope, tm=tm, t_keys=t_all)
            qc, k_all, vt_all = proj(xc, mod_c, rope=None, tm=n_ctx, t_keys=t_all, kv_buffers=(k_all, vt_all))
            lat_keys = dict(tk=tk, key_rows=t_all, key_block=0)
            ctx_keys = dict(tk=tkc, key_rows=n_ctx, key_block=t // n_ctx)
            mixed_c = None
            if m == 0:
                mixed = _gqa_call(q, k_all, vt_all, group=group, tq=tq, **lat_keys)
                if need_ctx:
                    mixed_c = _gqa_call(qc, k_all, vt_all, group=group, tq=n_ctx, **ctx_keys)
            elif m == 2:
                lam_init = 0.8 - 0.6 * math.exp(-0.3 * i)
                lam_vecs = jnp.stack([diff_lam_q1[j], diff_lam_k1[j], diff_lam_q2[j], diff_lam_k2[j]]).astype(F32)
                sg = diff_subln_g[j].reshape(-1, 1).astype(F32)
                mixed = _diff_call(q, k_all, vt_all, lam_vecs, sg, lam_init=lam_init, tq=tq_diff, **lat_keys)
                if need_ctx:
                    mixed_c = _diff_call(qc, k_all, vt_all, lam_vecs, sg, lam_init=lam_init, tq=n_ctx, **ctx_keys)
            else:
                assert not need_ctx, "windowed layer with a context update is not supported"
                sink_rows = jnp.repeat(swa_sink[j].astype(F32).reshape(n_k // HEAD_DIM, 1, group), tq, axis=2)
                mixed = _window_call(q, k_all, vt_all, sink_rows, group=group, n_ctx=n_ctx, tq=tq)
        x = _post_call(x, mixed, mod_l, w_o, b_o, g2, w_up, w_down, conv_params, tm=tm)
        if need_ctx:
            xc = _post_call(xc, mixed_c, mod_c, w_o, b_o, g2, w_up, w_down, conv_params, tm=n_ctx)
    return x
```

```python
import functools
import math

import jax
import jax.numpy as jnp
from jax import lax
from jax.experimental import pallas as pl
from jax.experimental.pallas import tpu as pltpu

F32 = jnp.float32
BF16 = jnp.bfloat16

GRID_W = 64
HEAD_DIM = 64
N_MIXERS = 4
CONV_WIDTH = 31
WINDOW = 128
ROPE_THETA = 10000.0
EPS = 1e-6
NEG_INF = -1e30
LOG2E = math.log2(math.e)

LANES = 128
MXU_DIM = 256
VMEM_LIMIT_BYTES = 56 * 1024 * 1024

CONV_HALO = 16
CONV_ROWS = 32


def _params(n_axes):
    return pltpu.CompilerParams(dimension_semantics=("parallel",) * n_axes,
                                vmem_limit_bytes=VMEM_LIMIT_BYTES)


def _resident(shape):
    nd = len(shape)
    return pl.BlockSpec(shape, lambda *_: (0,) * nd, pipeline_mode=pl.Buffered(1))


def _pick(n, candidates):
    for cand in candidates:
        if n % cand == 0:
            return cand
    return n


def _modulate(x, gain, shift, scale):
    ms = jnp.mean(x * x, axis=-1, keepdims=True)
    return (x * lax.rsqrt(ms + EPS) * gain) * (1.0 + scale) + shift


def _dot(a, b):
    return jnp.dot(a, b, preferred_element_type=F32)


def _dot_nt(a, b):
    return lax.dot_general(a, b, (((1,), (1,)), ((), ())), preferred_element_type=F32)


def _mod_kernel(cv_ref, w_ref, b_ref, o_ref):
    cv = cv_ref[...]
    s = cv * jax.nn.sigmoid(cv)
    o_ref[0] = _dot(s.astype(BF16), w_ref[0].astype(BF16)) + b_ref[0]


def _mod_call(cv, mod_w, mod_b):
    depth, d, n = mod_w.shape
    rows = cv.shape[0]
    tn = _pick(n, (1536, 1024, 512))
    return pl.pallas_call(
        _mod_kernel,
        out_shape=jax.ShapeDtypeStruct((depth, rows, n), F32),
        grid=(depth, n // tn),
        in_specs=[pl.BlockSpec((rows, d), lambda i, j: (0, 0)),
                  pl.BlockSpec((1, d, tn), lambda i, j: (i, 0, j)),
                  pl.BlockSpec((1, 1, tn), lambda i, j: (i, 0, j))],
        out_specs=pl.BlockSpec((1, rows, tn), lambda i, j: (i, 0, j)),
        compiler_params=_params(2),
        name="modulation",
    )(cv, mod_w, mod_b.reshape(depth, 1, n))


def _proj_kernel(*refs, n_q, n_k, q_half, use_rope, n_alias):
    q_ref, k_ref, vt_ref = refs[-3:]
    refs = refs[:len(refs) - 3 - n_alias]
    if use_rope:
        (x_ref, mod_ref, g_ref, w_ref, wvt_ref, hg_ref, e_ref, cos_ref, sin_ref) = refs
    else:
        (x_ref, mod_ref, g_ref, w_ref, wvt_ref, hg_ref, e_ref) = refs
    x = x_ref[0]
    mod = mod_ref[0]
    tm = x.shape[0]
    h = _modulate(x, g_ref[...], mod[0:1], mod[1:2]).astype(BF16)
    qk = _dot(h, w_ref[...])
    vt_ref[0] = _dot_nt(wvt_ref[...], h).astype(BF16)
    n_qk = n_q + n_k
    sq = qk * qk
    hi = sq.astype(BF16)
    lo = (sq - hi.astype(F32)).astype(BF16)
    ones_bd = e_ref[...]
    ss = jnp.concatenate(
        [_dot(hi[:, t:t + MXU_DIM], ones_bd) + _dot(lo[:, t:t + MXU_DIM], ones_bd)
         for t in range(0, n_qk, MXU_DIM)], axis=1)
    qk = qk * lax.rsqrt(ss * (1.0 / HEAD_DIM) + EPS) * hg_ref[...]

    lane = lax.broadcasted_iota(jnp.int32, (tm, LANES), 1)
    tiles = [qk[:, t:t + LANES] for t in range(0, n_qk, LANES)]
    if use_rope:
        first = (lane % HEAD_DIM) < (HEAD_DIM // 2)
        cos = cos_ref[...]
        sin = sin_ref[...]
        tiles = [t * cos + jnp.where(first, pltpu.roll(t, LANES - HEAD_DIM // 2, axis=1),
                                     pltpu.roll(t, HEAD_DIM // 2, axis=1)) * sin
                 for t in tiles]
    low = lane < HEAD_DIM
    for j in range(n_q // HEAD_DIM):
        t = tiles[j // 2]
        if j % 2 != q_half[j]:
            t = pltpu.roll(t, HEAD_DIM, axis=1)
        keep = low if q_half[j] == 0 else jnp.logical_not(low)
        q_ref[0, j] = jnp.where(keep, t, 0.0).astype(BF16)
    k_ref[0] = jnp.concatenate(tiles[n_q // LANES:], axis=1).astype(BF16)


def _proj_call(x, mod, gain, w, wvt, head_gain, ones_bd, rope, *, n_q, n_k, q_half, tm, t_keys,
               kv_buffers=None):
    b, t, d = x.shape
    n_v = wvt.shape[0]
    hq = n_q // HEAD_DIM
    use_rope = rope is not None
    in_specs = [pl.BlockSpec((1, tm, d), lambda bi, j: (bi, j, 0)),
                pl.BlockSpec((1, 6, d), lambda bi, j: (bi, 0, 0)),
                _resident((1, d)),
                _resident((d, n_q + n_k)),
                _resident((n_v, d)),
                _resident((1, n_q + n_k)),
                _resident((MXU_DIM, MXU_DIM))]
    args = [x, mod, gain, w, wvt, head_gain, ones_bd]
    if use_rope:
        in_specs += [pl.BlockSpec((tm, LANES), lambda bi, j: (j, 0)),
                     pl.BlockSpec((tm, LANES), lambda bi, j: (j, 0))]
        args += list(rope)
    aliases = {}
    off = 0
    if kv_buffers is not None:
        off = (t_keys - t) // tm
        assert off * tm + t == t_keys
        aliases = {len(args): 1, len(args) + 1: 2}
        in_specs += [pl.BlockSpec(memory_space=pl.ANY), pl.BlockSpec(memory_space=pl.ANY)]
        args += list(kv_buffers)
    return pl.pallas_call(
        functools.partial(_proj_kernel, n_q=n_q, n_k=n_k, q_half=q_half, use_rope=use_rope,
                          n_alias=len(aliases)),
        out_shape=(jax.ShapeDtypeStruct((b, hq, t, LANES), BF16),
                   jax.ShapeDtypeStruct((b, t_keys, n_k), BF16),
                   jax.ShapeDtypeStruct((b, n_v, t_keys), BF16)),
        grid=(b, t // tm),
        in_specs=in_specs,
        out_specs=(pl.BlockSpec((1, hq, tm, LANES), lambda bi, j: (bi, 0, j, 0)),
                   pl.BlockSpec((1, tm, n_k), lambda bi, j: (bi, j + off, 0)),
                   pl.BlockSpec((1, n_v, tm), lambda bi, j: (bi, 0, j + off))),
        input_output_aliases=aliases,
        compiler_params=_params(2),
        name="attn_proj",
    )(*args)


ONES_ROWS = 16
FLASH_UNROLL = 4


def _flash_loop(load_q, k_ref, load_vt, s_ref, cm_ref, m_ref, acc_ref, *, n_kv, tk, q_is_t=False,
                exp_dtype=F32):
    m_ref[...] = jnp.full(m_ref.shape, NEG_INF, F32)
    acc_ref[...] = jnp.zeros(acc_ref.shape, F32)
    ones = jnp.ones((ONES_ROWS, tk), BF16)
    m_cols = s_ref.shape[2]
    col_blocks = [slice(c0, c0 + MXU_DIM) for c0 in range(0, m_cols, MXU_DIM)]

    def produce(kc, q, slot, cols):
        s = _dot(kc, q[:, cols]) if q_is_t else _dot_nt(kc, q[cols])
        s_ref[slot, :, cols] = s
        cm_ref[slot, :, cols] = jnp.max(s.reshape(tk // 8, 8, MXU_DIM), axis=0)

    def consume(vt1, slot, cols):
        m_prev = m_ref[:, cols]
        m_new = jnp.maximum(m_prev, jnp.max(cm_ref[slot, :, cols], axis=0, keepdims=True))
        alpha = jnp.exp2(m_prev - m_new)
        p = jnp.exp2((s_ref[slot, :, cols] - m_new).astype(exp_dtype)).astype(BF16)
        acc_ref[:, cols] = alpha * acc_ref[:, cols] + _dot(vt1, p)
        m_ref[:, cols] = m_new

    def step(j_prod, slot_prod, j_cons, slot_cons):
        chunk_start = lambda j: j * tk if isinstance(j, int) else pl.multiple_of(j * tk, tk)
        if j_prod is not None:
            kc = k_ref[0, pl.ds(chunk_start(j_prod), tk), :]
            q = load_q()
        if j_cons is not None:
            vt1 = jnp.concatenate([load_vt(chunk_start(j_cons), tk), ones], axis=0)
        for cols in col_blocks:
            if j_prod is not None:
                produce(kc, q, slot_prod, cols)
            if j_cons is not None:
                consume(vt1, slot_cons, cols)

    step(0, 0, None, None)
    trips = (n_kv - 1) // FLASH_UNROLL if n_kv - 1 > FLASH_UNROLL else 0

    def body(trip, carry):
        j0 = trip * FLASH_UNROLL
        for u in range(FLASH_UNROLL):
            step(j0 + u + 1, (u + 1) % 2, j0 + u, u % 2)
        return carry

    if trips:
        lax.fori_loop(0, trips, body, 0)
    for j in range(trips * FLASH_UNROLL, n_kv - 1):
        step(j + 1, (j + 1) % 2, j, j % 2)
    step(None, None, n_kv - 1, (n_kv - 1) % 2)


def _store_heads_t(o_ref, ot, tq, group):
    for gg in range(group // 2):
        slab = jnp.concatenate([ot[:, (2 * gg) * tq:(2 * gg + 1) * tq],
                                ot[:, (2 * gg + 1) * tq:(2 * gg + 2) * tq]], axis=0)
        o_ref[0, :, gg * LANES:(gg + 1) * LANES] = slab.T.astype(o_ref.dtype)


def _kv_specs(key_rows, key_block, v_rows, kv_per_tile):
    return [pl.BlockSpec((1, key_rows, LANES), lambda bi, h, i: (bi, key_block, h // kv_per_tile)),
            pl.BlockSpec((1, v_rows, key_rows), lambda bi, h, i: (bi, h, key_block))]


def _flash_scratch(m_cols, tk, v_rows):
    return [pltpu.VMEM((2, tk, m_cols), F32),
            pltpu.VMEM((2, 8, m_cols), F32),
            pltpu.VMEM((1, m_cols), F32),
            pltpu.VMEM((v_rows + ONES_ROWS, m_cols), F32)]


def _gqa_kernel(q_ref, k_ref, vt_ref, o_ref, s_ref, cm_ref, m_ref, acc_ref, qt_ref, *, n_kv, tk):
    group, tq = q_ref.shape[1:3]
    qt_ref[...] = q_ref[0].reshape(group * tq, LANES).astype(F32).T.astype(BF16)
    load_q = lambda: qt_ref[...]
    load_vt = lambda start, size: vt_ref[0, :, pl.ds(start, size)]
    _flash_loop(load_q, k_ref, load_vt, s_ref, cm_ref, m_ref, acc_ref, n_kv=n_kv, tk=tk, q_is_t=True)
    ot = acc_ref[0:HEAD_DIM, :] / acc_ref[HEAD_DIM:HEAD_DIM + 1, :]
    _store_heads_t(o_ref, ot, tq, group)


def _gqa_call(q, k_all, vt_all, *, group, tq, tk, key_rows, key_block):
    b, hq, t, _ = q.shape
    return pl.pallas_call(
        functools.partial(_gqa_kernel, n_kv=key_rows // tk, tk=tk),
        out_shape=jax.ShapeDtypeStruct((b, t, hq * HEAD_DIM), BF16),
        grid=(b, hq // group, t // tq),
        in_specs=[pl.BlockSpec((1, group, tq, LANES), lambda bi, h, i: (bi, h, i, 0))]
                 + _kv_specs(key_rows, key_block, HEAD_DIM, LANES // HEAD_DIM),
        out_specs=pl.BlockSpec((1, tq, group * HEAD_DIM), lambda bi, h, i: (bi, i, h)),
        scratch_shapes=_flash_scratch(group * tq, tk, HEAD_DIM) + [pltpu.VMEM((LANES, group * tq), BF16)],
        compiler_params=_params(3),
        name="gqa_attention",
    )(q, k_all, vt_all)


def _diff_kernel(q_ref, k_ref, vt_ref, lam_ref, sg_ref, o_ref, s_ref, cm_ref, m_ref, acc_ref, *,
                 lam_init, n_kv, tk):
    tq = q_ref.shape[2]
    dv = vt_ref.shape[1]
    load_q = lambda: q_ref[0].reshape(2 * tq, LANES)
    load_vt = lambda start, size: vt_ref[0, :, pl.ds(start, size)]
    _flash_loop(load_q, k_ref, load_vt, s_ref, cm_ref, m_ref, acc_ref, n_kv=n_kv, tk=tk, exp_dtype=BF16)
    ot = acc_ref[0:dv, :] / acc_ref[dv:dv + 1, :]
    lv = lam_ref[...]
    lam = (jnp.exp(jnp.sum(lv[0:1] * lv[1:2], axis=1, keepdims=True))
           - jnp.exp(jnp.sum(lv[2:3] * lv[3:4], axis=1, keepdims=True)) + lam_init)
    ot = ot[:, :tq] - lam * ot[:, tq:]
    ms = jnp.mean(ot * ot, axis=0, keepdims=True)
    ot = (ot * lax.rsqrt(ms + EPS) * sg_ref[...]) * (1.0 - lam_init)
    o_ref[0] = ot.T.astype(o_ref.dtype)


def _diff_call(q, k_all, vt_all, lam_vecs, subln_g, *, lam_init, tq, tk, key_rows, key_block):
    b, hq, t, _ = q.shape
    heads = k_all.shape[2] // LANES
    dv = vt_all.shape[1] // heads
    return pl.pallas_call(
        functools.partial(_diff_kernel, lam_init=lam_init, n_kv=key_rows // tk, tk=tk),
        out_shape=jax.ShapeDtypeStruct((b, t, heads * dv), BF16),
        grid=(b, heads, t // tq),
        in_specs=[pl.BlockSpec((1, 2, tq, LANES), lambda bi, h, i: (bi, h, i, 0))]
                 + _kv_specs(key_rows, key_block, dv, 1)
                 + [pl.BlockSpec((4, HEAD_DIM), lambda bi, h, i: (0, 0)),
                    pl.BlockSpec((dv, 1), lambda bi, h, i: (0, 0))],
        out_specs=pl.BlockSpec((1, tq, dv), lambda bi, h, i: (bi, i, h)),
        scratch_shapes=_flash_scratch(2 * tq, tk, dv),
        compiler_params=_params(3),
        name="diff_attention",
    )(q, k_all, vt_all, lam_vecs, subln_g)


def _window_kernel(q_ref, k_ref, vt_ref, sink_ref, o_ref, *, n_ctx, t_lat):
    group, tq = q_ref.shape[1:3]
    span = tq + 2 * WINDOW
    i = pl.program_id(2)
    start = pl.multiple_of(jnp.clip(i * tq - WINDOW, 0, t_lat - span), WINDOW)
    k_lat = k_ref[0, pl.ds(start, span), :]
    k_ctx = k_ref[0, t_lat:t_lat + n_ctx, :]
    kpos = start + lax.broadcasted_iota(jnp.int32, (span, tq), 0)
    qpos = i * tq + lax.broadcasted_iota(jnp.int32, (span, tq), 1)
    in_band = jnp.abs(kpos - qpos) <= WINDOW
    in_band = jnp.concatenate([in_band] * group, axis=1)
    q_all = q_ref[0].reshape(group * tq, LANES)
    s_lat = jnp.where(in_band, _dot_nt(k_lat, q_all), NEG_INF)
    s_ctx = _dot_nt(k_ctx, q_all)
    sink = sink_ref[0] * LOG2E
    m = jnp.maximum(jnp.maximum(jnp.max(s_lat, axis=0, keepdims=True),
                                jnp.max(s_ctx, axis=0, keepdims=True)), sink)
    e_lat = jnp.exp2(s_lat - m).astype(BF16)
    e_ctx = jnp.exp2(s_ctx - m).astype(BF16)
    vt_lat = jnp.concatenate([vt_ref[0, :, pl.ds(start, span)], jnp.ones((ONES_ROWS, span), BF16)], axis=0)
    vt_ctx = jnp.concatenate([vt_ref[0, :, t_lat:t_lat + n_ctx], jnp.ones((ONES_ROWS, n_ctx), BF16)], axis=0)
    acc = _dot(vt_lat, e_lat) + _dot(vt_ctx, e_ctx)
    den = acc[HEAD_DIM:HEAD_DIM + 1, :] + jnp.exp2(sink - m)
    _store_heads_t(o_ref, acc[0:HEAD_DIM, :] / den, tq, group)


def _window_call(q, k_all, vt_all, sink_rows, *, group, n_ctx, tq):
    b, hq, t, _ = q.shape
    t_all = k_all.shape[1]
    return pl.pallas_call(
        functools.partial(_window_kernel, n_ctx=n_ctx, t_lat=t),
        out_shape=jax.ShapeDtypeStruct((b, t, hq * HEAD_DIM), BF16),
        grid=(b, hq // group, t // tq),
        in_specs=[pl.BlockSpec((1, group, tq, LANES), lambda bi, h, i: (bi, h, i, 0))]
                 + _kv_specs(t_all, 0, HEAD_DIM, LANES // HEAD_DIM)
                 + [pl.BlockSpec((1, 1, group * tq), lambda bi, h, i: (h, 0, 0))],
        out_specs=pl.BlockSpec((1, tq, group * HEAD_DIM), lambda bi, h, i: (bi, i, h)),
        compiler_params=_params(3),
        name="window_attention",
    )(q, k_all, vt_all, sink_rows)


def _glu_kernel(x_ref, mod_ref, g_ref, w_ref, b_ref, u_ref):
    x = x_ref[0]
    mod = mod_ref[0]
    d = x.shape[1]
    h = _modulate(x, g_ref[...], mod[0:1], mod[1:2]).astype(BF16)
    ag = _dot(h, w_ref[...]) + b_ref[...]
    u_ref[0] = (ag[:, :d] * jax.nn.sigmoid(ag[:, d:])).astype(u_ref.dtype)


def _glu_call(x, mod, gain, w, bias, *, tm):
    b, t, d = x.shape
    return pl.pallas_call(
        _glu_kernel,
        out_shape=jax.ShapeDtypeStruct((b, t, d), BF16),
        grid=(b, t // tm),
        in_specs=[pl.BlockSpec((1, tm, d), lambda bi, j: (bi, j, 0)),
                  pl.BlockSpec((1, 6, d), lambda bi, j: (bi, 0, 0)),
                  _resident((1, d)),
                  _resident((d, 2 * d)),
                  _resident((1, 2 * d))],
        out_specs=pl.BlockSpec((1, tm, d), lambda bi, j: (bi, j, 0)),
        compiler_params=_params(2),
        name="conv_glu",
    )(x, mod, gain, w, bias)


def _depthwise_conv(ubuf_ref, cbuf_ref, wdw_ref, bdw_ref, tm, d):
    win_rows = CONV_ROWS + 2 * CONV_HALO
    base = CONV_HALO - CONV_WIDTH // 2
    lane_chunk = 2 * LANES

    def step(r, carry):
        r0 = pl.multiple_of(r * CONV_ROWS, CONV_ROWS)
        for c0 in range(0, d, lane_chunk):
            win = ubuf_ref[pl.ds(r0, win_rows), c0:c0 + lane_chunk]
            acc = jnp.zeros((CONV_ROWS, lane_chunk), F32) + bdw_ref[:, c0:c0 + lane_chunk]
            for sub in range(8):
                shifted = win if sub == 0 else pltpu.roll(win, win_rows - sub, axis=0)
                for al in range(0, win_rows - CONV_ROWS + 1, 8):
                    k = al + sub - base
                    if 0 <= k < CONV_WIDTH:
                        acc = acc + wdw_ref[k:k + 1, c0:c0 + lane_chunk] * shifted[al:al + CONV_ROWS]
            cbuf_ref[pl.ds(r0, CONV_ROWS), c0:c0 + lane_chunk] = acc
        return carry

    lax.fori_loop(0, tm // CONV_ROWS, step, 0)


def _post_kernel(*refs, conv, has_bias, d_ff_chunk):
    it = iter(refs)
    x_ref = next(it)
    if conv:
        up_ref, uc_ref, un_ref = next(it), next(it), next(it)
        wdw_ref, bdw_ref, lng_ref, lnb_ref = next(it), next(it), next(it), next(it)
    else:
        o_ref = next(it)
    mod_ref, wo_ref = next(it), next(it)
    bo_ref = next(it) if has_bias else None
    g2_ref, wup_ref, wdn_ref, out_ref = next(it), next(it), next(it), next(it)
    if conv:
        ubuf_ref, cbuf_ref = next(it), next(it)

    x = x_ref[0]
    mod = mod_ref[0]
    tm, d = x.shape
    if conv:
        j = pl.program_id(1)
        prev = jnp.where(j > 0, up_ref[0].astype(F32), 0.0)
        nxt = jnp.where(j < pl.num_programs(1) - 1, un_ref[0].astype(F32), 0.0)
        ubuf_ref[0:CONV_HALO] = prev
        ubuf_ref[CONV_HALO:CONV_HALO + tm] = uc_ref[0].astype(F32)
        ubuf_ref[CONV_HALO + tm:] = nxt
        _depthwise_conv(ubuf_ref, cbuf_ref, wdw_ref, bdw_ref, tm, d)
        cv = cbuf_ref[...]
        mu = jnp.mean(cv, axis=-1, keepdims=True)
        cc = cv - mu
        var = jnp.mean(cc * cc, axis=-1, keepdims=True)
        yn = cc * lax.rsqrt(var + EPS) * lng_ref[...] + lnb_ref[...]
        o = (yn * jax.nn.sigmoid(yn)).astype(BF16)
    else:
        o = o_ref[0]
    y = _dot(o, wo_ref[...])
    if has_bias:
        y = y + bo_ref[...]
    x1 = x + mod[2:3] * y
    h2 = _modulate(x1, g2_ref[...], mod[3:4], mod[4:5]).astype(BF16)
    acc = jnp.zeros((tm, d), F32)
    d_ff = wup_ref.shape[1]
    for c0 in range(0, d_ff, d_ff_chunk):
        up = jnp.maximum(_dot(h2, wup_ref[:, c0:c0 + d_ff_chunk]), 0.0)
        acc = acc + _dot((up * up).astype(BF16), wdn_ref[c0:c0 + d_ff_chunk, :])
    out_ref[0] = x1 + mod[5:6] * acc


def _post_call(x, mixed, mod, w_o, b_o, gain2, w_up, w_down, conv_params, *, tm):
    b, t, d = x.shape
    d_in = w_o.shape[0]
    d_ff = w_up.shape[1]
    conv = conv_params is not None
    has_bias = b_o is not None
    tok = lambda bi, j: (bi, j, 0)
    in_specs = [pl.BlockSpec((1, tm, d), tok)]
    args = [x]
    scratch = []
    if conv:
        hb = tm // CONV_HALO
        last = t // CONV_HALO - 1
        in_specs += [pl.BlockSpec((1, CONV_HALO, d), lambda bi, j: (bi, jnp.maximum(j * hb - 1, 0), 0)),
                     pl.BlockSpec((1, tm, d), tok),
                     pl.BlockSpec((1, CONV_HALO, d), lambda bi, j: (bi, jnp.minimum((j + 1) * hb, last), 0)),
                     _resident((CONV_WIDTH, d)), _resident((1, d)), _resident((1, d)), _resident((1, d))]
        args += [mixed, mixed, mixed] + list(conv_params)
        scratch = [pltpu.VMEM((tm + 2 * CONV_HALO, d), F32), pltpu.VMEM((tm, d), F32)]
    else:
        in_specs += [pl.BlockSpec((1, tm, d_in), tok)]
        args += [mixed]
    in_specs += [pl.BlockSpec((1, 6, d), lambda bi, j: (bi, 0, 0)), _resident((d_in, d))]
    args += [mod, w_o]
    if has_bias:
        in_specs += [_resident((1, d))]
        args += [b_o]
    in_specs += [_resident((1, d)), _resident((d, d_ff)), _resident((d_ff, d))]
    args += [gain2, w_up, w_down]
    return pl.pallas_call(
        functools.partial(_post_kernel, conv=conv, has_bias=has_bias, d_ff_chunk=min(d_ff, 1024)),
        out_shape=jax.ShapeDtypeStruct((b, t, d), F32),
        grid=(b, t // tm),
        in_specs=in_specs,
        out_specs=pl.BlockSpec((1, tm, d), tok),
        scratch_shapes=scratch,
        compiler_params=_params(2),
        name="post_mlp",
    )(*args)


def _rope_tables(t):
    rows = t // GRID_W
    row = jnp.repeat(jnp.arange(rows, dtype=F32), GRID_W)
    col = jnp.tile(jnp.arange(GRID_W, dtype=F32), rows)
    half = HEAD_DIM // 2
    inv = 1.0 / jnp.power(ROPE_THETA, jnp.arange(0, half, 2, dtype=F32) / half)
    ang = jnp.concatenate([row[:, None] * inv, col[:, None] * inv], axis=-1)
    cos, sin = jnp.cos(ang), jnp.sin(ang)
    reps = LANES // HEAD_DIM
    return (jnp.tile(jnp.concatenate([cos, cos], axis=1), (1, reps)),
            jnp.tile(jnp.concatenate([-sin, sin], axis=1), (1, reps)))


def _split_pairs(a):
    lead = a.shape[:-1]
    a = a.reshape(lead + (-1, HEAD_DIM // 2, 2))
    return jnp.swapaxes(a, -1, -2).reshape(lead + (-1,))


def _prep_qkv(w_qkv, q_g, k_g, n_q, n_k):
    hq, hk = n_q // HEAD_DIM, n_k // HEAD_DIM
    w = _split_pairs(w_qkv[:, :n_q + n_k]).astype(BF16)
    wvt = w_qkv[:, n_q + n_k:].T.astype(BF16)
    scale = LOG2E / math.sqrt(HEAD_DIM)
    head_gain = jnp.concatenate([jnp.tile(_split_pairs(q_g) * scale, hq), jnp.tile(_split_pairs(k_g), hk)])
    return w, wvt, head_gain.reshape(1, n_q + n_k).astype(F32)


def _ones_block_diag():
    idx = jnp.arange(MXU_DIM) // HEAD_DIM
    return (idx[:, None] == idx[None, :]).astype(BF16)


def kernel(x, c, ctx, c_ctx, norm1_g, norm2_g, mod_w, mod_b, mlp_up, mlp_down,
           gqa_w_qkv, gqa_q_g, gqa_k_g, gqa_w_o,
           conv_w_pw1, conv_b_pw1, conv_w_dw, conv_b_dw, conv_ln_g, conv_ln_b, conv_w_pw2, conv_b_pw2,
           diff_w_qkv, diff_q_g, diff_k_g, diff_lam_q1, diff_lam_k1, diff_lam_q2, diff_lam_k2,
           diff_subln_g, diff_w_o,
           swa_w_qkv, swa_q_g, swa_k_g, swa_sink, swa_w_o):
    b, t, d = x.shape
    n_ctx = ctx.shape[1]
    depth = norm1_g.shape[0]
    assert depth == N_MIXERS and t % GRID_W == 0 and t % n_ctx == 0

    tm = _pick(t, (512, 256, 128))
    tq = _pick(t, (256, 128))
    tq_diff = _pick(t, (512, 256, 128))
    t_all = n_ctx + t
    tk = _pick(t_all, (768, 512, 256, 128))
    tkc = _pick(n_ctx, (768, 512, 256, 128))

    cv = jnp.concatenate([c, c_ctx[None], jnp.zeros((8 - b - 1, d), F32)], axis=0)
    mod_all = _mod_call(cv, mod_w, mod_b).reshape(depth, 8, 6, d)
    rope = _rope_tables(t)
    ones_bd = _ones_block_diag()
    row = lambda v: v.reshape(1, -1).astype(F32)

    xc = ctx
    for i in range(depth):
        m, j = i % N_MIXERS, i // N_MIXERS
        need_ctx = i < depth - 1
        mod_l = mod_all[i, :b]
        mod_c = jnp.broadcast_to(mod_all[i, b], (b, 6, d))
        g1, g2 = row(norm1_g[i]), row(norm2_g[i])
        w_up, w_down = mlp_up[i].astype(BF16), mlp_down[i].astype(BF16)
        conv_params = None
        b_o = None
        if m == 1:
            w1, b1 = conv_w_pw1[j].astype(BF16), row(conv_b_pw1[j])
            mixed = _glu_call(x, mod_l, g1, w1, b1, tm=tm)
            mixed_c = _glu_call(xc, mod_c, g1, w1, b1, tm=n_ctx) if need_ctx else None
            conv_params = (conv_w_dw[j].astype(F32), row(conv_b_dw[j]), row(conv_ln_g[j]), row(conv_ln_b[j]))
            w_o, b_o = conv_w_pw2[j].astype(BF16), row(conv_b_pw2[j])
        else:
            if m == 2:
                w_qkv, q_g, k_g, w_o = diff_w_qkv[j], diff_q_g[j], diff_k_g[j], diff_w_o[j]
                n_q = n_k = w_qkv.shape[1] // 3
                q_half = tuple(h % 2 for h in range(n_q // HEAD_DIM))
            else:
                w_qkv, q_g, k_g, w_o = ((gqa_w_qkv[j], gqa_q_g[j], gqa_k_g[j], gqa_w_o[j]) if m == 0 else
                                        (swa_w_qkv[j], swa_q_g[j], swa_k_g[j], swa_w_o[j]))
                n_q = w_o.shape[0]
                n_k = (w_qkv.shape[1] - n_q) // 2
                group = n_q // n_k
                q_half = tuple((h // group) % 2 for h in range(n_q // HEAD_DIM))
            w_o = w_o.astype(BF16)
            w_p, wvt, head_gain = _prep_qkv(w_qkv, q_g, k_g, n_q, n_k)
            proj = functools.partial(_proj_call, gain=g1, w=w_p, wvt=wvt, head_gain=head_gain,
                                     ones_bd=ones_bd, n_q=n_q, n_k=n_k, q_half=q_half)
            q, k_all, vt_all = proj(x, mod_l, rope=rope, tm=tm, t_keys=t_all)
            qc, k_all, vt_all = proj(xc, mod_c, rope=None, tm=n_ctx, t_keys=t_all, kv_buffers=(k_all, vt_all))
            lat_keys = dict(tk=tk, key_rows=t_all, key_block=0)
            ctx_keys = dict(tk=tkc, key_rows=n_ctx, key_block=t // n_ctx)
            mixed_c = None
            if m == 0:
                mixed = _gqa_call(q, k_all, vt_all, group=group, tq=tq, **lat_keys)
                if need_ctx:
                    mixed_c = _gqa_call(qc, k_all, vt_all, group=group, tq=n_ctx, **ctx_keys)
            elif m == 2:
                lam_init = 0.8 - 0.6 * math.exp(-0.3 * i)
                lam_vecs = jnp.stack([diff_lam_q1[j], diff_lam_k1[j], diff_lam_q2[j], diff_lam_k2[j]]).astype(F32)
                sg = diff_subln_g[j].reshape(-1, 1).astype(F32)
                mixed = _diff_call(q, k_all, vt_all, lam_vecs, sg, lam_init=lam_init, tq=tq_diff, **lat_keys)
                if need_ctx:
                    mixed_c = _diff_call(qc, k_all, vt_all, lam_vecs, sg, lam_init=lam_init, tq=n_ctx, **ctx_keys)
            else:
                assert not need_ctx, "windowed layer with a context update is not supported"
                sink_rows = jnp.repeat(swa_sink[j].astype(F32).reshape(n_k // HEAD_DIM, 1, group), tq, axis=2)
                mixed = _window_call(q, k_all, vt_all, sink_rows, group=group, n_ctx=n_ctx, tq=tq)
        x = _post_call(x, mixed, mod_l, w_o, b_o, g2, w_up, w_down, conv_params, tm=tm)
        if need_ctx:
            xc = _post_call(xc, mixed_c, mod_c, w_o, b_o, g2, w_up, w_down, conv_params, tm=n_ctx)
    return x
```

```python
import functools
import math

import jax
import jax.numpy as jnp
from jax import lax
from jax.experimental import pallas as pl
from jax.experimental.pallas import tpu as pltpu

F32 = jnp.float32
BF16 = jnp.bfloat16

GRID_W = 64
HEAD_DIM = 64
N_MIXERS = 4
CONV_WIDTH = 31
WINDOW = 128
ROPE_THETA = 10000.0
EPS = 1e-6
NEG_INF = -1e30
LOG2E = math.log2(math.e)

LANES = 128
MXU_DIM = 256
VMEM_LIMIT_BYTES = 56 * 1024 * 1024

CONV_HALO = 16
CONV_ROWS = 32


def _params(n_axes):
    return pltpu.CompilerParams(dimension_semantics=("parallel",) * n_axes,
                                vmem_limit_bytes=VMEM_LIMIT_BYTES)


def _resident(shape):
    nd = len(shape)
    return pl.BlockSpec(shape, lambda *_: (0,) * nd, pipeline_mode=pl.Buffered(1))


def _pick(n, candidates):
    for cand in candidates:
        if n % cand == 0:
            return cand
    return n


def _modulate(x, gain, shift, scale):
    ms = jnp.mean(x * x, axis=-1, keepdims=True)
    return (x * lax.rsqrt(ms + EPS) * gain) * (1.0 + scale) + shift


def _dot(a, b):
    return jnp.dot(a, b, preferred_element_type=F32)


def _dot_nt(a, b):
    return lax.dot_general(a, b, (((1,), (1,)), ((), ())), preferred_element_type=F32)


def _mod_kernel(cv_ref, w_ref, b_ref, o_ref):
    cv = cv_ref[...]
    s = cv * jax.nn.sigmoid(cv)
    o_ref[0] = _dot(s.astype(BF16), w_ref[0].astype(BF16)) + b_ref[0]


def _mod_call(cv, mod_w, mod_b):
    depth, d, n = mod_w.shape
    rows = cv.shape[0]
    tn = _pick(n, (1536, 1024, 512))
    return pl.pallas_call(
        _mod_kernel,
        out_shape=jax.ShapeDtypeStruct((depth, rows, n), F32),
        grid=(depth, n // tn),
        in_specs=[pl.BlockSpec((rows, d), lambda i, j: (0, 0)),
                  pl.BlockSpec((1, d, tn), lambda i, j: (i, 0, j)),
                  pl.BlockSpec((1, 1, tn), lambda i, j: (i, 0, j))],
        out_specs=pl.BlockSpec((1, rows, tn), lambda i, j: (i, 0, j)),
        compiler_params=_params(2),
        name="modulation",
    )(cv, mod_w, mod_b.reshape(depth, 1, n))


def _proj_kernel(*refs, n_q, n_k, q_half, use_rope, n_alias):
    q_ref, k_ref, vt_ref = refs[-3:]
    refs = refs[:len(refs) - 3 - n_alias]
    if use_rope:
        (x_ref, mod_ref, g_ref, w_ref, wvt_ref, hg_ref, e_ref, cos_ref, sin_ref) = refs
    else:
        (x_ref, mod_ref, g_ref, w_ref, wvt_ref, hg_ref, e_ref) = refs
    x = x_ref[0]
    mod = mod_ref[0]
    tm = x.shape[0]
    h = _modulate(x, g_ref[...], mod[0:1], mod[1:2]).astype(BF16)
    qk = _dot(h, w_ref[...])
    vt_ref[0] = _dot_nt(wvt_ref[...], h).astype(BF16)
    n_qk = n_q + n_k
    sq = qk * qk
    hi = sq.astype(BF16)
    lo = (sq - hi.astype(F32)).astype(BF16)
    ones_bd = e_ref[...]
    ss = jnp.concatenate(
        [_dot(hi[:, t:t + MXU_DIM], ones_bd) + _dot(lo[:, t:t + MXU_DIM], ones_bd)
         for t in range(0, n_qk, MXU_DIM)], axis=1)
    qk = qk * lax.rsqrt(ss * (1.0 / HEAD_DIM) + EPS) * hg_ref[...]

    lane = lax.broadcasted_iota(jnp.int32, (tm, LANES), 1)
    tiles = [qk[:, t:t + LANES] for t in range(0, n_qk, LANES)]
    if use_rope:
        first = (lane % HEAD_DIM) < (HEAD_DIM // 2)
        cos = cos_ref[...]
        sin = sin_ref[...]
        tiles = [t * cos + jnp.where(first, pltpu.roll(t, LANES - HEAD_DIM // 2, axis=1),
                                     pltpu.roll(t, HEAD_DIM // 2, axis=1)) * sin
                 for t in tiles]
    low = lane < HEAD_DIM
    for j in range(n_q // HEAD_DIM):
        t = tiles[j // 2]
        if j % 2 != q_half[j]:
            t = pltpu.roll(t, HEAD_DIM, axis=1)
        keep = low if q_half[j] == 0 else jnp.logical_not(low)
        q_ref[0, j] = jnp.where(keep, t, 0.0).astype(BF16)
    k_ref[0] = jnp.concatenate(tiles[n_q // LANES:], axis=1).astype(BF16)


def _proj_call(x, mod, gain, w, wvt, head_gain, ones_bd, rope, *, n_q, n_k, q_half, tm, kv_buffers,
               first_row):
    b, t, d = x.shape
    n_v = wvt.shape[0]
    t_keys = kv_buffers[0].shape[1]
    hq = n_q // HEAD_DIM
    use_rope = rope is not None
    in_specs = [pl.BlockSpec((1, tm, d), lambda bi, j: (bi, j, 0)),
                pl.BlockSpec((1, 6, d), lambda bi, j: (bi, 0, 0)),
                _resident((1, d)),
                _resident((d, n_q + n_k)),
                _resident((n_v, d)),
                _resident((1, n_q + n_k)),
                _resident((MXU_DIM, MXU_DIM))]
    args = [x, mod, gain, w, wvt, head_gain, ones_bd]
    if use_rope:
        in_specs += [pl.BlockSpec((tm, LANES), lambda bi, j: (j, 0)),
                     pl.BlockSpec((tm, LANES), lambda bi, j: (j, 0))]
        args += list(rope)
    off = first_row // tm
    assert off * tm == first_row and first_row + t <= t_keys
    aliases = {len(args): 1, len(args) + 1: 2}
    in_specs += [pl.BlockSpec(memory_space=pl.ANY), pl.BlockSpec(memory_space=pl.ANY)]
    args += list(kv_buffers)
    return pl.pallas_call(
        functools.partial(_proj_kernel, n_q=n_q, n_k=n_k, q_half=q_half, use_rope=use_rope,
                          n_alias=len(aliases)),
        out_shape=(jax.ShapeDtypeStruct((b, hq, t, LANES), BF16),
                   jax.ShapeDtypeStruct((b, t_keys, n_k), BF16),
                   jax.ShapeDtypeStruct((b, n_v, t_keys), BF16)),
        grid=(b, t // tm),
        in_specs=in_specs,
        out_specs=(pl.BlockSpec((1, hq, tm, LANES), lambda bi, j: (bi, 0, j, 0)),
                   pl.BlockSpec((1, tm, n_k), lambda bi, j: (bi, j + off, 0)),
                   pl.BlockSpec((1, n_v, tm), lambda bi, j: (bi, 0, j + off))),
        input_output_aliases=aliases,
        compiler_params=_params(2),
        name="attn_proj",
    )(*args)


ONES_ROWS = 16
FLASH_UNROLL = 4


def _flash_loop(q_stacked, k_ref, load_vt, qt_ref, s_ref, cm_ref, m_ref, acc_ref, *, n_kv, tk):
    qt_ref[...] = q_stacked.astype(F32).T.astype(BF16)
    m_ref[...] = jnp.full(m_ref.shape, NEG_INF, F32)
    acc_ref[...] = jnp.zeros(acc_ref.shape, F32)
    ones = jnp.ones((ONES_ROWS, tk), BF16)
    m_cols = s_ref.shape[2]
    col_blocks = [slice(c0, c0 + MXU_DIM) for c0 in range(0, m_cols, MXU_DIM)]

    def produce(kc, slot, cols):
        s = _dot(kc, qt_ref[:, cols])
        s_ref[slot, :, cols] = s
        cm_ref[slot, :, cols] = jnp.max(s.reshape(tk // 8, 8, MXU_DIM), axis=0)

    def consume(vt1, slot, cols):
        m_prev = m_ref[:, cols]
        m_new = jnp.maximum(m_prev, jnp.max(cm_ref[slot, :, cols], axis=0, keepdims=True))
        alpha = jnp.exp2(m_prev - m_new)
        p = jnp.exp2(s_ref[slot, :, cols] - m_new).astype(BF16)
        acc_ref[:, cols] = alpha * acc_ref[:, cols] + _dot(vt1, p)
        m_ref[:, cols] = m_new

    def step(j_prod, slot_prod, j_cons, slot_cons):
        chunk_start = lambda j: j * tk if isinstance(j, int) else pl.multiple_of(j * tk, tk)
        if j_prod is not None:
            kc = k_ref[0, pl.ds(chunk_start(j_prod), tk), :]
        if j_cons is not None:
            vt1 = jnp.concatenate([load_vt(chunk_start(j_cons), tk), ones], axis=0)
        for cols in col_blocks:
            if j_prod is not None:
                produce(kc, slot_prod, cols)
            if j_cons is not None:
                consume(vt1, slot_cons, cols)

    step(0, 0, None, None)
    trips = (n_kv - 1) // FLASH_UNROLL if n_kv - 1 > FLASH_UNROLL else 0

    def body(trip, carry):
        j0 = trip * FLASH_UNROLL
        for u in range(FLASH_UNROLL):
            step(j0 + u + 1, (u + 1) % 2, j0 + u, u % 2)
        return carry

    if trips:
        lax.fori_loop(0, trips, body, 0)
    for j in range(trips * FLASH_UNROLL, n_kv - 1):
        step(j + 1, (j + 1) % 2, j, j % 2)
    step(None, None, n_kv - 1, (n_kv - 1) % 2)


def _store_heads_t(o_ref, ot, tq, group):
    for gg in range(group // 2):
        slab = jnp.concatenate([ot[:, (2 * gg) * tq:(2 * gg + 1) * tq],
                                ot[:, (2 * gg + 1) * tq:(2 * gg + 2) * tq]], axis=0)
        o_ref[0, :, gg * LANES:(gg + 1) * LANES] = slab.T.astype(o_ref.dtype)


def _kv_specs(key_rows, key_block, v_rows, kv_per_tile):
    return [pl.BlockSpec((1, key_rows, LANES), lambda bi, h, i: (bi, key_block, h // kv_per_tile)),
            pl.BlockSpec((1, v_rows, key_rows), lambda bi, h, i: (bi, h, key_block))]


def _flash_scratch(m_cols, tk, v_rows):
    return [pltpu.VMEM((LANES, m_cols), BF16),
            pltpu.VMEM((2, tk, m_cols), F32),
            pltpu.VMEM((2, 8, m_cols), F32),
            pltpu.VMEM((1, m_cols), F32),
            pltpu.VMEM((v_rows + ONES_ROWS, m_cols), F32)]


def _gqa_kernel(q_ref, k_ref, vt_ref, o_ref, qt_ref, s_ref, cm_ref, m_ref, acc_ref, *, n_kv, tk):
    group, tq = q_ref.shape[1:3]
    q_stacked = q_ref[0].reshape(group * tq, LANES)
    load_vt = lambda start, size: vt_ref[0, :, pl.ds(start, size)]
    _flash_loop(q_stacked, k_ref, load_vt, qt_ref, s_ref, cm_ref, m_ref, acc_ref, n_kv=n_kv, tk=tk)
    ot = acc_ref[0:HEAD_DIM, :] / acc_ref[HEAD_DIM:HEAD_DIM + 1, :]
    _store_heads_t(o_ref, ot, tq, group)


def _gqa_call(q, k_all, vt_all, *, group, tq, tk, key_rows, key_block):
    b, hq, t, _ = q.shape
    return pl.pallas_call(
        functools.partial(_gqa_kernel, n_kv=key_rows // tk, tk=tk),
        out_shape=jax.ShapeDtypeStruct((b, t, hq * HEAD_DIM), BF16),
        grid=(b, hq // group, t // tq),
        in_specs=[pl.BlockSpec((1, group, tq, LANES), lambda bi, h, i: (bi, h, i, 0))]
                 + _kv_specs(key_rows, key_block, HEAD_DIM, LANES // HEAD_DIM),
        out_specs=pl.BlockSpec((1, tq, group * HEAD_DIM), lambda bi, h, i: (bi, i, h)),
        scratch_shapes=_flash_scratch(group * tq, tk, HEAD_DIM),
        compiler_params=_params(3),
        name="gqa_attention",
    )(q, k_all, vt_all)


def _diff_kernel(q_ref, k_ref, vt_ref, lam_ref, sg_ref, o_ref, qt_ref, s_ref, cm_ref, m_ref, acc_ref, *,
                 lam_init, n_kv, tk):
    tq = q_ref.shape[2]
    dv = vt_ref.shape[1]
    q_stacked = q_ref[0].reshape(2 * tq, LANES)
    load_vt = lambda start, size: vt_ref[0, :, pl.ds(start, size)]
    _flash_loop(q_stacked, k_ref, load_vt, qt_ref, s_ref, cm_ref, m_ref, acc_ref, n_kv=n_kv, tk=tk)
    ot = acc_ref[0:dv, :] / acc_ref[dv:dv + 1, :]
    lv = lam_ref[...]
    lam = (jnp.exp(jnp.sum(lv[0:1] * lv[1:2], axis=1, keepdims=True))
           - jnp.exp(jnp.sum(lv[2:3] * lv[3:4], axis=1, keepdims=True)) + lam_init)
    ot = ot[:, :tq] - lam * ot[:, tq:]
    ms = jnp.mean(ot * ot, axis=0, keepdims=True)
    ot = (ot * lax.rsqrt(ms + EPS) * sg_ref[...]) * (1.0 - lam_init)
    o_ref[0] = ot.T.astype(o_ref.dtype)


def _diff_call(q, k_all, vt_all, lam_vecs, subln_g, *, lam_init, tq, tk, key_rows, key_block):
    b, hq, t, _ = q.shape
    heads = k_all.shape[2] // LANES
    dv = vt_all.shape[1] // heads
    return pl.pallas_call(
        functools.partial(_diff_kernel, lam_init=lam_init, n_kv=key_rows // tk, tk=tk),
        out_shape=jax.ShapeDtypeStruct((b, t, heads * dv), BF16),
        grid=(b, heads, t // tq),
        in_specs=[pl.BlockSpec((1, 2, tq, LANES), lambda bi, h, i: (bi, h, i, 0))]
                 + _kv_specs(key_rows, key_block, dv, 1)
                 + [pl.BlockSpec((4, HEAD_DIM), lambda bi, h, i: (0, 0)),
                    pl.BlockSpec((dv, 1), lambda bi, h, i: (0, 0))],
        out_specs=pl.BlockSpec((1, tq, dv), lambda bi, h, i: (bi, i, h)),
        scratch_shapes=_flash_scratch(2 * tq, tk, dv),
        compiler_params=_params(3),
        name="diff_attention",
    )(q, k_all, vt_all, lam_vecs, subln_g)


def _window_kernel(q_ref, k_ref, vt_ref, sink_ref, o_ref, *, n_ctx, t_lat):
    group, tq = q_ref.shape[1:3]
    span = tq + 2 * WINDOW
    i = pl.program_id(2)
    start = pl.multiple_of(jnp.clip(i * tq - WINDOW, 0, t_lat - span), WINDOW)
    k_lat = k_ref[0, pl.ds(start, span), :]
    k_ctx = k_ref[0, t_lat:t_lat + n_ctx, :]
    kpos = start + lax.broadcasted_iota(jnp.int32, (span, tq), 0)
    qpos = i * tq + lax.broadcasted_iota(jnp.int32, (span, tq), 1)
    in_band = jnp.abs(kpos - qpos) <= WINDOW
    in_band = jnp.concatenate([in_band] * group, axis=1)
    qt = q_ref[0].reshape(group * tq, LANES).astype(F32).T.astype(BF16)
    s_lat = jnp.where(in_band, _dot(k_lat, qt), NEG_INF)
    s_ctx = _dot(k_ctx, qt)
    sink = sink_ref[0] * LOG2E
    m = jnp.maximum(jnp.maximum(jnp.max(s_lat, axis=0, keepdims=True),
                                jnp.max(s_ctx, axis=0, keepdims=True)), sink)
    e_lat = jnp.exp2(s_lat - m).astype(BF16)
    e_ctx = jnp.exp2(s_ctx - m).astype(BF16)
    vt_lat = jnp.concatenate([vt_ref[0, :, pl.ds(start, span)], jnp.ones((ONES_ROWS, span), BF16)], axis=0)
    vt_ctx = jnp.concatenate([vt_ref[0, :, t_lat:t_lat + n_ctx], jnp.ones((ONES_ROWS, n_ctx), BF16)], axis=0)
    acc = _dot(vt_lat, e_lat) + _dot(vt_ctx, e_ctx)
    den = acc[HEAD_DIM:HEAD_DIM + 1, :] + jnp.exp2(sink - m)
    _store_heads_t(o_ref, acc[0:HEAD_DIM, :] / den, tq, group)


def _window_call(q, k_all, vt_all, sink_rows, *, group, n_ctx, tq):
    b, hq, t, _ = q.shape
    t_all = k_all.shape[1]
    return pl.pallas_call(
        functools.partial(_window_kernel, n_ctx=n_ctx, t_lat=t),
        out_shape=jax.ShapeDtypeStruct((b, t, hq * HEAD_DIM), BF16),
        grid=(b, hq // group, t // tq),
        in_specs=[pl.BlockSpec((1, group, tq, LANES), lambda bi, h, i: (bi, h, i, 0))]
                 + _kv_specs(t_all, 0, HEAD_DIM, LANES // HEAD_DIM)
                 + [pl.BlockSpec((1, 1, group * tq), lambda bi, h, i: (h, 0, 0))],
        out_specs=pl.BlockSpec((1, tq, group * HEAD_DIM), lambda bi, h, i: (bi, i, h)),
        compiler_params=_params(3),
        name="window_attention",
    )(q, k_all, vt_all, sink_rows)


def _glu_kernel(x_ref, mod_ref, g_ref, w_ref, b_ref, u_ref):
    x = x_ref[0]
    mod = mod_ref[0]
    d = x.shape[1]
    h = _modulate(x, g_ref[...], mod[0:1], mod[1:2]).astype(BF16)
    ag = _dot(h, w_ref[...]) + b_ref[...]
    u_ref[0] = (ag[:, :d] * jax.nn.sigmoid(ag[:, d:])).astype(u_ref.dtype)


def _glu_call(x, mod, gain, w, bias, *, tm):
    b, t, d = x.shape
    return pl.pallas_call(
        _glu_kernel,
        out_shape=jax.ShapeDtypeStruct((b, t, d), BF16),
        grid=(b, t // tm),
        in_specs=[pl.BlockSpec((1, tm, d), lambda bi, j: (bi, j, 0)),
                  pl.BlockSpec((1, 6, d), lambda bi, j: (bi, 0, 0)),
                  _resident((1, d)),
                  _resident((d, 2 * d)),
                  _resident((1, 2 * d))],
        out_specs=pl.BlockSpec((1, tm, d), lambda bi, j: (bi, j, 0)),
        compiler_params=_params(2),
        name="conv_glu",
    )(x, mod, gain, w, bias)


def _depthwise_conv(ubuf_ref, cbuf_ref, wdw_ref, bdw_ref, tm, d):
    win_rows = CONV_ROWS + 2 * CONV_HALO
    base = CONV_HALO - CONV_WIDTH // 2
    lane_chunk = 2 * LANES

    def step(r, carry):
        r0 = pl.multiple_of(r * CONV_ROWS, CONV_ROWS)
        for c0 in range(0, d, lane_chunk):
            win = ubuf_ref[pl.ds(r0, win_rows), c0:c0 + lane_chunk]
            acc = jnp.zeros((CONV_ROWS, lane_chunk), F32) + bdw_ref[:, c0:c0 + lane_chunk]
            for sub in range(8):
                shifted = win if sub == 0 else pltpu.roll(win, win_rows - sub, axis=0)
                for al in range(0, win_rows - CONV_ROWS + 1, 8):
                    k = al + sub - base
                    if 0 <= k < CONV_WIDTH:
                        acc = acc + wdw_ref[k:k + 1, c0:c0 + lane_chunk] * shifted[al:al + CONV_ROWS]
            cbuf_ref[pl.ds(r0, CONV_ROWS), c0:c0 + lane_chunk] = acc
        return carry

    lax.fori_loop(0, tm // CONV_ROWS, step, 0)


def _post_kernel(*refs, conv, has_bias, d_ff_chunk):
    it = iter(refs)
    x_ref = next(it)
    if conv:
        up_ref, uc_ref, un_ref = next(it), next(it), next(it)
        wdw_ref, bdw_ref, lng_ref, lnb_ref = next(it), next(it), next(it), next(it)
    else:
        o_ref = next(it)
    mod_ref, wo_ref = next(it), next(it)
    bo_ref = next(it) if has_bias else None
    g2_ref, wup_ref, wdn_ref, out_ref = next(it), next(it), next(it), next(it)
    if conv:
        ubuf_ref, cbuf_ref = next(it), next(it)

    x = x_ref[0]
    mod = mod_ref[0]
    tm, d = x.shape
    if conv:
        j = pl.program_id(1)
        prev = jnp.where(j > 0, up_ref[0].astype(F32), 0.0)
        nxt = jnp.where(j < pl.num_programs(1) - 1, un_ref[0].astype(F32), 0.0)
        ubuf_ref[0:CONV_HALO] = prev
        ubuf_ref[CONV_HALO:CONV_HALO + tm] = uc_ref[0].astype(F32)
        ubuf_ref[CONV_HALO + tm:] = nxt
        _depthwise_conv(ubuf_ref, cbuf_ref, wdw_ref, bdw_ref, tm, d)
        cv = cbuf_ref[...]
        mu = jnp.mean(cv, axis=-1, keepdims=True)
        cc = cv - mu
        var = jnp.mean(cc * cc, axis=-1, keepdims=True)
        yn = cc * lax.rsqrt(var + EPS) * lng_ref[...] + lnb_ref[...]
        o = (yn * jax.nn.sigmoid(yn)).astype(BF16)
    else:
        o = o_ref[0]
    y = _dot(o, wo_ref[...])
    if has_bias:
        y = y + bo_ref[...]
    x1 = x + mod[2:3] * y
    h2 = _modulate(x1, g2_ref[...], mod[3:4], mod[4:5]).astype(BF16)
    acc = jnp.zeros((tm, d), F32)
    d_ff = wup_ref.shape[1]
    for c0 in range(0, d_ff, d_ff_chunk):
        up = jnp.maximum(_dot(h2, wup_ref[:, c0:c0 + d_ff_chunk]), 0.0)
        acc = acc + _dot((up * up).astype(BF16), wdn_ref[c0:c0 + d_ff_chunk, :])
    out_ref[0] = x1 + mod[5:6] * acc


def _post_call(x, mixed, mod, w_o, b_o, gain2, w_up, w_down, conv_params, *, tm):
    b, t, d = x.shape
    d_in = w_o.shape[0]
    d_ff = w_up.shape[1]
    conv = conv_params is not None
    has_bias = b_o is not None
    tok = lambda bi, j: (bi, j, 0)
    in_specs = [pl.BlockSpec((1, tm, d), tok)]
    args = [x]
    scratch = []
    if conv:
        hb = tm // CONV_HALO
        last = t // CONV_HALO - 1
        in_specs += [pl.BlockSpec((1, CONV_HALO, d), lambda bi, j: (bi, jnp.maximum(j * hb - 1, 0), 0)),
                     pl.BlockSpec((1, tm, d), tok),
                     pl.BlockSpec((1, CONV_HALO, d), lambda bi, j: (bi, jnp.minimum((j + 1) * hb, last), 0)),
                     _resident((CONV_WIDTH, d)), _resident((1, d)), _resident((1, d)), _resident((1, d))]
        args += [mixed, mixed, mixed] + list(conv_params)
        scratch = [pltpu.VMEM((tm + 2 * CONV_HALO, d), F32), pltpu.VMEM((tm, d), F32)]
    else:
        in_specs += [pl.BlockSpec((1, tm, d_in), tok)]
        args += [mixed]
    in_specs += [pl.BlockSpec((1, 6, d), lambda bi, j: (bi, 0, 0)), _resident((d_in, d))]
    args += [mod, w_o]
    if has_bias:
        in_specs += [_resident((1, d))]
        args += [b_o]
    in_specs += [_resident((1, d)), _resident((d, d_ff)), _resident((d_ff, d))]
    args += [gain2, w_up, w_down]
    return pl.pallas_call(
        functools.partial(_post_kernel, conv=conv, has_bias=has_bias, d_ff_chunk=min(d_ff, 1024)),
        out_shape=jax.ShapeDtypeStruct((b, t, d), F32),
        grid=(b, t // tm),
        in_specs=in_specs,
        out_specs=pl.BlockSpec((1, tm, d), tok),
        scratch_shapes=scratch,
        compiler_params=_params(2),
        name="post_mlp",
    )(*args)


def _rope_tables(t):
    rows = t // GRID_W
    row = jnp.repeat(jnp.arange(rows, dtype=F32), GRID_W)
    col = jnp.tile(jnp.arange(GRID_W, dtype=F32), rows)
    half = HEAD_DIM // 2
    inv = 1.0 / jnp.power(ROPE_THETA, jnp.arange(0, half, 2, dtype=F32) / half)
    ang = jnp.concatenate([row[:, None] * inv, col[:, None] * inv], axis=-1)
    cos, sin = jnp.cos(ang), jnp.sin(ang)
    reps = LANES // HEAD_DIM
    return (jnp.tile(jnp.concatenate([cos, cos], axis=1), (1, reps)),
            jnp.tile(jnp.concatenate([-sin, sin], axis=1), (1, reps)))


def _split_pairs(a):
    lead = a.shape[:-1]
    a = a.reshape(lead + (-1, HEAD_DIM // 2, 2))
    return jnp.swapaxes(a, -1, -2).reshape(lead + (-1,))


def _prep_qkv(w_qkv, q_g, k_g, n_q, n_k):
    hq, hk = n_q // HEAD_DIM, n_k // HEAD_DIM
    w = _split_pairs(w_qkv[:, :n_q + n_k]).astype(BF16)
    wvt = w_qkv[:, n_q + n_k:].T.astype(BF16)
    scale = LOG2E / math.sqrt(HEAD_DIM)
    head_gain = jnp.concatenate([jnp.tile(_split_pairs(q_g) * scale, hq), jnp.tile(_split_pairs(k_g), hk)])
    return w, wvt, head_gain.reshape(1, n_q + n_k).astype(F32)


def _ones_block_diag():
    idx = jnp.arange(MXU_DIM) // HEAD_DIM
    return (idx[:, None] == idx[None, :]).astype(BF16)


def kernel(x, c, ctx, c_ctx, norm1_g, norm2_g, mod_w, mod_b, mlp_up, mlp_down,
           gqa_w_qkv, gqa_q_g, gqa_k_g, gqa_w_o,
           conv_w_pw1, conv_b_pw1, conv_w_dw, conv_b_dw, conv_ln_g, conv_ln_b, conv_w_pw2, conv_b_pw2,
           diff_w_qkv, diff_q_g, diff_k_g, diff_lam_q1, diff_lam_k1, diff_lam_q2, diff_lam_k2,
           diff_subln_g, diff_w_o,
           swa_w_qkv, swa_q_g, swa_k_g, swa_sink, swa_w_o):
    b, t, d = x.shape
    n_ctx = ctx.shape[1]
    depth = norm1_g.shape[0]
    assert depth == N_MIXERS and t % GRID_W == 0 and t % n_ctx == 0

    tm = _pick(t, (512, 256, 128))
    tq = _pick(t, (256, 128))
    tq_diff = _pick(t, (512, 256, 128))
    t_all = n_ctx + t
    tk = _pick(t_all, (768, 512, 256, 128))
    tkc = _pick(n_ctx, (768, 512, 256, 128))

    cv = jnp.concatenate([c, c_ctx[None], jnp.zeros((8 - b - 1, d), F32)], axis=0)
    mod_all = _mod_call(cv, mod_w, mod_b).reshape(depth, 8, 6, d)
    rope = _rope_tables(t)
    ones_bd = _ones_block_diag()
    row = lambda v: v.reshape(1, -1).astype(F32)

    xc = ctx
    for i in range(depth):
        m, j = i % N_MIXERS, i // N_MIXERS
        need_ctx = i < depth - 1
        mod_l = mod_all[i, :b]
        mod_c = jnp.broadcast_to(mod_all[i, b], (b, 6, d))
        g1, g2 = row(norm1_g[i]), row(norm2_g[i])
        w_up, w_down = mlp_up[i].astype(BF16), mlp_down[i].astype(BF16)
        conv_params = None
        b_o = None
        if m == 1:
            w1, b1 = conv_w_pw1[j].astype(BF16), row(conv_b_pw1[j])
            mixed = _glu_call(x, mod_l, g1, w1, b1, tm=tm)
            mixed_c = _glu_call(xc, mod_c, g1, w1, b1, tm=n_ctx) if need_ctx else None
            conv_params = (conv_w_dw[j].astype(F32), row(conv_b_dw[j]), row(conv_ln_g[j]), row(conv_ln_b[j]))
            w_o, b_o = conv_w_pw2[j].astype(BF16), row(conv_b_pw2[j])
        else:
            if m == 2:
                w_qkv, q_g, k_g, w_o = diff_w_qkv[j], diff_q_g[j], diff_k_g[j], diff_w_o[j]
                n_q = n_k = w_qkv.shape[1] // 3
                q_half = tuple(h % 2 for h in range(n_q // HEAD_DIM))
            else:
                w_qkv, q_g, k_g, w_o = ((gqa_w_qkv[j], gqa_q_g[j], gqa_k_g[j], gqa_w_o[j]) if m == 0 else
                                        (swa_w_qkv[j], swa_q_g[j], swa_k_g[j], swa_w_o[j]))
                n_q = w_o.shape[0]
                n_k = (w_qkv.shape[1] - n_q) // 2
                group = n_q // n_k
                q_half = tuple((h // group) % 2 for h in range(n_q // HEAD_DIM))
            w_o = w_o.astype(BF16)
            w_p, wvt, head_gain = _prep_qkv(w_qkv, q_g, k_g, n_q, n_k)
            proj = functools.partial(_proj_call, gain=g1, w=w_p, wvt=wvt, head_gain=head_gain,
                                     ones_bd=ones_bd, n_q=n_q, n_k=n_k, q_half=q_half)
            kv_buffers = (jnp.zeros((b, t_all, n_k), BF16), jnp.zeros((b, wvt.shape[0], t_all), BF16))
            q, *kv_buffers = proj(x, mod_l, rope=rope, tm=tm, kv_buffers=kv_buffers, first_row=0)
            qc, k_all, vt_all = proj(xc, mod_c, rope=None, tm=n_ctx, kv_buffers=kv_buffers, first_row=t)
            lat_keys = dict(tk=tk, key_rows=t_all, key_block=0)
            ctx_keys = dict(tk=tkc, key_rows=n_ctx, key_block=t // n_ctx)
            mixed_c = None
            if m == 0:
                mixed = _gqa_call(q, k_all, vt_all, group=group, tq=tq, **lat_keys)
                if need_ctx:
                    mixed_c = _gqa_call(qc, k_all, vt_all, group=group, tq=n_ctx, **ctx_keys)
            elif m == 2:
                lam_init = 0.8 - 0.6 * math.exp(-0.3 * i)
                lam_vecs = jnp.stack([diff_lam_q1[j], diff_lam_k1[j], diff_lam_q2[j], diff_lam_k2[j]]).astype(F32)
                sg = diff_subln_g[j].reshape(-1, 1).astype(F32)
                mixed = _diff_call(q, k_all, vt_all, lam_vecs, sg, lam_init=lam_init, tq=tq_diff, **lat_keys)
                if need_ctx:
                    mixed_c = _diff_call(qc, k_all, vt_all, lam_vecs, sg, lam_init=lam_init, tq=n_ctx, **ctx_keys)
            else:
                assert not need_ctx, "windowed layer with a context update is not supported"
                sink_rows = jnp.repeat(swa_sink[j].astype(F32).reshape(n_k // HEAD_DIM, 1, group), tq, axis=2)
                mixed = _window_call(q, k_all, vt_all, sink_rows, group=group, n_ctx=n_ctx, tq=tq)
        x = _post_call(x, mixed, mod_l, w_o, b_o, g2, w_up, w_down, conv_params, tm=tm)
        if need_ctx:
            xc = _post_call(xc, mixed_c, mod_c, w_o, b_o, g2, w_up, w_down, conv_params, tm=n_ctx)
    return x
```

```python
import functools
import math

import jax
import jax.numpy as jnp
from jax import lax
from jax.experimental import pallas as pl
from jax.experimental.pallas import tpu as pltpu

F32 = jnp.float32
BF16 = jnp.bfloat16

GRID_W = 64
HEAD_DIM = 64
N_MIXERS = 4
CONV_WIDTH = 31
WINDOW = 128
ROPE_THETA = 10000.0
EPS = 1e-6
NEG_INF = -1e30
LOG2E = math.log2(math.e)

LANES = 128
MXU_DIM = 256
VMEM_LIMIT_BYTES = 56 * 1024 * 1024

CONV_HALO = 16
CONV_ROWS = 64


def _params(n_axes):
    return pltpu.CompilerParams(dimension_semantics=("parallel",) * n_axes,
                                vmem_limit_bytes=VMEM_LIMIT_BYTES)


def _resident(shape):
    nd = len(shape)
    return pl.BlockSpec(shape, lambda *_: (0,) * nd, pipeline_mode=pl.Buffered(1))


def _pick(n, candidates):
    for cand in candidates:
        if n % cand == 0:
            return cand
    return n


def _modulate(x, gain, shift, scale):
    ms = jnp.mean(x * x, axis=-1, keepdims=True)
    return (x * lax.rsqrt(ms + EPS) * gain) * (1.0 + scale) + shift


def _dot(a, b):
    return jnp.dot(a, b, preferred_element_type=F32)


def _dot_nt(a, b):
    return lax.dot_general(a, b, (((1,), (1,)), ((), ())), preferred_element_type=F32)


def _mod_kernel(cv_ref, w_ref, b_ref, o_ref):
    cv = cv_ref[...]
    s = cv * jax.nn.sigmoid(cv)
    o_ref[0] = _dot(s.astype(BF16), w_ref[0].astype(BF16)) + b_ref[0]


def _mod_call(cv, mod_w, mod_b):
    depth, d, n = mod_w.shape
    rows = cv.shape[0]
    tn = _pick(n, (1536, 1024, 512))
    return pl.pallas_call(
        _mod_kernel,
        out_shape=jax.ShapeDtypeStruct((depth, rows, n), F32),
        grid=(depth, n // tn),
        in_specs=[pl.BlockSpec((rows, d), lambda i, j: (0, 0)),
                  pl.BlockSpec((1, d, tn), lambda i, j: (i, 0, j)),
                  pl.BlockSpec((1, 1, tn), lambda i, j: (i, 0, j))],
        out_specs=pl.BlockSpec((1, rows, tn), lambda i, j: (i, 0, j)),
        compiler_params=_params(2),
        name="modulation",
    )(cv, mod_w, mod_b.reshape(depth, 1, n))


def _proj_kernel(*refs, n_q, n_k, q_half, use_rope, n_alias):
    q_ref, k_ref, vt_ref = refs[-3:]
    refs = refs[:len(refs) - 3 - n_alias]
    if use_rope:
        (x_ref, mod_ref, g_ref, w_ref, wvt_ref, hg_ref, e_ref, cos_ref, sin_ref) = refs
    else:
        (x_ref, mod_ref, g_ref, w_ref, wvt_ref, hg_ref, e_ref) = refs
    x = x_ref[0]
    mod = mod_ref[0]
    tm = x.shape[0]
    h = _modulate(x, g_ref[...], mod[0:1], mod[1:2]).astype(BF16)
    qk = _dot(h, w_ref[...])
    vt_ref[0] = _dot_nt(wvt_ref[...], h).astype(BF16)
    n_qk = n_q + n_k
    sq = qk * qk
    hi = sq.astype(BF16)
    lo = (sq - hi.astype(F32)).astype(BF16)
    ones_bd = e_ref[...]
    ss = jnp.concatenate(
        [_dot(hi[:, t:t + MXU_DIM], ones_bd) + _dot(lo[:, t:t + MXU_DIM], ones_bd)
         for t in range(0, n_qk, MXU_DIM)], axis=1)
    qk = qk * lax.rsqrt(ss * (1.0 / HEAD_DIM) + EPS) * hg_ref[...]

    lane = lax.broadcasted_iota(jnp.int32, (tm, LANES), 1)
    tiles = [qk[:, t:t + LANES] for t in range(0, n_qk, LANES)]
    if use_rope:
        first = (lane % HEAD_DIM) < (HEAD_DIM // 2)
        cos = cos_ref[...]
        sin = sin_ref[...]
        tiles = [t * cos + jnp.where(first, pltpu.roll(t, LANES - HEAD_DIM // 2, axis=1),
                                     pltpu.roll(t, HEAD_DIM // 2, axis=1)) * sin
                 for t in tiles]
    low = lane < HEAD_DIM
    for j in range(n_q // HEAD_DIM):
        t = tiles[j // 2]
        if j % 2 != q_half[j]:
            t = pltpu.roll(t, HEAD_DIM, axis=1)
        keep = low if q_half[j] == 0 else jnp.logical_not(low)
        q_ref[0, j] = jnp.where(keep, t, 0.0).astype(BF16)
    k_ref[0] = jnp.concatenate(tiles[n_q // LANES:], axis=1).astype(BF16)


def _proj_call(x, mod, gain, w, wvt, head_gain, ones_bd, rope, *, n_q, n_k, q_half, tm, kv_buffers,
               first_row):
    b, t, d = x.shape
    n_v = wvt.shape[0]
    t_keys = kv_buffers[0].shape[1]
    hq = n_q // HEAD_DIM
    use_rope = rope is not None
    in_specs = [pl.BlockSpec((1, tm, d), lambda bi, j: (bi, j, 0)),
                pl.BlockSpec((1, 6, d), lambda bi, j: (bi, 0, 0)),
                _resident((1, d)),
                _resident((d, n_q + n_k)),
                _resident((n_v, d)),
                _resident((1, n_q + n_k)),
                _resident((MXU_DIM, MXU_DIM))]
    args = [x, mod, gain, w, wvt, head_gain, ones_bd]
    if use_rope:
        in_specs += [pl.BlockSpec((tm, LANES), lambda bi, j: (j, 0)),
                     pl.BlockSpec((tm, LANES), lambda bi, j: (j, 0))]
        args += list(rope)
    off = first_row // tm
    assert off * tm == first_row and first_row + t <= t_keys
    aliases = {len(args): 1, len(args) + 1: 2}
    in_specs += [pl.BlockSpec(memory_space=pl.ANY), pl.BlockSpec(memory_space=pl.ANY)]
    args += list(kv_buffers)
    return pl.pallas_call(
        functools.partial(_proj_kernel, n_q=n_q, n_k=n_k, q_half=q_half, use_rope=use_rope,
                          n_alias=len(aliases)),
        out_shape=(jax.ShapeDtypeStruct((b, hq, t, LANES), BF16),
                   jax.ShapeDtypeStruct((b, t_keys, n_k), BF16),
                   jax.ShapeDtypeStruct((b, n_v, t_keys), BF16)),
        grid=(b, t // tm),
        in_specs=in_specs,
        out_specs=(pl.BlockSpec((1, hq, tm, LANES), lambda bi, j: (bi, 0, j, 0)),
                   pl.BlockSpec((1, tm, n_k), lambda bi, j: (bi, j + off, 0)),
                   pl.BlockSpec((1, n_v, tm), lambda bi, j: (bi, 0, j + off))),
        input_output_aliases=aliases,
        compiler_params=_params(2),
        name="attn_proj",
    )(*args)


ONES_ROWS = 16
FLASH_UNROLL = 4


def _flash_loop(q_stacked, k_ref, load_vt, qt_ref, s_ref, cm_ref, m_ref, acc_ref, *, n_kv, tk):
    qt_ref[...] = q_stacked.astype(F32).T.astype(BF16)
    m_ref[...] = jnp.full(m_ref.shape, NEG_INF, F32)
    acc_ref[...] = jnp.zeros(acc_ref.shape, F32)
    ones = jnp.ones((ONES_ROWS, tk), BF16)
    m_cols = s_ref.shape[2]
    col_blocks = [slice(c0, c0 + MXU_DIM) for c0 in range(0, m_cols, MXU_DIM)]

    def produce(kc, slot, cols):
        s = _dot(kc, qt_ref[:, cols])
        s_ref[slot, :, cols] = s
        cm_ref[slot, :, cols] = jnp.max(s.reshape(tk // 8, 8, MXU_DIM), axis=0)

    def consume(vt1, slot, cols):
        m_prev = m_ref[:, cols]
        m_new = jnp.maximum(m_prev, jnp.max(cm_ref[slot, :, cols], axis=0, keepdims=True))
        alpha = jnp.exp2(m_prev - m_new)
        p = jnp.exp2(s_ref[slot, :, cols] - m_new).astype(BF16)
        acc_ref[:, cols] = alpha * acc_ref[:, cols] + _dot(vt1, p)
        m_ref[:, cols] = m_new

    def step(j_prod, slot_prod, j_cons, slot_cons):
        chunk_start = lambda j: j * tk if isinstance(j, int) else pl.multiple_of(j * tk, tk)
        if j_prod is not None:
            kc = k_ref[0, pl.ds(chunk_start(j_prod), tk), :]
        if j_cons is not None:
            vt1 = jnp.concatenate([load_vt(chunk_start(j_cons), tk), ones], axis=0)
        for cols in col_blocks:
            if j_prod is not None:
                produce(kc, slot_prod, cols)
            if j_cons is not None:
                consume(vt1, slot_cons, cols)

    step(0, 0, None, None)
    trips = (n_kv - 1) // FLASH_UNROLL if n_kv - 1 > FLASH_UNROLL else 0

    def body(trip, carry):
        j0 = trip * FLASH_UNROLL
        for u in range(FLASH_UNROLL):
            step(j0 + u + 1, (u + 1) % 2, j0 + u, u % 2)
        return carry

    if trips:
        lax.fori_loop(0, trips, body, 0)
    for j in range(trips * FLASH_UNROLL, n_kv - 1):
        step(j + 1, (j + 1) % 2, j, j % 2)
    step(None, None, n_kv - 1, (n_kv - 1) % 2)


def _store_heads_t(o_ref, ot, tq, group):
    for gg in range(group // 2):
        slab = jnp.concatenate([ot[:, (2 * gg) * tq:(2 * gg + 1) * tq],
                                ot[:, (2 * gg + 1) * tq:(2 * gg + 2) * tq]], axis=0)
        o_ref[0, :, gg * LANES:(gg + 1) * LANES] = slab.T.astype(o_ref.dtype)


def _kv_specs(key_rows, key_block, v_rows, kv_per_tile):
    return [pl.BlockSpec((1, key_rows, LANES), lambda bi, h, i: (bi, key_block, h // kv_per_tile)),
            pl.BlockSpec((1, v_rows, key_rows), lambda bi, h, i: (bi, h, key_block))]


def _flash_scratch(m_cols, tk, v_rows):
    return [pltpu.VMEM((LANES, m_cols), BF16),
            pltpu.VMEM((2, tk, m_cols), F32),
            pltpu.VMEM((2, 8, m_cols), F32),
            pltpu.VMEM((1, m_cols), F32),
            pltpu.VMEM((v_rows + ONES_ROWS, m_cols), F32)]


def _gqa_kernel(q_ref, k_ref, vt_ref, o_ref, qt_ref, s_ref, cm_ref, m_ref, acc_ref, *, n_kv, tk):
    group, tq = q_ref.shape[1:3]
    q_stacked = q_ref[0].reshape(group * tq, LANES)
    load_vt = lambda start, size: vt_ref[0, :, pl.ds(start, size)]
    _flash_loop(q_stacked, k_ref, load_vt, qt_ref, s_ref, cm_ref, m_ref, acc_ref, n_kv=n_kv, tk=tk)
    ot = acc_ref[0:HEAD_DIM, :] / acc_ref[HEAD_DIM:HEAD_DIM + 1, :]
    _store_heads_t(o_ref, ot, tq, group)


def _gqa_call(q, k_all, vt_all, *, group, tq, tk, key_rows, key_block):
    b, hq, t, _ = q.shape
    return pl.pallas_call(
        functools.partial(_gqa_kernel, n_kv=key_rows // tk, tk=tk),
        out_shape=jax.ShapeDtypeStruct((b, t, hq * HEAD_DIM), BF16),
        grid=(b, hq // group, t // tq),
        in_specs=[pl.BlockSpec((1, group, tq, LANES), lambda bi, h, i: (bi, h, i, 0))]
                 + _kv_specs(key_rows, key_block, HEAD_DIM, LANES // HEAD_DIM),
        out_specs=pl.BlockSpec((1, tq, group * HEAD_DIM), lambda bi, h, i: (bi, i, h)),
        scratch_shapes=_flash_scratch(group * tq, tk, HEAD_DIM),
        compiler_params=_params(3),
        name="gqa_attention",
    )(q, k_all, vt_all)


def _diff_kernel(q_ref, k_ref, vt_ref, lam_ref, sg_ref, o_ref, qt_ref, s_ref, cm_ref, m_ref, acc_ref, *,
                 lam_init, n_kv, tk):
    tq = q_ref.shape[2]
    dv = vt_ref.shape[1]
    q_stacked = q_ref[0].reshape(2 * tq, LANES)
    load_vt = lambda start, size: vt_ref[0, :, pl.ds(start, size)]
    _flash_loop(q_stacked, k_ref, load_vt, qt_ref, s_ref, cm_ref, m_ref, acc_ref, n_kv=n_kv, tk=tk)
    ot = acc_ref[0:dv, :] / acc_ref[dv:dv + 1, :]
    lv = lam_ref[...]
    lam = (jnp.exp(jnp.sum(lv[0:1] * lv[1:2], axis=1, keepdims=True))
           - jnp.exp(jnp.sum(lv[2:3] * lv[3:4], axis=1, keepdims=True)) + lam_init)
    ot = ot[:, :tq] - lam * ot[:, tq:]
    ms = jnp.mean(ot * ot, axis=0, keepdims=True)
    ot = (ot * lax.rsqrt(ms + EPS) * sg_ref[...]) * (1.0 - lam_init)
    o_ref[0] = ot.T.astype(o_ref.dtype)


def _diff_call(q, k_all, vt_all, lam_vecs, subln_g, *, lam_init, tq, tk, key_rows, key_block):
    b, hq, t, _ = q.shape
    heads = k_all.shape[2] // LANES
    dv = vt_all.shape[1] // heads
    return pl.pallas_call(
        functools.partial(_diff_kernel, lam_init=lam_init, n_kv=key_rows // tk, tk=tk),
        out_shape=jax.ShapeDtypeStruct((b, t, heads * dv), BF16),
        grid=(b, heads, t // tq),
        in_specs=[pl.BlockSpec((1, 2, tq, LANES), lambda bi, h, i: (bi, h, i, 0))]
                 + _kv_specs(key_rows, key_block, dv, 1)
                 + [pl.BlockSpec((4, HEAD_DIM), lambda bi, h, i: (0, 0)),
                    pl.BlockSpec((dv, 1), lambda bi, h, i: (0, 0))],
        out_specs=pl.BlockSpec((1, tq, dv), lambda bi, h, i: (bi, i, h)),
        scratch_shapes=_flash_scratch(2 * tq, tk, dv),
        compiler_params=_params(3),
        name="diff_attention",
    )(q, k_all, vt_all, lam_vecs, subln_g)


def _window_kernel(q_ref, k_ref, vt_ref, sink_ref, o_ref, *, n_ctx, t_lat):
    group, tq = q_ref.shape[1:3]
    span = tq + 2 * WINDOW
    i = pl.program_id(2)
    start = pl.multiple_of(jnp.clip(i * tq - WINDOW, 0, t_lat - span), WINDOW)
    k_lat = k_ref[0, pl.ds(start, span), :]
    k_ctx = k_ref[0, t_lat:t_lat + n_ctx, :]
    kpos = start + lax.broadcasted_iota(jnp.int32, (span, tq), 0)
    qpos = i * tq + lax.broadcasted_iota(jnp.int32, (span, tq), 1)
    in_band = jnp.abs(kpos - qpos) <= WINDOW
    in_band = jnp.concatenate([in_band] * group, axis=1)
    qt = q_ref[0].reshape(group * tq, LANES).astype(F32).T.astype(BF16)
    s_lat = jnp.where(in_band, _dot(k_lat, qt), NEG_INF)
    s_ctx = _dot(k_ctx, qt)
    sink = sink_ref[0] * LOG2E
    m = jnp.maximum(jnp.maximum(jnp.max(s_lat, axis=0, keepdims=True),
                                jnp.max(s_ctx, axis=0, keepdims=True)), sink)
    e_lat = jnp.exp2(s_lat - m).astype(BF16)
    e_ctx = jnp.exp2(s_ctx - m).astype(BF16)
    vt_lat = jnp.concatenate([vt_ref[0, :, pl.ds(start, span)], jnp.ones((ONES_ROWS, span), BF16)], axis=0)
    vt_ctx = jnp.concatenate([vt_ref[0, :, t_lat:t_lat + n_ctx], jnp.ones((ONES_ROWS, n_ctx), BF16)], axis=0)
    acc = _dot(vt_lat, e_lat) + _dot(vt_ctx, e_ctx)
    den = acc[HEAD_DIM:HEAD_DIM + 1, :] + jnp.exp2(sink - m)
    _store_heads_t(o_ref, acc[0:HEAD_DIM, :] / den, tq, group)


def _window_call(q, k_all, vt_all, sink_rows, *, group, n_ctx, tq):
    b, hq, t, _ = q.shape
    t_all = k_all.shape[1]
    return pl.pallas_call(
        functools.partial(_window_kernel, n_ctx=n_ctx, t_lat=t),
        out_shape=jax.ShapeDtypeStruct((b, t, hq * HEAD_DIM), BF16),
        grid=(b, hq // group, t // tq),
        in_specs=[pl.BlockSpec((1, group, tq, LANES), lambda bi, h, i: (bi, h, i, 0))]
                 + _kv_specs(t_all, 0, HEAD_DIM, LANES // HEAD_DIM)
                 + [pl.BlockSpec((1, 1, group * tq), lambda bi, h, i: (h, 0, 0))],
        out_specs=pl.BlockSpec((1, tq, group * HEAD_DIM), lambda bi, h, i: (bi, i, h)),
        compiler_params=_params(3),
        name="window_attention",
    )(q, k_all, vt_all, sink_rows)


def _glu_kernel(x_ref, mod_ref, g_ref, w_ref, b_ref, u_ref):
    x = x_ref[0]
    mod = mod_ref[0]
    d = x.shape[1]
    h = _modulate(x, g_ref[...], mod[0:1], mod[1:2]).astype(BF16)
    ag = _dot(h, w_ref[...]) + b_ref[...]
    u_ref[0] = (ag[:, :d] * jax.nn.sigmoid(ag[:, d:])).astype(u_ref.dtype)


def _glu_call(x, mod, gain, w, bias, *, tm):
    b, t, d = x.shape
    return pl.pallas_call(
        _glu_kernel,
        out_shape=jax.ShapeDtypeStruct((b, t, d), BF16),
        grid=(b, t // tm),
        in_specs=[pl.BlockSpec((1, tm, d), lambda bi, j: (bi, j, 0)),
                  pl.BlockSpec((1, 6, d), lambda bi, j: (bi, 0, 0)),
                  _resident((1, d)),
                  _resident((d, 2 * d)),
                  _resident((1, 2 * d))],
        out_specs=pl.BlockSpec((1, tm, d), lambda bi, j: (bi, j, 0)),
        compiler_params=_params(2),
        name="conv_glu",
    )(x, mod, gain, w, bias)


def _depthwise_conv(ubuf_ref, cbuf_ref, wdw_ref, bdw_ref, tm, d):
    win_rows = CONV_ROWS + 2 * CONV_HALO
    base = CONV_HALO - CONV_WIDTH // 2
    lane_chunk = LANES

    def step(r, carry):
        r0 = pl.multiple_of(r * CONV_ROWS, CONV_ROWS)
        for c0 in range(0, d, lane_chunk):
            win = ubuf_ref[pl.ds(r0, win_rows), c0:c0 + lane_chunk]
            acc = jnp.zeros((CONV_ROWS, lane_chunk), F32) + bdw_ref[:, c0:c0 + lane_chunk]
            for sub in range(8):
                shifted = win if sub == 0 else pltpu.roll(win, win_rows - sub, axis=0)
                for al in range(0, win_rows - CONV_ROWS + 1, 8):
                    k = al + sub - base
                    if 0 <= k < CONV_WIDTH:
                        acc = acc + wdw_ref[k:k + 1, c0:c0 + lane_chunk] * shifted[al:al + CONV_ROWS]
            cbuf_ref[pl.ds(r0, CONV_ROWS), c0:c0 + lane_chunk] = acc
        return carry

    lax.fori_loop(0, tm // CONV_ROWS, step, 0)


def _post_kernel(*refs, conv, has_bias, d_ff_chunk):
    it = iter(refs)
    x_ref = next(it)
    if conv:
        up_ref, uc_ref, un_ref = next(it), next(it), next(it)
        wdw_ref, bdw_ref, lng_ref, lnb_ref = next(it), next(it), next(it), next(it)
    else:
        o_ref = next(it)
    mod_ref, wo_ref = next(it), next(it)
    bo_ref = next(it) if has_bias else None
    g2_ref, wup_ref, wdn_ref, out_ref = next(it), next(it), next(it), next(it)
    if conv:
        ubuf_ref, cbuf_ref = next(it), next(it)

    x = x_ref[0]
    mod = mod_ref[0]
    tm, d = x.shape
    if conv:
        j = pl.program_id(1)
        prev = jnp.where(j > 0, up_ref[0].astype(F32), 0.0)
        nxt = jnp.where(j < pl.num_programs(1) - 1, un_ref[0].astype(F32), 0.0)
        ubuf_ref[0:CONV_HALO] = prev
        ubuf_ref[CONV_HALO:CONV_HALO + tm] = uc_ref[0].astype(F32)
        ubuf_ref[CONV_HALO + tm:] = nxt
        _depthwise_conv(ubuf_ref, cbuf_ref, wdw_ref, bdw_ref, tm, d)
        cv = cbuf_ref[...]
        mu = jnp.mean(cv, axis=-1, keepdims=True)
        cc = cv - mu
        var = jnp.mean(cc * cc, axis=-1, keepdims=True)
        yn = cc * lax.rsqrt(var + EPS) * lng_ref[...] + lnb_ref[...]
        o = (yn * jax.nn.sigmoid(yn)).astype(BF16)
    else:
        o = o_ref[0]
    y = _dot(o, wo_ref[...])
    if has_bias:
        y = y + bo_ref[...]
    x1 = x + mod[2:3] * y
    h2 = _modulate(x1, g2_ref[...], mod[3:4], mod[4:5]).astype(BF16)
    acc = jnp.zeros((tm, d), F32)
    d_ff = wup_ref.shape[1]
    for c0 in range(0, d_ff, d_ff_chunk):
        up = jnp.maximum(_dot(h2, wup_ref[:, c0:c0 + d_ff_chunk]), 0.0)
        acc = acc + _dot((up * up).astype(BF16), wdn_ref[c0:c0 + d_ff_chunk, :])
    out_ref[0] = x1 + mod[5:6] * acc


def _post_call(x, mixed, mod, w_o, b_o, gain2, w_up, w_down, conv_params, *, tm):
    b, t, d = x.shape
    d_in = w_o.shape[0]
    d_ff = w_up.shape[1]
    conv = conv_params is not None
    has_bias = b_o is not None
    tok = lambda bi, j: (bi, j, 0)
    in_specs = [pl.BlockSpec((1, tm, d), tok)]
    args = [x]
    scratch = []
    if conv:
        hb = tm // CONV_HALO
        last = t // CONV_HALO - 1
        in_specs += [pl.BlockSpec((1, CONV_HALO, d), lambda bi, j: (bi, jnp.maximum(j * hb - 1, 0), 0)),
                     pl.BlockSpec((1, tm, d), tok),
                     pl.BlockSpec((1, CONV_HALO, d), lambda bi, j: (bi, jnp.minimum((j + 1) * hb, last), 0)),
                     _resident((CONV_WIDTH, d)), _resident((1, d)), _resident((1, d)), _resident((1, d))]
        args += [mixed, mixed, mixed] + list(conv_params)
        scratch = [pltpu.VMEM((tm + 2 * CONV_HALO, d), F32), pltpu.VMEM((tm, d), F32)]
    else:
        in_specs += [pl.BlockSpec((1, tm, d_in), tok)]
        args += [mixed]
    in_specs += [pl.BlockSpec((1, 6, d), lambda bi, j: (bi, 0, 0)), _resident((d_in, d))]
    args += [mod, w_o]
    if has_bias:
        in_specs += [_resident((1, d))]
        args += [b_o]
    in_specs += [_resident((1, d)), _resident((d, d_ff)), _resident((d_ff, d))]
    args += [gain2, w_up, w_down]
    return pl.pallas_call(
        functools.partial(_post_kernel, conv=conv, has_bias=has_bias, d_ff_chunk=min(d_ff, 1024)),
        out_shape=jax.ShapeDtypeStruct((b, t, d), F32),
        grid=(b, t // tm),
        in_specs=in_specs,
        out_specs=pl.BlockSpec((1, tm, d), tok),
        scratch_shapes=scratch,
        compiler_params=_params(2),
        name="post_mlp",
    )(*args)


def _rope_tables(t):
    rows = t // GRID_W
    row = jnp.repeat(jnp.arange(rows, dtype=F32), GRID_W)
    col = jnp.tile(jnp.arange(GRID_W, dtype=F32), rows)
    half = HEAD_DIM // 2
    inv = 1.0 / jnp.power(ROPE_THETA, jnp.arange(0, half, 2, dtype=F32) / half)
    ang = jnp.concatenate([row[:, None] * inv, col[:, None] * inv], axis=-1)
    cos, sin = jnp.cos(ang), jnp.sin(ang)
    reps = LANES // HEAD_DIM
    return (jnp.tile(jnp.concatenate([cos, cos], axis=1), (1, reps)),
            jnp.tile(jnp.concatenate([-sin, sin], axis=1), (1, reps)))


def _split_pairs(a):
    lead = a.shape[:-1]
    a = a.reshape(lead + (-1, HEAD_DIM // 2, 2))
    return jnp.swapaxes(a, -1, -2).reshape(lead + (-1,))


def _prep_qkv(w_qkv, q_g, k_g, n_q, n_k):
    hq, hk = n_q // HEAD_DIM, n_k // HEAD_DIM
    w = _split_pairs(w_qkv[:, :n_q + n_k]).astype(BF16)
    wvt = w_qkv[:, n_q + n_k:].T.astype(BF16)
    scale = LOG2E / math.sqrt(HEAD_DIM)
    head_gain = jnp.concatenate([jnp.tile(_split_pairs(q_g) * scale, hq), jnp.tile(_split_pairs(k_g), hk)])
    return w, wvt, head_gain.reshape(1, n_q + n_k).astype(F32)


def _ones_block_diag():
    idx = jnp.arange(MXU_DIM) // HEAD_DIM
    return (idx[:, None] == idx[None, :]).astype(BF16)


def kernel(x, c, ctx, c_ctx, norm1_g, norm2_g, mod_w, mod_b, mlp_up, mlp_down,
           gqa_w_qkv, gqa_q_g, gqa_k_g, gqa_w_o,
           conv_w_pw1, conv_b_pw1, conv_w_dw, conv_b_dw, conv_ln_g, conv_ln_b, conv_w_pw2, conv_b_pw2,
           diff_w_qkv, diff_q_g, diff_k_g, diff_lam_q1, diff_lam_k1, diff_lam_q2, diff_lam_k2,
           diff_subln_g, diff_w_o,
           swa_w_qkv, swa_q_g, swa_k_g, swa_sink, swa_w_o):
    b, t, d = x.shape
    n_ctx = ctx.shape[1]
    depth = norm1_g.shape[0]
    assert depth == N_MIXERS and t % GRID_W == 0 and t % n_ctx == 0

    tm = _pick(t, (512, 256, 128))
    tq_win = _pick(t, (256, 128))
    tq_gqa = _pick(t, (512, 256, 128))
    tq_diff = _pick(t, (1024, 512, 256, 128))
    t_all = n_ctx + t
    tk = _pick(t_all, (768, 512, 256, 128))
    tkc = _pick(n_ctx, (768, 512, 256, 128))

    cv = jnp.concatenate([c, c_ctx[None], jnp.zeros((8 - b - 1, d), F32)], axis=0)
    mod_all = _mod_call(cv, mod_w, mod_b).reshape(depth, 8, 6, d)
    rope = _rope_tables(t)
    ones_bd = _ones_block_diag()
    row = lambda v: v.reshape(1, -1).astype(F32)

    xc = ctx
    for i in range(depth):
        m, j = i % N_MIXERS, i // N_MIXERS
        need_ctx = i < depth - 1
        mod_l = mod_all[i, :b]
        mod_c = jnp.broadcast_to(mod_all[i, b], (b, 6, d))
        g1, g2 = row(norm1_g[i]), row(norm2_g[i])
        w_up, w_down = mlp_up[i].astype(BF16), mlp_down[i].astype(BF16)
        conv_params = None
        b_o = None
        if m == 1:
            w1, b1 = conv_w_pw1[j].astype(BF16), row(conv_b_pw1[j])
            mixed = _glu_call(x, mod_l, g1, w1, b1, tm=tm)
            mixed_c = _glu_call(xc, mod_c, g1, w1, b1, tm=n_ctx) if need_ctx else None
            conv_params = (conv_w_dw[j].astype(F32), row(conv_b_dw[j]), row(conv_ln_g[j]), row(conv_ln_b[j]))
            w_o, b_o = conv_w_pw2[j].astype(BF16), row(conv_b_pw2[j])
        else:
            if m == 2:
                w_qkv, q_g, k_g, w_o = diff_w_qkv[j], diff_q_g[j], diff_k_g[j], diff_w_o[j]
                n_q = n_k = w_qkv.shape[1] // 3
                q_half = tuple(h % 2 for h in range(n_q // HEAD_DIM))
            else:
                w_qkv, q_g, k_g, w_o = ((gqa_w_qkv[j], gqa_q_g[j], gqa_k_g[j], gqa_w_o[j]) if m == 0 else
                                        (swa_w_qkv[j], swa_q_g[j], swa_k_g[j], swa_w_o[j]))
                n_q = w_o.shape[0]
                n_k = (w_qkv.shape[1] - n_q) // 2
                group = n_q // n_k
                q_half = tuple((h // group) % 2 for h in range(n_q // HEAD_DIM))
            w_o = w_o.astype(BF16)
            w_p, wvt, head_gain = _prep_qkv(w_qkv, q_g, k_g, n_q, n_k)
            proj = functools.partial(_proj_call, gain=g1, w=w_p, wvt=wvt, head_gain=head_gain,
                                     ones_bd=ones_bd, n_q=n_q, n_k=n_k, q_half=q_half)
            kv_buffers = (jnp.zeros((b, t_all, n_k), BF16), jnp.zeros((b, wvt.shape[0], t_all), BF16))
            q, *kv_buffers = proj(x, mod_l, rope=rope, tm=tm, kv_buffers=kv_buffers, first_row=0)
            qc, k_all, vt_all = proj(xc, mod_c, rope=None, tm=n_ctx, kv_buffers=kv_buffers, first_row=t)
            lat_keys = dict(tk=tk, key_rows=t_all, key_block=0)
            ctx_keys = dict(tk=tkc, key_rows=n_ctx, key_block=t // n_ctx)
            mixed_c = None
            if m == 0:
                mixed = _gqa_call(q, k_all, vt_all, group=group, tq=tq_gqa, **lat_keys)
                if need_ctx:
                    mixed_c = _gqa_call(qc, k_all, vt_all, group=group, tq=n_ctx, **ctx_keys)
            elif m == 2:
                lam_init = 0.8 - 0.6 * math.exp(-0.3 * i)
                lam_vecs = jnp.stack([diff_lam_q1[j], diff_lam_k1[j], diff_lam_q2[j], diff_lam_k2[j]]).astype(F32)
                sg = diff_subln_g[j].reshape(-1, 1).astype(F32)
                mixed = _diff_call(q, k_all, vt_all, lam_vecs, sg, lam_init=lam_init, tq=tq_diff, **lat_keys)
                if need_ctx:
                    mixed_c = _diff_call(qc, k_all, vt_all, lam_vecs, sg, lam_init=lam_init, tq=n_ctx, **ctx_keys)
            else:
                assert not need_ctx, "windowed layer with a context update is not supported"
                sink_rows = jnp.repeat(swa_sink[j].astype(F32).reshape(n_k // HEAD_DIM, 1, group), tq_win, axis=2)
                mixed = _window_call(q, k_all, vt_all, sink_rows, group=group, n_ctx=n_ctx, tq=tq_win)
        x = _post_call(x, mixed, mod_l, w_o, b_o, g2, w_up, w_down, conv_params, tm=tm)
        if need_ctx:
            xc = _post_call(xc, mixed_c, mod_c, w_o, b_o, g2, w_up, w_down, conv_params, tm=n_ctx)
    return x
```

```python
import functools
import math

import jax
import jax.numpy as jnp
from jax import lax
from jax.experimental import pallas as pl
from jax.experimental.pallas import tpu as pltpu

F32 = jnp.float32
BF16 = jnp.bfloat16

GRID_W = 64
HEAD_DIM = 64
N_MIXERS = 4
CONV_WIDTH = 31
WINDOW = 128
ROPE_THETA = 10000.0
EPS = 1e-6
NEG_INF = -1e30
LOG2E = math.log2(math.e)

LANES = 128
MXU_DIM = 256
VMEM_LIMIT_BYTES = 56 * 1024 * 1024

CONV_HALO = 16
CONV_ROWS = 64


def _params(n_axes):
    return pltpu.CompilerParams(dimension_semantics=("parallel",) * n_axes,
                                vmem_limit_bytes=VMEM_LIMIT_BYTES)


def _resident(shape):
    nd = len(shape)
    return pl.BlockSpec(shape, lambda *_: (0,) * nd, pipeline_mode=pl.Buffered(1))


def _pick(n, candidates):
    for cand in candidates:
        if n % cand == 0:
            return cand
    return n


def _modulate(x, gain, shift, scale):
    ms = jnp.mean(x * x, axis=-1, keepdims=True)
    return (x * lax.rsqrt(ms + EPS) * gain) * (1.0 + scale) + shift


def _dot(a, b):
    return jnp.dot(a, b, preferred_element_type=F32)


def _dot_nt(a, b):
    return lax.dot_general(a, b, (((1,), (1,)), ((), ())), preferred_element_type=F32)


def _mod_kernel(cv_ref, w_ref, b_ref, o_ref):
    cv = cv_ref[...]
    s = cv * jax.nn.sigmoid(cv)
    o_ref[0] = _dot(s.astype(BF16), w_ref[0].astype(BF16)) + b_ref[0]


def _mod_call(cv, mod_w, mod_b):
    depth, d, n = mod_w.shape
    rows = cv.shape[0]
    tn = _pick(n, (1536, 1024, 512))
    return pl.pallas_call(
        _mod_kernel,
        out_shape=jax.ShapeDtypeStruct((depth, rows, n), F32),
        grid=(depth, n // tn),
        in_specs=[pl.BlockSpec((rows, d), lambda i, j: (0, 0)),
                  pl.BlockSpec((1, d, tn), lambda i, j: (i, 0, j)),
                  pl.BlockSpec((1, 1, tn), lambda i, j: (i, 0, j))],
        out_specs=pl.BlockSpec((1, rows, tn), lambda i, j: (i, 0, j)),
        compiler_params=_params(2),
        name="modulation",
    )(cv, mod_w, mod_b.reshape(depth, 1, n))


def _proj_kernel(*refs, n_q, n_k, q_half, use_rope, n_alias):
    q_ref, k_ref, vt_ref = refs[-3:]
    refs = refs[:len(refs) - 3 - n_alias]
    if use_rope:
        (x_ref, mod_ref, g_ref, w_ref, wvt_ref, hg_ref, e_ref, cos_ref, sin_ref) = refs
    else:
        (x_ref, mod_ref, g_ref, w_ref, wvt_ref, hg_ref, e_ref) = refs
    x = x_ref[0]
    mod = mod_ref[0]
    tm = x.shape[0]
    h = _modulate(x, g_ref[...], mod[0:1], mod[1:2]).astype(BF16)
    qk = _dot(h, w_ref[...])
    vt_ref[0] = _dot_nt(wvt_ref[...], h).astype(BF16)
    n_qk = n_q + n_k
    sq = qk * qk
    hi = sq.astype(BF16)
    lo = (sq - hi.astype(F32)).astype(BF16)
    ones_bd = e_ref[...]
    ss = jnp.concatenate(
        [_dot(hi[:, t:t + MXU_DIM], ones_bd) + _dot(lo[:, t:t + MXU_DIM], ones_bd)
         for t in range(0, n_qk, MXU_DIM)], axis=1)
    qk = qk * lax.rsqrt(ss * (1.0 / HEAD_DIM) + EPS) * hg_ref[...]

    lane = lax.broadcasted_iota(jnp.int32, (tm, LANES), 1)
    tiles = [qk[:, t:t + LANES] for t in range(0, n_qk, LANES)]
    if use_rope:
        first = (lane % HEAD_DIM) < (HEAD_DIM // 2)
        cos = cos_ref[...]
        sin = sin_ref[...]
        tiles = [t * cos + jnp.where(first, pltpu.roll(t, LANES - HEAD_DIM // 2, axis=1),
                                     pltpu.roll(t, HEAD_DIM // 2, axis=1)) * sin
                 for t in tiles]
    low = lane < HEAD_DIM
    for j in range(n_q // HEAD_DIM):
        t = tiles[j // 2]
        if j % 2 != q_half[j]:
            t = pltpu.roll(t, HEAD_DIM, axis=1)
        keep = low if q_half[j] == 0 else jnp.logical_not(low)
        q_ref[0, j] = jnp.where(keep, t, 0.0).astype(BF16)
    k_ref[0] = jnp.concatenate(tiles[n_q // LANES:], axis=1).astype(BF16)


def _proj_call(x, mod, gain, w, wvt, head_gain, ones_bd, rope, *, n_q, n_k, q_half, tm, kv_buffers,
               first_row):
    b, t, d = x.shape
    n_v = wvt.shape[0]
    t_keys = kv_buffers[0].shape[1]
    hq = n_q // HEAD_DIM
    use_rope = rope is not None
    in_specs = [pl.BlockSpec((1, tm, d), lambda bi, j: (bi, j, 0)),
                pl.BlockSpec((1, 6, d), lambda bi, j: (bi, 0, 0)),
                _resident((1, d)),
                _resident((d, n_q + n_k)),
                _resident((n_v, d)),
                _resident((1, n_q + n_k)),
                _resident((MXU_DIM, MXU_DIM))]
    args = [x, mod, gain, w, wvt, head_gain, ones_bd]
    if use_rope:
        in_specs += [pl.BlockSpec((tm, LANES), lambda bi, j: (j, 0)),
                     pl.BlockSpec((tm, LANES), lambda bi, j: (j, 0))]
        args += list(rope)
    off = first_row // tm
    assert off * tm == first_row and first_row + t <= t_keys
    aliases = {len(args): 1, len(args) + 1: 2}
    in_specs += [pl.BlockSpec(memory_space=pl.ANY), pl.BlockSpec(memory_space=pl.ANY)]
    args += list(kv_buffers)
    return pl.pallas_call(
        functools.partial(_proj_kernel, n_q=n_q, n_k=n_k, q_half=q_half, use_rope=use_rope,
                          n_alias=len(aliases)),
        out_shape=(jax.ShapeDtypeStruct((b, hq, t, LANES), BF16),
                   jax.ShapeDtypeStruct((b, t_keys, n_k), BF16),
                   jax.ShapeDtypeStruct((b, n_v, t_keys), BF16)),
        grid=(b, t // tm),
        in_specs=in_specs,
        out_specs=(pl.BlockSpec((1, hq, tm, LANES), lambda bi, j: (bi, 0, j, 0)),
                   pl.BlockSpec((1, tm, n_k), lambda bi, j: (bi, j + off, 0)),
                   pl.BlockSpec((1, n_v, tm), lambda bi, j: (bi, 0, j + off))),
        input_output_aliases=aliases,
        compiler_params=_params(2),
        name="attn_proj",
    )(*args)


ONES_ROWS = 16
FLASH_UNROLL = 4


def _flash_loop(q_stacked, k_ref, load_vt, qt_ref, s_ref, cm_ref, m_ref, acc_ref, *, n_kv, tk):
    qt_ref[...] = q_stacked.astype(F32).T.astype(BF16)
    m_ref[...] = jnp.full(m_ref.shape, NEG_INF, F32)
    acc_ref[...] = jnp.zeros(acc_ref.shape, F32)
    ones = jnp.ones((ONES_ROWS, tk), BF16)
    m_cols = s_ref.shape[2]
    col_blocks = [slice(c0, c0 + MXU_DIM) for c0 in range(0, m_cols, MXU_DIM)]

    def produce(kc, slot, cols):
        s = _dot(kc, qt_ref[:, cols])
        s_ref[slot, :, cols] = s
        cm_ref[slot, :, cols] = jnp.max(s.reshape(tk // 8, 8, MXU_DIM), axis=0)

    def consume(vt1, slot, cols):
        m_prev = m_ref[:, cols]
        m_new = jnp.maximum(m_prev, jnp.max(cm_ref[slot, :, cols], axis=0, keepdims=True))
        alpha = jnp.exp2(m_prev - m_new)
        p = jnp.exp2(s_ref[slot, :, cols] - m_new).astype(BF16)
        acc_ref[:, cols] = alpha * acc_ref[:, cols] + _dot(vt1, p)
        m_ref[:, cols] = m_new

    def step(j_prod, slot_prod, j_cons, slot_cons):
        chunk_start = lambda j: j * tk if isinstance(j, int) else pl.multiple_of(j * tk, tk)
        if j_prod is not None:
            kc = k_ref[0, pl.ds(chunk_start(j_prod), tk), :]
        if j_cons is not None:
            vt1 = jnp.concatenate([load_vt(chunk_start(j_cons), tk), ones], axis=0)
        for cols in col_blocks:
            if j_prod is not None:
                produce(kc, slot_prod, cols)
            if j_cons is not None:
                consume(vt1, slot_cons, cols)

    step(0, 0, None, None)
    trips = (n_kv - 1) // FLASH_UNROLL if n_kv - 1 > FLASH_UNROLL else 0

    def body(trip, carry):
        j0 = trip * FLASH_UNROLL
        for u in range(FLASH_UNROLL):
            step(j0 + u + 1, (u + 1) % 2, j0 + u, u % 2)
        return carry

    if trips:
        lax.fori_loop(0, trips, body, 0)
    for j in range(trips * FLASH_UNROLL, n_kv - 1):
        step(j + 1, (j + 1) % 2, j, j % 2)
    step(None, None, n_kv - 1, (n_kv - 1) % 2)


def _store_heads_t(o_ref, ot, tq, group):
    for gg in range(group // 2):
        slab = jnp.concatenate([ot[:, (2 * gg) * tq:(2 * gg + 1) * tq],
                                ot[:, (2 * gg + 1) * tq:(2 * gg + 2) * tq]], axis=0)
        o_ref[0, :, gg * LANES:(gg + 1) * LANES] = slab.T.astype(o_ref.dtype)


def _kv_specs(key_rows, key_block, v_rows, kv_per_tile):
    return [pl.BlockSpec((1, key_rows, LANES), lambda bi, h, i: (bi, key_block, h // kv_per_tile)),
            pl.BlockSpec((1, v_rows, key_rows), lambda bi, h, i: (bi, h, key_block))]


def _flash_scratch(m_cols, tk, v_rows):
    return [pltpu.VMEM((LANES, m_cols), BF16),
            pltpu.VMEM((2, tk, m_cols), F32),
            pltpu.VMEM((2, 8, m_cols), F32),
            pltpu.VMEM((1, m_cols), F32),
            pltpu.VMEM((v_rows + ONES_ROWS, m_cols), F32)]


def _gqa_kernel(q_ref, k_ref, vt_ref, o_ref, qt_ref, s_ref, cm_ref, m_ref, acc_ref, *, n_kv, tk):
    group, tq = q_ref.shape[1:3]
    q_stacked = q_ref[0].reshape(group * tq, LANES)
    load_vt = lambda start, size: vt_ref[0, :, pl.ds(start, size)]
    _flash_loop(q_stacked, k_ref, load_vt, qt_ref, s_ref, cm_ref, m_ref, acc_ref, n_kv=n_kv, tk=tk)
    ot = acc_ref[0:HEAD_DIM, :] / acc_ref[HEAD_DIM:HEAD_DIM + 1, :]
    _store_heads_t(o_ref, ot, tq, group)


def _gqa_call(q, k_all, vt_all, *, group, tq, tk, key_rows, key_block):
    b, hq, t, _ = q.shape
    return pl.pallas_call(
        functools.partial(_gqa_kernel, n_kv=key_rows // tk, tk=tk),
        out_shape=jax.ShapeDtypeStruct((b, t, hq * HEAD_DIM), BF16),
        grid=(b, hq // group, t // tq),
        in_specs=[pl.BlockSpec((1, group, tq, LANES), lambda bi, h, i: (bi, h, i, 0))]
                 + _kv_specs(key_rows, key_block, HEAD_DIM, LANES // HEAD_DIM),
        out_specs=pl.BlockSpec((1, tq, group * HEAD_DIM), lambda bi, h, i: (bi, i, h)),
        scratch_shapes=_flash_scratch(group * tq, tk, HEAD_DIM),
        compiler_params=_params(3),
        name="gqa_attention",
    )(q, k_all, vt_all)


def _diff_kernel(q_ref, k_ref, vt_ref, lam_ref, sg_ref, o_ref, qt_ref, s_ref, cm_ref, m_ref, acc_ref, *,
                 lam_init, n_kv, tk):
    tq = q_ref.shape[2]
    dv = vt_ref.shape[1]
    q_stacked = q_ref[0].reshape(2 * tq, LANES)
    load_vt = lambda start, size: vt_ref[0, :, pl.ds(start, size)]
    _flash_loop(q_stacked, k_ref, load_vt, qt_ref, s_ref, cm_ref, m_ref, acc_ref, n_kv=n_kv, tk=tk)
    ot = acc_ref[0:dv, :] / acc_ref[dv:dv + 1, :]
    lv = lam_ref[...]
    lam = (jnp.exp(jnp.sum(lv[0:1] * lv[1:2], axis=1, keepdims=True))
           - jnp.exp(jnp.sum(lv[2:3] * lv[3:4], axis=1, keepdims=True)) + lam_init)
    ot = ot[:, :tq] - lam * ot[:, tq:]
    ms = jnp.mean(ot * ot, axis=0, keepdims=True)
    ot = (ot * lax.rsqrt(ms + EPS) * sg_ref[...]) * (1.0 - lam_init)
    o_ref[0] = ot.T.astype(o_ref.dtype)


def _diff_call(q, k_all, vt_all, lam_vecs, subln_g, *, lam_init, tq, tk, key_rows, key_block):
    b, hq, t, _ = q.shape
    heads = k_all.shape[2] // LANES
    dv = vt_all.shape[1] // heads
    return pl.pallas_call(
        functools.partial(_diff_kernel, lam_init=lam_init, n_kv=key_rows // tk, tk=tk),
        out_shape=jax.ShapeDtypeStruct((b, t, heads * dv), BF16),
        grid=(b, heads, t // tq),
        in_specs=[pl.BlockSpec((1, 2, tq, LANES), lambda bi, h, i: (bi, h, i, 0))]
                 + _kv_specs(key_rows, key_block, dv, 1)
                 + [pl.BlockSpec((4, HEAD_DIM), lambda bi, h, i: (0, 0)),
                    pl.BlockSpec((dv, 1), lambda bi, h, i: (0, 0))],
        out_specs=pl.BlockSpec((1, tq, dv), lambda bi, h, i: (bi, i, h)),
        scratch_shapes=_flash_scratch(2 * tq, tk, dv),
        compiler_params=_params(3),
        name="diff_attention",
    )(q, k_all, vt_all, lam_vecs, subln_g)


def _window_kernel(q_ref, k_ref, vt_ref, sink_ref, o_ref, *, n_ctx, t_lat):
    group, tq = q_ref.shape[1:3]
    span = tq + 2 * WINDOW
    i = pl.program_id(2)
    start = pl.multiple_of(jnp.clip(i * tq - WINDOW, 0, t_lat - span), WINDOW)
    k_lat = k_ref[0, pl.ds(start, span), :]
    k_ctx = k_ref[0, t_lat:t_lat + n_ctx, :]
    kpos = start + lax.broadcasted_iota(jnp.int32, (span, tq), 0)
    qpos = i * tq + lax.broadcasted_iota(jnp.int32, (span, tq), 1)
    in_band = jnp.abs(kpos - qpos) <= WINDOW
    in_band = jnp.concatenate([in_band] * group, axis=1)
    qt = q_ref[0].reshape(group * tq, LANES).astype(F32).T.astype(BF16)
    s_lat = jnp.where(in_band, _dot(k_lat, qt), NEG_INF)
    s_ctx = _dot(k_ctx, qt)
    sink = sink_ref[0] * LOG2E
    m = jnp.maximum(jnp.maximum(jnp.max(s_lat, axis=0, keepdims=True),
                                jnp.max(s_ctx, axis=0, keepdims=True)), sink)
    e_lat = jnp.exp2(s_lat - m).astype(BF16)
    e_ctx = jnp.exp2(s_ctx - m).astype(BF16)
    vt_lat = jnp.concatenate([vt_ref[0, :, pl.ds(start, span)], jnp.ones((ONES_ROWS, span), BF16)], axis=0)
    vt_ctx = jnp.concatenate([vt_ref[0, :, t_lat:t_lat + n_ctx], jnp.ones((ONES_ROWS, n_ctx), BF16)], axis=0)
    acc = _dot(vt_lat, e_lat) + _dot(vt_ctx, e_ctx)
    den = acc[HEAD_DIM:HEAD_DIM + 1, :] + jnp.exp2(sink - m)
    _store_heads_t(o_ref, acc[0:HEAD_DIM, :] / den, tq, group)


def _window_call(q, k_all, vt_all, sink_rows, *, group, n_ctx, tq):
    b, hq, t, _ = q.shape
    t_all = k_all.shape[1]
    return pl.pallas_call(
        functools.partial(_window_kernel, n_ctx=n_ctx, t_lat=t),
        out_shape=jax.ShapeDtypeStruct((b, t, hq * HEAD_DIM), BF16),
        grid=(b, hq // group, t // tq),
        in_specs=[pl.BlockSpec((1, group, tq, LANES), lambda bi, h, i: (bi, h, i, 0))]
                 + _kv_specs(t_all, 0, HEAD_DIM, LANES // HEAD_DIM)
                 + [pl.BlockSpec((1, 1, group * tq), lambda bi, h, i: (h, 0, 0))],
        out_specs=pl.BlockSpec((1, tq, group * HEAD_DIM), lambda bi, h, i: (bi, i, h)),
        compiler_params=_params(3),
        name="window_attention",
    )(q, k_all, vt_all, sink_rows)


def _glu_kernel(x_ref, mod_ref, g_ref, w_ref, b_ref, u_ref):
    x = x_ref[0]
    mod = mod_ref[0]
    d = x.shape[1]
    h = _modulate(x, g_ref[...], mod[0:1], mod[1:2]).astype(BF16)
    ag = _dot(h, w_ref[...]) + b_ref[...]
    u_ref[0] = (ag[:, :d] * jax.nn.sigmoid(ag[:, d:])).astype(u_ref.dtype)


def _glu_call(x, mod, gain, w, bias, *, tm):
    b, t, d = x.shape
    return pl.pallas_call(
        _glu_kernel,
        out_shape=jax.ShapeDtypeStruct((b, t, d), BF16),
        grid=(b, t // tm),
        in_specs=[pl.BlockSpec((1, tm, d), lambda bi, j: (bi, j, 0)),
                  pl.BlockSpec((1, 6, d), lambda bi, j: (bi, 0, 0)),
                  _resident((1, d)),
                  _resident((d, 2 * d)),
                  _resident((1, 2 * d))],
        out_specs=pl.BlockSpec((1, tm, d), lambda bi, j: (bi, j, 0)),
        compiler_params=_params(2),
        name="conv_glu",
    )(x, mod, gain, w, bias)


def _depthwise_conv(ubuf_ref, cbuf_ref, wdw_ref, bdw_ref, tm, d):
    win_rows = CONV_ROWS + 2 * CONV_HALO
    base = CONV_HALO - CONV_WIDTH // 2
    lane_chunk = LANES

    def step(r, carry):
        r0 = pl.multiple_of(r * CONV_ROWS, CONV_ROWS)
        for c0 in range(0, d, lane_chunk):
            win = ubuf_ref[pl.ds(r0, win_rows), c0:c0 + lane_chunk]
            acc = jnp.zeros((CONV_ROWS, lane_chunk), F32) + bdw_ref[:, c0:c0 + lane_chunk]
            for sub in range(8):
                shifted = win if sub == 0 else pltpu.roll(win, win_rows - sub, axis=0)
                for al in range(0, win_rows - CONV_ROWS + 1, 8):
                    k = al + sub - base
                    if 0 <= k < CONV_WIDTH:
                        acc = acc + wdw_ref[k:k + 1, c0:c0 + lane_chunk] * shifted[al:al + CONV_ROWS]
            cbuf_ref[pl.ds(r0, CONV_ROWS), c0:c0 + lane_chunk] = acc
        return carry

    lax.fori_loop(0, tm // CONV_ROWS, step, 0)


def _post_kernel(*refs, conv, has_bias, d_ff_chunk):
    it = iter(refs)
    x_ref = next(it)
    if conv:
        up_ref, uc_ref, un_ref = next(it), next(it), next(it)
        wdw_ref, bdw_ref, lng_ref, lnb_ref = next(it), next(it), next(it), next(it)
    else:
        o_ref = next(it)
    mod_ref, wo_ref = next(it), next(it)
    bo_ref = next(it) if has_bias else None
    g2_ref, wup_ref, wdn_ref, out_ref = next(it), next(it), next(it), next(it)
    if conv:
        ubuf_ref, cbuf_ref = next(it), next(it)

    x = x_ref[0]
    mod = mod_ref[0]
    tm, d = x.shape
    if conv:
        j = pl.program_id(1)
        prev = jnp.where(j > 0, up_ref[0].astype(F32), 0.0)
        nxt = jnp.where(j < pl.num_programs(1) - 1, un_ref[0].astype(F32), 0.0)
        ubuf_ref[0:CONV_HALO] = prev
        ubuf_ref[CONV_HALO:CONV_HALO + tm] = uc_ref[0].astype(F32)
        ubuf_ref[CONV_HALO + tm:] = nxt
        _depthwise_conv(ubuf_ref, cbuf_ref, wdw_ref, bdw_ref, tm, d)
        cv = cbuf_ref[...]
        mu = jnp.mean(cv, axis=-1, keepdims=True)
        cc = cv - mu
        var = jnp.mean(cc * cc, axis=-1, keepdims=True)
        yn = cc * lax.rsqrt(var + EPS) * lng_ref[...] + lnb_ref[...]
        o = (yn * jax.nn.sigmoid(yn)).astype(BF16)
    else:
        o = o_ref[0]
    y = _dot(o, wo_ref[...])
    if has_bias:
        y = y + bo_ref[...]
    x1 = x + mod[2:3] * y
    h2 = _modulate(x1, g2_ref[...], mod[3:4], mod[4:5]).astype(BF16)
    acc = jnp.zeros((tm, d), F32)
    d_ff = wup_ref.shape[1]
    for c0 in range(0, d_ff, d_ff_chunk):
        up = jnp.maximum(_dot(h2, wup_ref[:, c0:c0 + d_ff_chunk]), 0.0)
        acc = acc + _dot((up * up).astype(BF16), wdn_ref[c0:c0 + d_ff_chunk, :])
    out_ref[0] = x1 + mod[5:6] * acc


def _post_call(x, mixed, mod, w_o, b_o, gain2, w_up, w_down, conv_params, *, tm):
    b, t, d = x.shape
    d_in = w_o.shape[0]
    d_ff = w_up.shape[1]
    conv = conv_params is not None
    has_bias = b_o is not None
    tok = lambda bi, j: (bi, j, 0)
    in_specs = [pl.BlockSpec((1, tm, d), tok)]
    args = [x]
    scratch = []
    if conv:
        hb = tm // CONV_HALO
        last = t // CONV_HALO - 1
        in_specs += [pl.BlockSpec((1, CONV_HALO, d), lambda bi, j: (bi, jnp.maximum(j * hb - 1, 0), 0)),
                     pl.BlockSpec((1, tm, d), tok),
                     pl.BlockSpec((1, CONV_HALO, d), lambda bi, j: (bi, jnp.minimum((j + 1) * hb, last), 0)),
                     _resident((CONV_WIDTH, d)), _resident((1, d)), _resident((1, d)), _resident((1, d))]
        args += [mixed, mixed, mixed] + list(conv_params)
        scratch = [pltpu.VMEM((tm + 2 * CONV_HALO, d), F32), pltpu.VMEM((tm, d), F32)]
    else:
        in_specs += [pl.BlockSpec((1, tm, d_in), tok)]
        args += [mixed]
    in_specs += [pl.BlockSpec((1, 6, d), lambda bi, j: (bi, 0, 0)), _resident((d_in, d))]
    args += [mod, w_o]
    if has_bias:
        in_specs += [_resident((1, d))]
        args += [b_o]
    in_specs += [_resident((1, d)), _resident((d, d_ff)), _resident((d_ff, d))]
    args += [gain2, w_up, w_down]
    return pl.pallas_call(
        functools.partial(_post_kernel, conv=conv, has_bias=has_bias, d_ff_chunk=min(d_ff, 1024)),
        out_shape=jax.ShapeDtypeStruct((b, t, d), F32),
        grid=(b, t // tm),
        in_specs=in_specs,
        out_specs=pl.BlockSpec((1, tm, d), tok),
        scratch_shapes=scratch,
        compiler_params=_params(2),
        name="post_mlp",
    )(*args)


def _rope_tables(t):
    rows = t // GRID_W
    row = jnp.repeat(jnp.arange(rows, dtype=F32), GRID_W)
    col = jnp.tile(jnp.arange(GRID_W, dtype=F32), rows)
    half = HEAD_DIM // 2
    inv = 1.0 / jnp.power(ROPE_THETA, jnp.arange(0, half, 2, dtype=F32) / half)
    ang = jnp.concatenate([row[:, None] * inv, col[:, None] * inv], axis=-1)
    cos, sin = jnp.cos(ang), jnp.sin(ang)
    reps = LANES // HEAD_DIM
    return (jnp.tile(jnp.concatenate([cos, cos], axis=1), (1, reps)),
            jnp.tile(jnp.concatenate([-sin, sin], axis=1), (1, reps)))


def _split_pairs(a):
    lead = a.shape[:-1]
    a = a.reshape(lead + (-1, HEAD_DIM // 2, 2))
    return jnp.swapaxes(a, -1, -2).reshape(lead + (-1,))


def _prep_qkv(w_qkv, q_g, k_g, n_q, n_k):
    hq, hk = n_q // HEAD_DIM, n_k // HEAD_DIM
    w = _split_pairs(w_qkv[:, :n_q + n_k]).astype(BF16)
    wvt = w_qkv[:, n_q + n_k:].T.astype(BF16)
    scale = LOG2E / math.sqrt(HEAD_DIM)
    head_gain = jnp.concatenate([jnp.tile(_split_pairs(q_g) * scale, hq), jnp.tile(_split_pairs(k_g), hk)])
    return w, wvt, head_gain.reshape(1, n_q + n_k).astype(F32)


def _ones_block_diag():
    idx = jnp.arange(MXU_DIM) // HEAD_DIM
    return (idx[:, None] == idx[None, :]).astype(BF16)


def kernel(x, c, ctx, c_ctx, norm1_g, norm2_g, mod_w, mod_b, mlp_up, mlp_down,
           gqa_w_qkv, gqa_q_g, gqa_k_g, gqa_w_o,
           conv_w_pw1, conv_b_pw1, conv_w_dw, conv_b_dw, conv_ln_g, conv_ln_b, conv_w_pw2, conv_b_pw2,
           diff_w_qkv, diff_q_g, diff_k_g, diff_lam_q1, diff_lam_k1, diff_lam_q2, diff_lam_k2,
           diff_subln_g, diff_w_o,
           swa_w_qkv, swa_q_g, swa_k_g, swa_sink, swa_w_o):
    b, t, d = x.shape
    n_ctx = ctx.shape[1]
    depth = norm1_g.shape[0]
    assert depth == N_MIXERS and t % GRID_W == 0 and t % n_ctx == 0

    tm = _pick(t, (512, 256, 128))
    tq_win = _pick(t, (256, 128))
    tq_gqa = _pick(t, (1024, 512, 256, 128))
    tq_diff = _pick(t, (2048, 1024, 512, 256, 128))
    t_all = n_ctx + t
    tk = _pick(t_all, (768, 512, 256, 128))
    tkc = _pick(n_ctx, (768, 512, 256, 128))

    cv = jnp.concatenate([c, c_ctx[None], jnp.zeros((8 - b - 1, d), F32)], axis=0)
    mod_all = _mod_call(cv, mod_w, mod_b).reshape(depth, 8, 6, d)
    rope = _rope_tables(t)
    ones_bd = _ones_block_diag()
    row = lambda v: v.reshape(1, -1).astype(F32)

    xc = ctx
    for i in range(depth):
        m, j = i % N_MIXERS, i // N_MIXERS
        need_ctx = i < depth - 1
        mod_l = mod_all[i, :b]
        mod_c = jnp.broadcast_to(mod_all[i, b], (b, 6, d))
        g1, g2 = row(norm1_g[i]), row(norm2_g[i])
        w_up, w_down = mlp_up[i].astype(BF16), mlp_down[i].astype(BF16)
        conv_params = None
        b_o = None
        if m == 1:
            w1, b1 = conv_w_pw1[j].astype(BF16), row(conv_b_pw1[j])
            mixed = _glu_call(x, mod_l, g1, w1, b1, tm=tm)
            mixed_c = _glu_call(xc, mod_c, g1, w1, b1, tm=n_ctx) if need_ctx else None
            conv_params = (conv_w_dw[j].astype(F32), row(conv_b_dw[j]), row(conv_ln_g[j]), row(conv_ln_b[j]))
            w_o, b_o = conv_w_pw2[j].astype(BF16), row(conv_b_pw2[j])
        else:
            if m == 2:
                w_qkv, q_g, k_g, w_o = diff_w_qkv[j], diff_q_g[j], diff_k_g[j], diff_w_o[j]
                n_q = n_k = w_qkv.shape[1] // 3
                q_half = tuple(h % 2 for h in range(n_q // HEAD_DIM))
            else:
                w_qkv, q_g, k_g, w_o = ((gqa_w_qkv[j], gqa_q_g[j], gqa_k_g[j], gqa_w_o[j]) if m == 0 else
                                        (swa_w_qkv[j], swa_q_g[j], swa_k_g[j], swa_w_o[j]))
                n_q = w_o.shape[0]
                n_k = (w_qkv.shape[1] - n_q) // 2
                group = n_q // n_k
                q_half = tuple((h // group) % 2 for h in range(n_q // HEAD_DIM))
            w_o = w_o.astype(BF16)
            w_p, wvt, head_gain = _prep_qkv(w_qkv, q_g, k_g, n_q, n_k)
            proj = functools.partial(_proj_call, gain=g1, w=w_p, wvt=wvt, head_gain=head_gain,
                                     ones_bd=ones_bd, n_q=n_q, n_k=n_k, q_half=q_half)
            kv_buffers = (jnp.zeros((b, t_all, n_k), BF16), jnp.zeros((b, wvt.shape[0], t_all), BF16))
            q, *kv_buffers = proj(x, mod_l, rope=rope, tm=tm, kv_buffers=kv_buffers, first_row=0)
            qc, k_all, vt_all = proj(xc, mod_c, rope=None, tm=n_ctx, kv_buffers=kv_buffers, first_row=t)
            lat_keys = dict(tk=tk, key_rows=t_all, key_block=0)
            ctx_keys = dict(tk=tkc, key_rows=n_ctx, key_block=t // n_ctx)
            mixed_c = None
            if m == 0:
                mixed = _gqa_call(q, k_all, vt_all, group=group, tq=tq_gqa, **lat_keys)
                if need_ctx:
                    mixed_c = _gqa_call(qc, k_all, vt_all, group=group, tq=n_ctx, **ctx_keys)
            elif m == 2:
                lam_init = 0.8 - 0.6 * math.exp(-0.3 * i)
                lam_vecs = jnp.stack([diff_lam_q1[j], diff_lam_k1[j], diff_lam_q2[j], diff_lam_k2[j]]).astype(F32)
                sg = diff_subln_g[j].reshape(-1, 1).astype(F32)
                mixed = _diff_call(q, k_all, vt_all, lam_vecs, sg, lam_init=lam_init, tq=tq_diff, **lat_keys)
                if need_ctx:
                    mixed_c = _diff_call(qc, k_all, vt_all, lam_vecs, sg, lam_init=lam_init, tq=n_ctx, **ctx_keys)
            else:
                assert not need_ctx, "windowed layer with a context update is not supported"
                sink_rows = jnp.repeat(swa_sink[j].astype(F32).reshape(n_k // HEAD_DIM, 1, group), tq_win, axis=2)
                mixed = _window_call(q, k_all, vt_all, sink_rows, group=group, n_ctx=n_ctx, tq=tq_win)
        x = _post_call(x, mixed, mod_l, w_o, b_o, g2, w_up, w_down, conv_params, tm=tm)
        if need_ctx:
            xc = _post_call(xc, mixed_c, mod_c, w_o, b_o, g2, w_up, w_down, conv_params, tm=n_ctx)
    return x
```

```python
import functools
import math

import jax
import jax.numpy as jnp
from jax import lax
from jax.experimental import pallas as pl
from jax.experimental.pallas import tpu as pltpu

F32 = jnp.float32
BF16 = jnp.bfloat16

GRID_W = 64
HEAD_DIM = 64
N_MIXERS = 4
CONV_WIDTH = 31
WINDOW = 128
ROPE_THETA = 10000.0
EPS = 1e-6
NEG_INF = -1e30
LOG2E = math.log2(math.e)

LANES = 128
MXU_DIM = 256
VMEM_LIMIT_BYTES = 56 * 1024 * 1024

CONV_HALO = 16
CONV_ROWS = 64


def _params(n_axes):
    return pltpu.CompilerParams(dimension_semantics=("parallel",) * n_axes,
                                vmem_limit_bytes=VMEM_LIMIT_BYTES)


def _resident(shape):
    nd = len(shape)
    return pl.BlockSpec(shape, lambda *_: (0,) * nd, pipeline_mode=pl.Buffered(1))


def _pick(n, candidates):
    for cand in candidates:
        if n % cand == 0:
            return cand
    return n


def _modulate(x, gain, shift, scale):
    ms = jnp.mean(x * x, axis=-1, keepdims=True)
    return (x * lax.rsqrt(ms + EPS) * gain) * (1.0 + scale) + shift


def _dot(a, b):
    return jnp.dot(a, b, preferred_element_type=F32)


def _dot_nt(a, b):
    return lax.dot_general(a, b, (((1,), (1,)), ((), ())), preferred_element_type=F32)


def _mod_kernel(cv_ref, w_ref, b_ref, o_ref):
    cv = cv_ref[...]
    s = cv * jax.nn.sigmoid(cv)
    o_ref[0] = _dot(s.astype(BF16), w_ref[0].astype(BF16)) + b_ref[0]


def _mod_call(cv, mod_w, mod_b):
    depth, d, n = mod_w.shape
    rows = cv.shape[0]
    tn = _pick(n, (1536, 1024, 512))
    return pl.pallas_call(
        _mod_kernel,
        out_shape=jax.ShapeDtypeStruct((depth, rows, n), F32),
        grid=(depth, n // tn),
        in_specs=[pl.BlockSpec((rows, d), lambda i, j: (0, 0)),
                  pl.BlockSpec((1, d, tn), lambda i, j: (i, 0, j)),
                  pl.BlockSpec((1, 1, tn), lambda i, j: (i, 0, j))],
        out_specs=pl.BlockSpec((1, rows, tn), lambda i, j: (i, 0, j)),
        compiler_params=_params(2),
        name="modulation",
    )(cv, mod_w, mod_b.reshape(depth, 1, n))


def _proj_kernel(*refs, n_q, n_k, q_half, use_rope, n_alias):
    q_ref, k_ref, vt_ref = refs[-3:]
    refs = refs[:len(refs) - 3 - n_alias]
    if use_rope:
        (x_ref, mod_ref, g_ref, w_ref, wvt_ref, hg_ref, e_ref, cos_ref, sin_ref) = refs
    else:
        (x_ref, mod_ref, g_ref, w_ref, wvt_ref, hg_ref, e_ref) = refs
    x = x_ref[0]
    mod = mod_ref[0]
    tm = x.shape[0]
    h = _modulate(x, g_ref[...], mod[0:1], mod[1:2]).astype(BF16)
    qk = _dot(h, w_ref[...])
    vt_ref[0] = _dot_nt(wvt_ref[...], h).astype(BF16)
    n_qk = n_q + n_k
    sq = qk * qk
    hi = sq.astype(BF16)
    lo = (sq - hi.astype(F32)).astype(BF16)
    ones_bd = e_ref[...]
    ss = jnp.concatenate(
        [_dot(hi[:, t:t + MXU_DIM], ones_bd) + _dot(lo[:, t:t + MXU_DIM], ones_bd)
         for t in range(0, n_qk, MXU_DIM)], axis=1)
    qk = qk * lax.rsqrt(ss * (1.0 / HEAD_DIM) + EPS) * hg_ref[...]

    lane = lax.broadcasted_iota(jnp.int32, (tm, LANES), 1)
    tiles = [qk[:, t:t + LANES] for t in range(0, n_qk, LANES)]
    if use_rope:
        even = (lane % 2) == 0
        cos = cos_ref[...]
        sin = sin_ref[...]
        tiles = [t * cos + jnp.where(even, pltpu.roll(t, LANES - 1, axis=1), pltpu.roll(t, 1, axis=1)) * sin
                 for t in tiles]
    low = lane < HEAD_DIM
    for j in range(n_q // HEAD_DIM):
        t = tiles[j // 2]
        if j % 2 != q_half[j]:
            t = pltpu.roll(t, HEAD_DIM, axis=1)
        keep = low if q_half[j] == 0 else jnp.logical_not(low)
        q_ref[0, j] = jnp.where(keep, t, 0.0).astype(BF16)
    k_ref[0] = jnp.concatenate(tiles[n_q // LANES:], axis=1).astype(BF16)


def _proj_call(x, mod, gain, w, wvt, head_gain, ones_bd, rope, *, n_q, n_k, q_half, tm, kv_buffers,
               first_row):
    b, t, d = x.shape
    n_v = wvt.shape[0]
    t_keys = kv_buffers[0].shape[1]
    hq = n_q // HEAD_DIM
    use_rope = rope is not None
    in_specs = [pl.BlockSpec((1, tm, d), lambda bi, j: (bi, j, 0)),
                pl.BlockSpec((1, 6, d), lambda bi, j: (bi, 0, 0)),
                _resident((1, d)),
                _resident((d, n_q + n_k)),
                _resident((n_v, d)),
                _resident((1, n_q + n_k)),
                _resident((MXU_DIM, MXU_DIM))]
    args = [x, mod, gain, w, wvt, head_gain, ones_bd]
    if use_rope:
        in_specs += [pl.BlockSpec((tm, LANES), lambda bi, j: (j, 0)),
                     pl.BlockSpec((tm, LANES), lambda bi, j: (j, 0))]
        args += list(rope)
    off = first_row // tm
    assert off * tm == first_row and first_row + t <= t_keys
    aliases = {len(args): 1, len(args) + 1: 2}
    in_specs += [pl.BlockSpec(memory_space=pl.ANY), pl.BlockSpec(memory_space=pl.ANY)]
    args += list(kv_buffers)
    return pl.pallas_call(
        functools.partial(_proj_kernel, n_q=n_q, n_k=n_k, q_half=q_half, use_rope=use_rope,
                          n_alias=len(aliases)),
        out_shape=(jax.ShapeDtypeStruct((b, hq, t, LANES), BF16),
                   jax.ShapeDtypeStruct((b, t_keys, n_k), BF16),
                   jax.ShapeDtypeStruct((b, n_v, t_keys), BF16)),
        grid=(b, t // tm),
        in_specs=in_specs,
        out_specs=(pl.BlockSpec((1, hq, tm, LANES), lambda bi, j: (bi, 0, j, 0)),
                   pl.BlockSpec((1, tm, n_k), lambda bi, j: (bi, j + off, 0)),
                   pl.BlockSpec((1, n_v, tm), lambda bi, j: (bi, 0, j + off))),
        input_output_aliases=aliases,
        compiler_params=_params(2),
        name="attn_proj",
    )(*args)


ONES_ROWS = 16
FLASH_UNROLL = 4


def _chunk_rows(j, tk):
    return pl.ds(j * tk if isinstance(j, int) else pl.multiple_of(j * tk, tk), tk)


def _flash_loop(q_stacked, load_k, load_vt, qt_ref, s_ref, cm_ref, m_ref, acc_ref, *, n_kv, tk,
                mask_fn=None, sink=None):
    v_rows = acc_ref.shape[0] - ONES_ROWS
    qt_ref[...] = q_stacked.astype(F32).T.astype(BF16)
    acc_ref[0:v_rows, :] = jnp.zeros((v_rows, acc_ref.shape[1]), F32)
    if sink is None:
        m_ref[...] = jnp.full(m_ref.shape, NEG_INF, F32)
        acc_ref[v_rows:, :] = jnp.zeros((ONES_ROWS, acc_ref.shape[1]), F32)
    else:
        m_ref[...] = sink
        acc_ref[v_rows:, :] = jnp.ones((ONES_ROWS, acc_ref.shape[1]), F32)
    ones = jnp.ones((ONES_ROWS, tk), BF16)
    m_cols = s_ref.shape[2]
    col_blocks = [slice(c0, c0 + MXU_DIM) for c0 in range(0, m_cols, MXU_DIM)]

    def produce(kc, keep, slot, cols):
        s = _dot(kc, qt_ref[:, cols])
        if keep is not None:
            s = jnp.where(keep, s, NEG_INF)
        s_ref[slot, :, cols] = s
        cm_ref[slot, :, cols] = jnp.max(s.reshape(tk // 8, 8, MXU_DIM), axis=0)

    def consume(vt1, slot, cols):
        m_prev = m_ref[:, cols]
        m_new = jnp.maximum(m_prev, jnp.max(cm_ref[slot, :, cols], axis=0, keepdims=True))
        alpha = jnp.exp2(m_prev - m_new)
        p = jnp.exp2(s_ref[slot, :, cols] - m_new).astype(BF16)
        acc_ref[:, cols] = alpha * acc_ref[:, cols] + _dot(vt1, p)
        m_ref[:, cols] = m_new

    def step(j_prod, slot_prod, j_cons, slot_cons):
        if j_prod is not None:
            kc = load_k(j_prod)
            keep = mask_fn(j_prod) if mask_fn is not None else None
        if j_cons is not None:
            vt1 = jnp.concatenate([load_vt(j_cons), ones], axis=0)
        for cols in col_blocks:
            if j_prod is not None:
                produce(kc, keep, slot_prod, cols)
            if j_cons is not None:
                consume(vt1, slot_cons, cols)

    step(0, 0, None, None)
    trips = (n_kv - 1) // FLASH_UNROLL if n_kv - 1 > FLASH_UNROLL else 0

    def body(trip, carry):
        j0 = trip * FLASH_UNROLL
        for u in range(FLASH_UNROLL):
            step(j0 + u + 1, (u + 1) % 2, j0 + u, u % 2)
        return carry

    if trips:
        lax.fori_loop(0, trips, body, 0)
    for j in range(trips * FLASH_UNROLL, n_kv - 1):
        step(j + 1, (j + 1) % 2, j, j % 2)
    step(None, None, n_kv - 1, (n_kv - 1) % 2)


def _store_heads_t(o_ref, ot, tq, group):
    for gg in range(group // 2):
        slab = jnp.concatenate([ot[:, (2 * gg) * tq:(2 * gg + 1) * tq],
                                ot[:, (2 * gg + 1) * tq:(2 * gg + 2) * tq]], axis=0)
        o_ref[0, :, gg * LANES:(gg + 1) * LANES] = slab.T.astype(o_ref.dtype)


def _kv_specs(key_rows, key_block, v_rows, kv_per_tile):
    return [pl.BlockSpec((1, key_rows, LANES), lambda bi, h, i: (bi, key_block, h // kv_per_tile)),
            pl.BlockSpec((1, v_rows, key_rows), lambda bi, h, i: (bi, h, key_block))]


def _flash_scratch(m_cols, tk, v_rows):
    return [pltpu.VMEM((LANES, m_cols), BF16),
            pltpu.VMEM((2, tk, m_cols), F32),
            pltpu.VMEM((2, 8, m_cols), F32),
            pltpu.VMEM((1, m_cols), F32),
            pltpu.VMEM((v_rows + ONES_ROWS, m_cols), F32)]


def _gqa_kernel(q_ref, k_ref, vt_ref, o_ref, qt_ref, s_ref, cm_ref, m_ref, acc_ref, *, n_kv, tk):
    group, tq = q_ref.shape[1:3]
    q_stacked = q_ref[0].reshape(group * tq, LANES)
    load_k = lambda j: k_ref[0, _chunk_rows(j, tk), :]
    load_vt = lambda j: vt_ref[0, :, _chunk_rows(j, tk)]
    _flash_loop(q_stacked, load_k, load_vt, qt_ref, s_ref, cm_ref, m_ref, acc_ref, n_kv=n_kv, tk=tk)
    ot = acc_ref[0:HEAD_DIM, :] / acc_ref[HEAD_DIM:HEAD_DIM + 1, :]
    _store_heads_t(o_ref, ot, tq, group)


def _gqa_call(q, k_all, vt_all, *, group, tq, tk, key_rows, key_block):
    b, hq, t, _ = q.shape
    return pl.pallas_call(
        functools.partial(_gqa_kernel, n_kv=key_rows // tk, tk=tk),
        out_shape=jax.ShapeDtypeStruct((b, t, hq * HEAD_DIM), BF16),
        grid=(b, hq // group, t // tq),
        in_specs=[pl.BlockSpec((1, group, tq, LANES), lambda bi, h, i: (bi, h, i, 0))]
                 + _kv_specs(key_rows, key_block, HEAD_DIM, LANES // HEAD_DIM),
        out_specs=pl.BlockSpec((1, tq, group * HEAD_DIM), lambda bi, h, i: (bi, i, h)),
        scratch_shapes=_flash_scratch(group * tq, tk, HEAD_DIM),
        compiler_params=_params(3),
        name="gqa_attention",
    )(q, k_all, vt_all)


def _diff_kernel(q_ref, k_ref, vt_ref, lam_ref, sg_ref, o_ref, qt_ref, s_ref, cm_ref, m_ref, acc_ref, *,
                 lam_init, n_kv, tk):
    tq = q_ref.shape[2]
    dv = vt_ref.shape[1]
    q_stacked = q_ref[0].reshape(2 * tq, LANES)
    load_k = lambda j: k_ref[0, _chunk_rows(j, tk), :]
    load_vt = lambda j: vt_ref[0, :, _chunk_rows(j, tk)]
    _flash_loop(q_stacked, load_k, load_vt, qt_ref, s_ref, cm_ref, m_ref, acc_ref, n_kv=n_kv, tk=tk)
    ot = acc_ref[0:dv, :] / acc_ref[dv:dv + 1, :]
    lv = lam_ref[...]
    lam = (jnp.exp(jnp.sum(lv[0:1] * lv[1:2], axis=1, keepdims=True))
           - jnp.exp(jnp.sum(lv[2:3] * lv[3:4], axis=1, keepdims=True)) + lam_init)
    ot = ot[:, :tq] - lam * ot[:, tq:]
    ms = jnp.mean(ot * ot, axis=0, keepdims=True)
    ot = (ot * lax.rsqrt(ms + EPS) * sg_ref[...]) * (1.0 - lam_init)
    o_ref[0] = ot.T.astype(o_ref.dtype)


def _diff_call(q, k_all, vt_all, lam_vecs, subln_g, *, lam_init, tq, tk, key_rows, key_block):
    b, hq, t, _ = q.shape
    heads = k_all.shape[2] // LANES
    dv = vt_all.shape[1] // heads
    return pl.pallas_call(
        functools.partial(_diff_kernel, lam_init=lam_init, n_kv=key_rows // tk, tk=tk),
        out_shape=jax.ShapeDtypeStruct((b, t, heads * dv), BF16),
        grid=(b, heads, t // tq),
        in_specs=[pl.BlockSpec((1, 2, tq, LANES), lambda bi, h, i: (bi, h, i, 0))]
                 + _kv_specs(key_rows, key_block, dv, 1)
                 + [pl.BlockSpec((4, HEAD_DIM), lambda bi, h, i: (0, 0)),
                    pl.BlockSpec((dv, 1), lambda bi, h, i: (0, 0))],
        out_specs=pl.BlockSpec((1, tq, dv), lambda bi, h, i: (bi, i, h)),
        scratch_shapes=_flash_scratch(2 * tq, tk, dv),
        compiler_params=_params(3),
        name="diff_attention",
    )(q, k_all, vt_all, lam_vecs, subln_g)


def _window_kernel(q_ref, k_ref, vt_ref, sink_ref, o_ref, qt_ref, s_ref, cm_ref, m_ref, acc_ref, *,
                   n_ctx, t_lat, tk):
    group, tq = q_ref.shape[1:3]
    assert tq == MXU_DIM
    span = tq + 2 * WINDOW
    n_lat = span // tk
    i = pl.program_id(2)
    start = pl.multiple_of(jnp.clip(i * tq - WINDOW, 0, t_lat - span), WINDOW)
    first_row = lambda j: (pl.multiple_of(start + j * tk, WINDOW) if j < n_lat
                           else t_lat + (j - n_lat) * tk)
    load_k = lambda j: k_ref[0, pl.ds(first_row(j), tk), :]
    load_vt = lambda j: vt_ref[0, :, pl.ds(first_row(j), tk)]
    rel = (lax.broadcasted_iota(jnp.int32, (tk, tq), 0) - lax.broadcasted_iota(jnp.int32, (tk, tq), 1)
           + (start - i * tq))

    def mask_fn(j):
        return jnp.abs(rel + j * tk) <= WINDOW if j < n_lat else None

    q_stacked = q_ref[0].reshape(group * tq, LANES)
    _flash_loop(q_stacked, load_k, load_vt, qt_ref, s_ref, cm_ref, m_ref, acc_ref,
                n_kv=n_lat + n_ctx // tk, tk=tk, mask_fn=mask_fn, sink=sink_ref[0] * LOG2E)
    ot = acc_ref[0:HEAD_DIM, :] / acc_ref[HEAD_DIM:HEAD_DIM + 1, :]
    _store_heads_t(o_ref, ot, tq, group)


def _window_call(q, k_all, vt_all, sink_rows, *, group, n_ctx, tq):
    b, hq, t, _ = q.shape
    t_all = k_all.shape[1]
    tk = MXU_DIM
    n_chunks = (tq + 2 * WINDOW) // tk + n_ctx // tk
    assert (tq + 2 * WINDOW) % tk == 0 and n_ctx % tk == 0 and n_chunks - 1 <= FLASH_UNROLL
    return pl.pallas_call(
        functools.partial(_window_kernel, n_ctx=n_ctx, t_lat=t, tk=tk),
        out_shape=jax.ShapeDtypeStruct((b, t, hq * HEAD_DIM), BF16),
        grid=(b, hq // group, t // tq),
        in_specs=[pl.BlockSpec((1, group, tq, LANES), lambda bi, h, i: (bi, h, i, 0))]
                 + _kv_specs(t_all, 0, HEAD_DIM, LANES // HEAD_DIM)
                 + [pl.BlockSpec((1, 1, group * tq), lambda bi, h, i: (h, 0, 0))],
        out_specs=pl.BlockSpec((1, tq, group * HEAD_DIM), lambda bi, h, i: (bi, i, h)),
        scratch_shapes=_flash_scratch(group * tq, tk, HEAD_DIM),
        compiler_params=_params(3),
        name="window_attention",
    )(q, k_all, vt_all, sink_rows)


def _glu_kernel(x_ref, mod_ref, g_ref, w_ref, b_ref, u_ref):
    x = x_ref[0]
    mod = mod_ref[0]
    d = x.shape[1]
    h = _modulate(x, g_ref[...], mod[0:1], mod[1:2]).astype(BF16)
    ag = _dot(h, w_ref[...]) + b_ref[...]
    u_ref[0] = (ag[:, :d] * jax.nn.sigmoid(ag[:, d:])).astype(u_ref.dtype)


def _glu_call(x, mod, gain, w, bias, *, tm):
    b, t, d = x.shape
    return pl.pallas_call(
        _glu_kernel,
        out_shape=jax.ShapeDtypeStruct((b, t, d), BF16),
        grid=(b, t // tm),
        in_specs=[pl.BlockSpec((1, tm, d), lambda bi, j: (bi, j, 0)),
                  pl.BlockSpec((1, 6, d), lambda bi, j: (bi, 0, 0)),
                  _resident((1, d)),
                  _resident((d, 2 * d)),
                  _resident((1, 2 * d))],
        out_specs=pl.BlockSpec((1, tm, d), lambda bi, j: (bi, j, 0)),
        compiler_params=_params(2),
        name="conv_glu",
    )(x, mod, gain, w, bias)


def _depthwise_conv(ubuf_ref, cbuf_ref, wdw_ref, bdw_ref, tm, d):
    win_rows = CONV_ROWS + 2 * CONV_HALO
    base = CONV_HALO - CONV_WIDTH // 2
    lane_chunk = LANES

    def step(r, carry):
        r0 = pl.multiple_of(r * CONV_ROWS, CONV_ROWS)
        for c0 in range(0, d, lane_chunk):
            win = ubuf_ref[pl.ds(r0, win_rows), c0:c0 + lane_chunk]
            acc = jnp.zeros((CONV_ROWS, lane_chunk), F32) + bdw_ref[:, c0:c0 + lane_chunk]
            for sub in range(8):
                shifted = win if sub == 0 else pltpu.roll(win, win_rows - sub, axis=0)
                for al in range(0, win_rows - CONV_ROWS + 1, 8):
                    k = al + sub - base
                    if 0 <= k < CONV_WIDTH:
                        acc = acc + wdw_ref[k:k + 1, c0:c0 + lane_chunk] * shifted[al:al + CONV_ROWS]
            cbuf_ref[pl.ds(r0, CONV_ROWS), c0:c0 + lane_chunk] = acc
        return carry

    lax.fori_loop(0, tm // CONV_ROWS, step, 0)


def _post_kernel(*refs, conv, has_bias, d_ff_chunk):
    it = iter(refs)
    x_ref = next(it)
    if conv:
        up_ref, uc_ref, un_ref = next(it), next(it), next(it)
        wdw_ref, bdw_ref, lng_ref, lnb_ref = next(it), next(it), next(it), next(it)
    else:
        o_ref = next(it)
    mod_ref, wo_ref = next(it), next(it)
    bo_ref = next(it) if has_bias else None
    g2_ref, wup_ref, wdn_ref, out_ref = next(it), next(it), next(it), next(it)
    if conv:
        ubuf_ref, cbuf_ref = next(it), next(it)

    x = x_ref[0]
    mod = mod_ref[0]
    tm, d = x.shape
    if conv:
        j = pl.program_id(1)
        prev = jnp.where(j > 0, up_ref[0].astype(F32), 0.0)
        nxt = jnp.where(j < pl.num_programs(1) - 1, un_ref[0].astype(F32), 0.0)
        ubuf_ref[0:CONV_HALO] = prev
        ubuf_ref[CONV_HALO:CONV_HALO + tm] = uc_ref[0].astype(F32)
        ubuf_ref[CONV_HALO + tm:] = nxt
        _depthwise_conv(ubuf_ref, cbuf_ref, wdw_ref, bdw_ref, tm, d)
        cv = cbuf_ref[...]
        mu = jnp.mean(cv, axis=-1, keepdims=True)
        cc = cv - mu
        var = jnp.mean(cc * cc, axis=-1, keepdims=True)
        yn = cc * lax.rsqrt(var + EPS) * lng_ref[...] + lnb_ref[...]
        o = (yn * jax.nn.sigmoid(yn)).astype(BF16)
    else:
        o = o_ref[0]
    y = _dot(o, wo_ref[...])
    if has_bias:
        y = y + bo_ref[...]
    x1 = x + mod[2:3] * y
    h2 = _modulate(x1, g2_ref[...], mod[3:4], mod[4:5]).astype(BF16)
    acc = jnp.zeros((tm, d), F32)
    d_ff = wup_ref.shape[1]
    for c0 in range(0, d_ff, d_ff_chunk):
        up = jnp.maximum(_dot(h2, wup_ref[:, c0:c0 + d_ff_chunk]), 0.0)
        acc = acc + _dot((up * up).astype(BF16), wdn_ref[c0:c0 + d_ff_chunk, :])
    out_ref[0] = x1 + mod[5:6] * acc


def _post_call(x, mixed, mod, w_o, b_o, gain2, w_up, w_down, conv_params, *, tm):
    b, t, d = x.shape
    d_in = w_o.shape[0]
    d_ff = w_up.shape[1]
    conv = conv_params is not None
    has_bias = b_o is not None
    tok = lambda bi, j: (bi, j, 0)
    in_specs = [pl.BlockSpec((1, tm, d), tok)]
    args = [x]
    scratch = []
    if conv:
        hb = tm // CONV_HALO
        last = t // CONV_HALO - 1
        in_specs += [pl.BlockSpec((1, CONV_HALO, d), lambda bi, j: (bi, jnp.maximum(j * hb - 1, 0), 0)),
                     pl.BlockSpec((1, tm, d), tok),
                     pl.BlockSpec((1, CONV_HALO, d), lambda bi, j: (bi, jnp.minimum((j + 1) * hb, last), 0)),
                     _resident((CONV_WIDTH, d)), _resident((1, d)), _resident((1, d)), _resident((1, d))]
        args += [mixed, mixed, mixed] + list(conv_params)
        scratch = [pltpu.VMEM((tm + 2 * CONV_HALO, d), F32), pltpu.VMEM((tm, d), F32)]
    else:
        in_specs += [pl.BlockSpec((1, tm, d_in), tok)]
        args += [mixed]
    in_specs += [pl.BlockSpec((1, 6, d), lambda bi, j: (bi, 0, 0)), _resident((d_in, d))]
    args += [mod, w_o]
    if has_bias:
        in_specs += [_resident((1, d))]
        args += [b_o]
    in_specs += [_resident((1, d)), _resident((d, d_ff)), _resident((d_ff, d))]
    args += [gain2, w_up, w_down]
    return pl.pallas_call(
        functools.partial(_post_kernel, conv=conv, has_bias=has_bias, d_ff_chunk=min(d_ff, 1024)),
        out_shape=jax.ShapeDtypeStruct((b, t, d), F32),
        grid=(b, t // tm),
        in_specs=in_specs,
        out_specs=pl.BlockSpec((1, tm, d), tok),
        scratch_shapes=scratch,
        compiler_params=_params(2),
        name="post_mlp",
    )(*args)


def _rope_tables(t):
    rows = t // GRID_W
    row = jnp.repeat(jnp.arange(rows, dtype=F32), GRID_W)
    col = jnp.tile(jnp.arange(GRID_W, dtype=F32), rows)
    half = HEAD_DIM // 2
    inv = 1.0 / jnp.power(ROPE_THETA, jnp.arange(0, half, 2, dtype=F32) / half)
    ang = jnp.concatenate([row[:, None] * inv, col[:, None] * inv], axis=-1)
    cos = jnp.repeat(jnp.cos(ang), 2, axis=1)
    sin = jnp.repeat(jnp.sin(ang), 2, axis=1) * jnp.tile(jnp.array([-1.0, 1.0], F32), half)
    reps = LANES // HEAD_DIM
    return jnp.tile(cos, (1, reps)), jnp.tile(sin, (1, reps))


def _prep_qkv(w_qkv, q_g, k_g, n_q, n_k):
    hq, hk = n_q // HEAD_DIM, n_k // HEAD_DIM
    w = w_qkv[:, :n_q + n_k].astype(BF16)
    wvt = w_qkv[:, n_q + n_k:].T.astype(BF16)
    scale = LOG2E / math.sqrt(HEAD_DIM)
    head_gain = jnp.concatenate([jnp.tile(q_g * scale, hq), jnp.tile(k_g, hk)])
    return w, wvt, head_gain.reshape(1, n_q + n_k).astype(F32)


def _ones_block_diag():
    idx = jnp.arange(MXU_DIM) // HEAD_DIM
    return (idx[:, None] == idx[None, :]).astype(BF16)


def kernel(x, c, ctx, c_ctx, norm1_g, norm2_g, mod_w, mod_b, mlp_up, mlp_down,
           gqa_w_qkv, gqa_q_g, gqa_k_g, gqa_w_o,
           conv_w_pw1, conv_b_pw1, conv_w_dw, conv_b_dw, conv_ln_g, conv_ln_b, conv_w_pw2, conv_b_pw2,
           diff_w_qkv, diff_q_g, diff_k_g, diff_lam_q1, diff_lam_k1, diff_lam_q2, diff_lam_k2,
           diff_subln_g, diff_w_o,
           swa_w_qkv, swa_q_g, swa_k_g, swa_sink, swa_w_o):
    b, t, d = x.shape
    n_ctx = ctx.shape[1]
    depth = norm1_g.shape[0]
    assert depth == N_MIXERS and t % GRID_W == 0 and t % n_ctx == 0

    tm = _pick(t, (512, 256, 128))
    tq_win = _pick(t, (256, 128))
    tq_gqa = _pick(t, (1024, 512, 256, 128))
    tq_diff = _pick(t, (2048, 1024, 512, 256, 128))
    t_all = n_ctx + t
    tk = _pick(t_all, (768, 512, 256, 128))
    tkc = _pick(n_ctx, (768, 512, 256, 128))

    cv = jnp.concatenate([c, c_ctx[None], jnp.zeros((8 - b - 1, d), F32)], axis=0)
    mod_all = _mod_call(cv, mod_w, mod_b).reshape(depth, 8, 6, d)
    rope = _rope_tables(t)
    ones_bd = _ones_block_diag()
    row = lambda v: v.reshape(1, -1).astype(F32)

    xc = ctx
    for i in range(depth):
        m, j = i % N_MIXERS, i // N_MIXERS
        need_ctx = i < depth - 1
        mod_l = mod_all[i, :b]
        mod_c = jnp.broadcast_to(mod_all[i, b], (b, 6, d))
        g1, g2 = row(norm1_g[i]), row(norm2_g[i])
        w_up, w_down = mlp_up[i].astype(BF16), mlp_down[i].astype(BF16)
        conv_params = None
        b_o = None
        if m == 1:
            w1, b1 = conv_w_pw1[j].astype(BF16), row(conv_b_pw1[j])
            mixed = _glu_call(x, mod_l, g1, w1, b1, tm=tm)
            mixed_c = _glu_call(xc, mod_c, g1, w1, b1, tm=n_ctx) if need_ctx else None
            conv_params = (conv_w_dw[j].astype(F32), row(conv_b_dw[j]), row(conv_ln_g[j]), row(conv_ln_b[j]))
            w_o, b_o = conv_w_pw2[j].astype(BF16), row(conv_b_pw2[j])
        else:
            if m == 2:
                w_qkv, q_g, k_g, w_o = diff_w_qkv[j], diff_q_g[j], diff_k_g[j], diff_w_o[j]
                n_q = n_k = w_qkv.shape[1] // 3
                q_half = tuple(h % 2 for h in range(n_q // HEAD_DIM))
            else:
                w_qkv, q_g, k_g, w_o = ((gqa_w_qkv[j], gqa_q_g[j], gqa_k_g[j], gqa_w_o[j]) if m == 0 else
                                        (swa_w_qkv[j], swa_q_g[j], swa_k_g[j], swa_w_o[j]))
                n_q = w_o.shape[0]
                n_k = (w_qkv.shape[1] - n_q) // 2
                group = n_q // n_k
                q_half = tuple((h // group) % 2 for h in range(n_q // HEAD_DIM))
            w_o = w_o.astype(BF16)
            w_p, wvt, head_gain = _prep_qkv(w_qkv, q_g, k_g, n_q, n_k)
            proj = functools.partial(_proj_call, gain=g1, w=w_p, wvt=wvt, head_gain=head_gain,
                                     ones_bd=ones_bd, n_q=n_q, n_k=n_k, q_half=q_half)
            kv_buffers = (jnp.zeros((b, t_all, n_k), BF16), jnp.zeros((b, wvt.shape[0], t_all), BF16))
            q, *kv_buffers = proj(x, mod_l, rope=rope, tm=tm, kv_buffers=kv_buffers, first_row=0)
            qc, k_all, vt_all = proj(xc, mod_c, rope=None, tm=n_ctx, kv_buffers=kv_buffers, first_row=t)
            lat_keys = dict(tk=tk, key_rows=t_all, key_block=0)
            ctx_keys = dict(tk=tkc, key_rows=n_ctx, key_block=t // n_ctx)
            mixed_c = None
            if m == 0:
                mixed = _gqa_call(q, k_all, vt_all, group=group, tq=tq_gqa, **lat_keys)
                if need_ctx:
                    mixed_c = _gqa_call(qc, k_all, vt_all, group=group, tq=n_ctx, **ctx_keys)
            elif m == 2:
                lam_init = 0.8 - 0.6 * math.exp(-0.3 * i)
                lam_vecs = jnp.stack([diff_lam_q1[j], diff_lam_k1[j], diff_lam_q2[j], diff_lam_k2[j]]).astype(F32)
                sg = diff_subln_g[j].reshape(-1, 1).astype(F32)
                mixed = _diff_call(q, k_all, vt_all, lam_vecs, sg, lam_init=lam_init, tq=tq_diff, **lat_keys)
                if need_ctx:
                    mixed_c = _diff_call(qc, k_all, vt_all, lam_vecs, sg, lam_init=lam_init, tq=n_ctx, **ctx_keys)
            else:
                assert not need_ctx, "windowed layer with a context update is not supported"
                sink_rows = jnp.repeat(swa_sink[j].astype(F32).reshape(n_k // HEAD_DIM, 1, group), tq_win, axis=2)
                mixed = _window_call(q, k_all, vt_all, sink_rows, group=group, n_ctx=n_ctx, tq=tq_win)
        x = _post_call(x, mixed, mod_l, w_o, b_o, g2, w_up, w_down, conv_params, tm=tm)
        if need_ctx:
            xc = _post_call(xc, mixed_c, mod_c, w_o, b_o, g2, w_up, w_down, conv_params, tm=n_ctx)
    return x
```

```python
import functools
import math

import jax
import jax.numpy as jnp
from jax import lax
from jax.experimental import pallas as pl
from jax.experimental.pallas import tpu as pltpu

F32 = jnp.float32
BF16 = jnp.bfloat16

GRID_W = 64
HEAD_DIM = 64
N_MIXERS = 4
CONV_WIDTH = 31
WINDOW = 128
ROPE_THETA = 10000.0
EPS = 1e-6
NEG_INF = -1e30
LOG2E = math.log2(math.e)

LANES = 128
MXU_DIM = 256
VMEM_LIMIT_BYTES = 56 * 1024 * 1024

CONV_HALO = 16
CONV_ROWS = 64


def _params(n_axes):
    return pltpu.CompilerParams(dimension_semantics=("parallel",) * n_axes,
                                vmem_limit_bytes=VMEM_LIMIT_BYTES)


def _resident(shape):
    nd = len(shape)
    return pl.BlockSpec(shape, lambda *_: (0,) * nd, pipeline_mode=pl.Buffered(1))


def _pick(n, candidates):
    for cand in candidates:
        if n % cand == 0:
            return cand
    return n


def _modulate(x, gain, shift, scale):
    ms = jnp.mean(x * x, axis=-1, keepdims=True)
    return (x * lax.rsqrt(ms + EPS) * gain) * (1.0 + scale) + shift


def _dot(a, b):
    return jnp.dot(a, b, preferred_element_type=F32)


def _dot_nt(a, b):
    return lax.dot_general(a, b, (((1,), (1,)), ((), ())), preferred_element_type=F32)


def _mod_kernel(cv_ref, w_ref, b_ref, o_ref):
    cv = cv_ref[...]
    s = cv * jax.nn.sigmoid(cv)
    o_ref[0] = _dot(s.astype(BF16), w_ref[0].astype(BF16)) + b_ref[0]


def _mod_call(cv, mod_w, mod_b):
    depth, d, n = mod_w.shape
    rows = cv.shape[0]
    tn = _pick(n, (1536, 1024, 512))
    return pl.pallas_call(
        _mod_kernel,
        out_shape=jax.ShapeDtypeStruct((depth, rows, n), F32),
        grid=(depth, n // tn),
        in_specs=[pl.BlockSpec((rows, d), lambda i, j: (0, 0)),
                  pl.BlockSpec((1, d, tn), lambda i, j: (i, 0, j)),
                  pl.BlockSpec((1, 1, tn), lambda i, j: (i, 0, j))],
        out_specs=pl.BlockSpec((1, rows, tn), lambda i, j: (i, 0, j)),
        compiler_params=_params(2),
        name="modulation",
    )(cv, mod_w, mod_b.reshape(depth, 1, n))


def _proj_kernel(*refs, n_q, n_k, q_half, use_rope, n_alias):
    q_ref, k_ref, vt_ref = refs[-3:]
    refs = refs[:len(refs) - 3 - n_alias]
    if use_rope:
        (x_ref, mod_ref, g_ref, w_ref, wvt_ref, hg_ref, e_ref, cos_ref, sin_ref) = refs
    else:
        (x_ref, mod_ref, g_ref, w_ref, wvt_ref, hg_ref, e_ref) = refs
    mod = mod_ref[0]
    tm_full = x_ref.shape[1]
    n_qk = n_q + n_k
    ones_bd = e_ref[...]
    n_parts = 2 if tm_full % (2 * MXU_DIM) == 0 else 1
    tm = tm_full // n_parts
    lane = lax.broadcasted_iota(jnp.int32, (tm, LANES), 1)
    even = (lane % 2) == 0
    low = lane < HEAD_DIM
    for part in range(n_parts):
        rows = slice(part * tm, (part + 1) * tm)
        h = _modulate(x_ref[0, rows, :], g_ref[...], mod[0:1], mod[1:2]).astype(BF16)
        qk = _dot(h, w_ref[...])
        vt_ref[0, :, rows] = _dot_nt(wvt_ref[...], h).astype(BF16)
        sq = qk * qk
        hi = sq.astype(BF16)
        lo = (sq - hi.astype(F32)).astype(BF16)
        ss = jnp.concatenate(
            [_dot(hi[:, t:t + MXU_DIM], ones_bd) + _dot(lo[:, t:t + MXU_DIM], ones_bd)
             for t in range(0, n_qk, MXU_DIM)], axis=1)
        qk = qk * lax.rsqrt(ss * (1.0 / HEAD_DIM) + EPS) * hg_ref[...]
        tiles = [qk[:, t:t + LANES] for t in range(0, n_qk, LANES)]
        if use_rope:
            cos = cos_ref[rows, :]
            sin = sin_ref[rows, :]
            tiles = [t * cos + jnp.where(even, pltpu.roll(t, LANES - 1, axis=1), pltpu.roll(t, 1, axis=1)) * sin
                     for t in tiles]
        for j in range(n_q // HEAD_DIM):
            t = tiles[j // 2]
            if j % 2 != q_half[j]:
                t = pltpu.roll(t, HEAD_DIM, axis=1)
            keep = low if q_half[j] == 0 else jnp.logical_not(low)
            q_ref[0, j, rows, :] = jnp.where(keep, t, 0.0).astype(BF16)
        k_ref[0, rows, :] = jnp.concatenate(tiles[n_q // LANES:], axis=1).astype(BF16)


def _proj_call(x, mod, gain, w, wvt, head_gain, ones_bd, rope, *, n_q, n_k, q_half, tm, kv_buffers,
               first_row):
    b, t, d = x.shape
    n_v = wvt.shape[0]
    t_keys = kv_buffers[0].shape[1]
    hq = n_q // HEAD_DIM
    use_rope = rope is not None
    in_specs = [pl.BlockSpec((1, tm, d), lambda bi, j: (bi, j, 0)),
                pl.BlockSpec((1, 6, d), lambda bi, j: (bi, 0, 0)),
                _resident((1, d)),
                _resident((d, n_q + n_k)),
                _resident((n_v, d)),
                _resident((1, n_q + n_k)),
                _resident((MXU_DIM, MXU_DIM))]
    args = [x, mod, gain, w, wvt, head_gain, ones_bd]
    if use_rope:
        in_specs += [pl.BlockSpec((tm, LANES), lambda bi, j: (j, 0)),
                     pl.BlockSpec((tm, LANES), lambda bi, j: (j, 0))]
        args += list(rope)
    off = first_row // tm
    assert off * tm == first_row and first_row + t <= t_keys
    aliases = {len(args): 1, len(args) + 1: 2}
    in_specs += [pl.BlockSpec(memory_space=pl.ANY), pl.BlockSpec(memory_space=pl.ANY)]
    args += list(kv_buffers)
    return pl.pallas_call(
        functools.partial(_proj_kernel, n_q=n_q, n_k=n_k, q_half=q_half, use_rope=use_rope,
                          n_alias=len(aliases)),
        out_shape=(jax.ShapeDtypeStruct((b, hq, t, LANES), BF16),
                   jax.ShapeDtypeStruct((b, t_keys, n_k), BF16),
                   jax.ShapeDtypeStruct((b, n_v, t_keys), BF16)),
        grid=(b, t // tm),
        in_specs=in_specs,
        out_specs=(pl.BlockSpec((1, hq, tm, LANES), lambda bi, j: (bi, 0, j, 0)),
                   pl.BlockSpec((1, tm, n_k), lambda bi, j: (bi, j + off, 0)),
                   pl.BlockSpec((1, n_v, tm), lambda bi, j: (bi, 0, j + off))),
        input_output_aliases=aliases,
        compiler_params=_params(2),
        name="attn_proj",
    )(*args)


ONES_ROWS = 16
FLASH_UNROLL = 4


def _chunk_rows(j, tk):
    return pl.ds(j * tk if isinstance(j, int) else pl.multiple_of(j * tk, tk), tk)


def _flash_loop(q_stacked, load_k, load_vt, qt_ref, s_ref, cm_ref, m_ref, acc_ref, *, n_kv, tk,
                mask_fn=None, sink=None):
    v_rows = acc_ref.shape[0] - ONES_ROWS
    qt_ref[...] = q_stacked.astype(F32).T.astype(BF16)
    acc_ref[0:v_rows, :] = jnp.zeros((v_rows, acc_ref.shape[1]), F32)
    if sink is None:
        m_ref[...] = jnp.full(m_ref.shape, NEG_INF, F32)
        acc_ref[v_rows:, :] = jnp.zeros((ONES_ROWS, acc_ref.shape[1]), F32)
    else:
        m_ref[...] = sink
        acc_ref[v_rows:, :] = jnp.ones((ONES_ROWS, acc_ref.shape[1]), F32)
    ones = jnp.ones((ONES_ROWS, tk), BF16)
    m_cols = s_ref.shape[2]
    col_blocks = [slice(c0, c0 + MXU_DIM) for c0 in range(0, m_cols, MXU_DIM)]

    def produce(kc, keep, slot, cols):
        s = _dot(kc, qt_ref[:, cols])
        if keep is not None:
            s = jnp.where(keep, s, NEG_INF)
        s_ref[slot, :, cols] = s
        cm_ref[slot, :, cols] = jnp.max(s.reshape(tk // 8, 8, MXU_DIM), axis=0)

    def consume(vt1, slot, cols):
        m_prev = m_ref[:, cols]
        m_new = jnp.maximum(m_prev, jnp.max(cm_ref[slot, :, cols], axis=0, keepdims=True))
        alpha = jnp.exp2(m_prev - m_new)
        p = jnp.exp2(s_ref[slot, :, cols] - m_new).astype(BF16)
        acc_ref[:, cols] = alpha * acc_ref[:, cols] + _dot(vt1, p)
        m_ref[:, cols] = m_new

    def step(j_prod, slot_prod, j_cons, slot_cons):
        if j_prod is not None:
            kc = load_k(j_prod)
            keep = mask_fn(j_prod) if mask_fn is not None else None
        if j_cons is not None:
            vt1 = jnp.concatenate([load_vt(j_cons), ones], axis=0)
        for cols in col_blocks:
            if j_prod is not None:
                produce(kc, keep, slot_prod, cols)
            if j_cons is not None:
                consume(vt1, slot_cons, cols)

    step(0, 0, None, None)
    trips = (n_kv - 1) // FLASH_UNROLL if n_kv - 1 > FLASH_UNROLL else 0

    def body(trip, carry):
        j0 = trip * FLASH_UNROLL
        for u in range(FLASH_UNROLL):
            step(j0 + u + 1, (u + 1) % 2, j0 + u, u % 2)
        return carry

    if trips:
        lax.fori_loop(0, trips, body, 0)
    for j in range(trips * FLASH_UNROLL, n_kv - 1):
        step(j + 1, (j + 1) % 2, j, j % 2)
    step(None, None, n_kv - 1, (n_kv - 1) % 2)


def _store_heads_t(o_ref, ot, tq, group):
    for gg in range(group // 2):
        slab = jnp.concatenate([ot[:, (2 * gg) * tq:(2 * gg + 1) * tq],
                                ot[:, (2 * gg + 1) * tq:(2 * gg + 2) * tq]], axis=0)
        o_ref[0, :, gg * LANES:(gg + 1) * LANES] = slab.T.astype(o_ref.dtype)


def _kv_specs(key_rows, key_block, v_rows, kv_per_tile):
    return [pl.BlockSpec((1, key_rows, LANES), lambda bi, h, i: (bi, key_block, h // kv_per_tile)),
            pl.BlockSpec((1, v_rows, key_rows), lambda bi, h, i: (bi, h, key_block))]


def _flash_scratch(m_cols, tk, v_rows):
    return [pltpu.VMEM((LANES, m_cols), BF16),
            pltpu.VMEM((2, tk, m_cols), F32),
            pltpu.VMEM((2, 8, m_cols), F32),
            pltpu.VMEM((1, m_cols), F32),
            pltpu.VMEM((v_rows + ONES_ROWS, m_cols), F32)]


def _gqa_kernel(q_ref, k_ref, vt_ref, o_ref, qt_ref, s_ref, cm_ref, m_ref, acc_ref, *, n_kv, tk):
    group, tq = q_ref.shape[1:3]
    q_stacked = q_ref[0].reshape(group * tq, LANES)
    load_k = lambda j: k_ref[0, _chunk_rows(j, tk), :]
    load_vt = lambda j: vt_ref[0, :, _chunk_rows(j, tk)]
    _flash_loop(q_stacked, load_k, load_vt, qt_ref, s_ref, cm_ref, m_ref, acc_ref, n_kv=n_kv, tk=tk)
    ot = acc_ref[0:HEAD_DIM, :] / acc_ref[HEAD_DIM:HEAD_DIM + 1, :]
    _store_heads_t(o_ref, ot, tq, group)


def _gqa_call(q, k_all, vt_all, *, group, tq, tk, key_rows, key_block):
    b, hq, t, _ = q.shape
    return pl.pallas_call(
        functools.partial(_gqa_kernel, n_kv=key_rows // tk, tk=tk),
        out_shape=jax.ShapeDtypeStruct((b, t, hq * HEAD_DIM), BF16),
        grid=(b, hq // group, t // tq),
        in_specs=[pl.BlockSpec((1, group, tq, LANES), lambda bi, h, i: (bi, h, i, 0))]
                 + _kv_specs(key_rows, key_block, HEAD_DIM, LANES // HEAD_DIM),
        out_specs=pl.BlockSpec((1, tq, group * HEAD_DIM), lambda bi, h, i: (bi, i, h)),
        scratch_shapes=_flash_scratch(group * tq, tk, HEAD_DIM),
        compiler_params=_params(3),
        name="gqa_attention",
    )(q, k_all, vt_all)


def _diff_kernel(q_ref, k_ref, vt_ref, lam_ref, sg_ref, o_ref, qt_ref, s_ref, cm_ref, m_ref, acc_ref, *,
                 lam_init, n_kv, tk):
    tq = q_ref.shape[2]
    dv = vt_ref.shape[1]
    q_stacked = q_ref[0].reshape(2 * tq, LANES)
    load_k = lambda j: k_ref[0, _chunk_rows(j, tk), :]
    load_vt = lambda j: vt_ref[0, :, _chunk_rows(j, tk)]
    _flash_loop(q_stacked, load_k, load_vt, qt_ref, s_ref, cm_ref, m_ref, acc_ref, n_kv=n_kv, tk=tk)
    ot = acc_ref[0:dv, :] / acc_ref[dv:dv + 1, :]
    lv = lam_ref[...]
    lam = (jnp.exp(jnp.sum(lv[0:1] * lv[1:2], axis=1, keepdims=True))
           - jnp.exp(jnp.sum(lv[2:3] * lv[3:4], axis=1, keepdims=True)) + lam_init)
    ot = ot[:, :tq] - lam * ot[:, tq:]
    ms = jnp.mean(ot * ot, axis=0, keepdims=True)
    ot = (ot * lax.rsqrt(ms + EPS) * sg_ref[...]) * (1.0 - lam_init)
    o_ref[0] = ot.T.astype(o_ref.dtype)


def _diff_call(q, k_all, vt_all, lam_vecs, subln_g, *, lam_init, tq, tk, key_rows, key_block):
    b, hq, t, _ = q.shape
    heads = k_all.shape[2] // LANES
    dv = vt_all.shape[1] // heads
    return pl.pallas_call(
        functools.partial(_diff_kernel, lam_init=lam_init, n_kv=key_rows // tk, tk=tk),
        out_shape=jax.ShapeDtypeStruct((b, t, heads * dv), BF16),
        grid=(b, heads, t // tq),
        in_specs=[pl.BlockSpec((1, 2, tq, LANES), lambda bi, h, i: (bi, h, i, 0))]
                 + _kv_specs(key_rows, key_block, dv, 1)
                 + [pl.BlockSpec((4, HEAD_DIM), lambda bi, h, i: (0, 0)),
                    pl.BlockSpec((dv, 1), lambda bi, h, i: (0, 0))],
        out_specs=pl.BlockSpec((1, tq, dv), lambda bi, h, i: (bi, i, h)),
        scratch_shapes=_flash_scratch(2 * tq, tk, dv),
        compiler_params=_params(3),
        name="diff_attention",
    )(q, k_all, vt_all, lam_vecs, subln_g)


def _window_kernel(q_ref, k_ref, vt_ref, sink_ref, o_ref, qt_ref, s_ref, cm_ref, m_ref, acc_ref, *,
                   n_ctx, t_lat, tk):
    group, tq = q_ref.shape[1:3]
    assert tq == MXU_DIM
    span = tq + 2 * WINDOW
    n_lat = span // tk
    i = pl.program_id(2)
    start = pl.multiple_of(jnp.clip(i * tq - WINDOW, 0, t_lat - span), WINDOW)
    first_row = lambda j: (pl.multiple_of(start + j * tk, WINDOW) if j < n_lat
                           else t_lat + (j - n_lat) * tk)
    load_k = lambda j: k_ref[0, pl.ds(first_row(j), tk), :]
    load_vt = lambda j: vt_ref[0, :, pl.ds(first_row(j), tk)]
    rel = (lax.broadcasted_iota(jnp.int32, (tk, tq), 0) - lax.broadcasted_iota(jnp.int32, (tk, tq), 1)
           + (start - i * tq))

    def mask_fn(j):
        return jnp.abs(rel + j * tk) <= WINDOW if j < n_lat else None

    q_stacked = q_ref[0].reshape(group * tq, LANES)
    _flash_loop(q_stacked, load_k, load_vt, qt_ref, s_ref, cm_ref, m_ref, acc_ref,
                n_kv=n_lat + n_ctx // tk, tk=tk, mask_fn=mask_fn, sink=sink_ref[0] * LOG2E)
    ot = acc_ref[0:HEAD_DIM, :] / acc_ref[HEAD_DIM:HEAD_DIM + 1, :]
    _store_heads_t(o_ref, ot, tq, group)


def _window_call(q, k_all, vt_all, sink_rows, *, group, n_ctx, tq):
    b, hq, t, _ = q.shape
    t_all = k_all.shape[1]
    tk = MXU_DIM
    n_chunks = (tq + 2 * WINDOW) // tk + n_ctx // tk
    assert (tq + 2 * WINDOW) % tk == 0 and n_ctx % tk == 0 and n_chunks - 1 <= FLASH_UNROLL
    return pl.pallas_call(
        functools.partial(_window_kernel, n_ctx=n_ctx, t_lat=t, tk=tk),
        out_shape=jax.ShapeDtypeStruct((b, t, hq * HEAD_DIM), BF16),
        grid=(b, hq // group, t // tq),
        in_specs=[pl.BlockSpec((1, group, tq, LANES), lambda bi, h, i: (bi, h, i, 0))]
                 + _kv_specs(t_all, 0, HEAD_DIM, LANES // HEAD_DIM)
                 + [pl.BlockSpec((1, 1, group * tq), lambda bi, h, i: (h, 0, 0))],
        out_specs=pl.BlockSpec((1, tq, group * HEAD_DIM), lambda bi, h, i: (bi, i, h)),
        scratch_shapes=_flash_scratch(group * tq, tk, HEAD_DIM),
        compiler_params=_params(3),
        name="window_attention",
    )(q, k_all, vt_all, sink_rows)


def _glu_kernel(x_ref, mod_ref, g_ref, w_ref, b_ref, u_ref):
    x = x_ref[0]
    mod = mod_ref[0]
    d = x.shape[1]
    h = _modulate(x, g_ref[...], mod[0:1], mod[1:2]).astype(BF16)
    ag = _dot(h, w_ref[...]) + b_ref[...]
    u_ref[0] = (ag[:, :d] * jax.nn.sigmoid(ag[:, d:])).astype(u_ref.dtype)


def _glu_call(x, mod, gain, w, bias, *, tm):
    b, t, d = x.shape
    return pl.pallas_call(
        _glu_kernel,
        out_shape=jax.ShapeDtypeStruct((b, t, d), BF16),
        grid=(b, t // tm),
        in_specs=[pl.BlockSpec((1, tm, d), lambda bi, j: (bi, j, 0)),
                  pl.BlockSpec((1, 6, d), lambda bi, j: (bi, 0, 0)),
                  _resident((1, d)),
                  _resident((d, 2 * d)),
                  _resident((1, 2 * d))],
        out_specs=pl.BlockSpec((1, tm, d), lambda bi, j: (bi, j, 0)),
        compiler_params=_params(2),
        name="conv_glu",
    )(x, mod, gain, w, bias)


def _depthwise_conv(ubuf_ref, cbuf_ref, wdw_ref, bdw_ref, tm, d):
    win_rows = CONV_ROWS + 2 * CONV_HALO
    base = CONV_HALO - CONV_WIDTH // 2
    lane_chunk = LANES

    def step(r, carry):
        r0 = pl.multiple_of(r * CONV_ROWS, CONV_ROWS)
        for c0 in range(0, d, lane_chunk):
            win = ubuf_ref[pl.ds(r0, win_rows), c0:c0 + lane_chunk]
            acc = jnp.zeros((CONV_ROWS, lane_chunk), F32) + bdw_ref[:, c0:c0 + lane_chunk]
            for sub in range(8):
                shifted = win if sub == 0 else pltpu.roll(win, win_rows - sub, axis=0)
                for al in range(0, win_rows - CONV_ROWS + 1, 8):
                    k = al + sub - base
                    if 0 <= k < CONV_WIDTH:
                        acc = acc + wdw_ref[k:k + 1, c0:c0 + lane_chunk] * shifted[al:al + CONV_ROWS]
            cbuf_ref[pl.ds(r0, CONV_ROWS), c0:c0 + lane_chunk] = acc
        return carry

    lax.fori_loop(0, tm // CONV_ROWS, step, 0)


def _post_kernel(*refs, conv, has_bias, d_ff_chunk):
    it = iter(refs)
    x_ref = next(it)
    if conv:
        up_ref, uc_ref, un_ref = next(it), next(it), next(it)
        wdw_ref, bdw_ref, lng_ref, lnb_ref = next(it), next(it), next(it), next(it)
    else:
        o_ref = next(it)
    mod_ref, wo_ref = next(it), next(it)
    bo_ref = next(it) if has_bias else None
    g2_ref, wup_ref, wdn_ref, out_ref = next(it), next(it), next(it), next(it)
    if conv:
        ubuf_ref, cbuf_ref = next(it), next(it)

    x = x_ref[0]
    mod = mod_ref[0]
    tm, d = x.shape
    if conv:
        j = pl.program_id(1)
        prev = jnp.where(j > 0, up_ref[0].astype(F32), 0.0)
        nxt = jnp.where(j < pl.num_programs(1) - 1, un_ref[0].astype(F32), 0.0)
        ubuf_ref[0:CONV_HALO] = prev
        ubuf_ref[CONV_HALO:CONV_HALO + tm] = uc_ref[0].astype(F32)
        ubuf_ref[CONV_HALO + tm:] = nxt
        _depthwise_conv(ubuf_ref, cbuf_ref, wdw_ref, bdw_ref, tm, d)
        cv = cbuf_ref[...]
        mu = jnp.mean(cv, axis=-1, keepdims=True)
        cc = cv - mu
        var = jnp.mean(cc * cc, axis=-1, keepdims=True)
        yn = cc * lax.rsqrt(var + EPS) * lng_ref[...] + lnb_ref[...]
        o = (yn * jax.nn.sigmoid(yn)).astype(BF16)
    else:
        o = o_ref[0]
    y = _dot(o, wo_ref[...])
    if has_bias:
        y = y + bo_ref[...]
    x1 = x + mod[2:3] * y
    h2 = _modulate(x1, g2_ref[...], mod[3:4], mod[4:5]).astype(BF16)
    acc = jnp.zeros((tm, d), F32)
    d_ff = wup_ref.shape[1]
    for c0 in range(0, d_ff, d_ff_chunk):
        up = jnp.maximum(_dot(h2, wup_ref[:, c0:c0 + d_ff_chunk]), 0.0)
        acc = acc + _dot((up * up).astype(BF16), wdn_ref[c0:c0 + d_ff_chunk, :])
    out_ref[0] = x1 + mod[5:6] * acc


def _post_call(x, mixed, mod, w_o, b_o, gain2, w_up, w_down, conv_params, *, tm):
    b, t, d = x.shape
    d_in = w_o.shape[0]
    d_ff = w_up.shape[1]
    conv = conv_params is not None
    has_bias = b_o is not None
    tok = lambda bi, j: (bi, j, 0)
    in_specs = [pl.BlockSpec((1, tm, d), tok)]
    args = [x]
    scratch = []
    if conv:
        hb = tm // CONV_HALO
        last = t // CONV_HALO - 1
        in_specs += [pl.BlockSpec((1, CONV_HALO, d), lambda bi, j: (bi, jnp.maximum(j * hb - 1, 0), 0)),
                     pl.BlockSpec((1, tm, d), tok),
                     pl.BlockSpec((1, CONV_HALO, d), lambda bi, j: (bi, jnp.minimum((j + 1) * hb, last), 0)),
                     _resident((CONV_WIDTH, d)), _resident((1, d)), _resident((1, d)), _resident((1, d))]
        args += [mixed, mixed, mixed] + list(conv_params)
        scratch = [pltpu.VMEM((tm + 2 * CONV_HALO, d), F32), pltpu.VMEM((tm, d), F32)]
    else:
        in_specs += [pl.BlockSpec((1, tm, d_in), tok)]
        args += [mixed]
    in_specs += [pl.BlockSpec((1, 6, d), lambda bi, j: (bi, 0, 0)), _resident((d_in, d))]
    args += [mod, w_o]
    if has_bias:
        in_specs += [_resident((1, d))]
        args += [b_o]
    in_specs += [_resident((1, d)), _resident((d, d_ff)), _resident((d_ff, d))]
    args += [gain2, w_up, w_down]
    return pl.pallas_call(
        functools.partial(_post_kernel, conv=conv, has_bias=has_bias, d_ff_chunk=min(d_ff, 1024)),
        out_shape=jax.ShapeDtypeStruct((b, t, d), F32),
        grid=(b, t // tm),
        in_specs=in_specs,
        out_specs=pl.BlockSpec((1, tm, d), tok),
        scratch_shapes=scratch,
        compiler_params=_params(2),
        name="post_mlp",
    )(*args)


def _rope_tables(t):
    rows = t // GRID_W
    row = jnp.repeat(jnp.arange(rows, dtype=F32), GRID_W)
    col = jnp.tile(jnp.arange(GRID_W, dtype=F32), rows)
    half = HEAD_DIM // 2
    inv = 1.0 / jnp.power(ROPE_THETA, jnp.arange(0, half, 2, dtype=F32) / half)
    ang = jnp.concatenate([row[:, None] * inv, col[:, None] * inv], axis=-1)
    cos = jnp.repeat(jnp.cos(ang), 2, axis=1)
    sin = jnp.repeat(jnp.sin(ang), 2, axis=1) * jnp.tile(jnp.array([-1.0, 1.0], F32), half)
    reps = LANES // HEAD_DIM
    return jnp.tile(cos, (1, reps)), jnp.tile(sin, (1, reps))


def _prep_qkv(w_qkv, q_g, k_g, n_q, n_k):
    hq, hk = n_q // HEAD_DIM, n_k // HEAD_DIM
    w = w_qkv[:, :n_q + n_k].astype(BF16)
    wvt = w_qkv[:, n_q + n_k:].T.astype(BF16)
    scale = LOG2E / math.sqrt(HEAD_DIM)
    head_gain = jnp.concatenate([jnp.tile(q_g * scale, hq), jnp.tile(k_g, hk)])
    return w, wvt, head_gain.reshape(1, n_q + n_k).astype(F32)


def _ones_block_diag():
    idx = jnp.arange(MXU_DIM) // HEAD_DIM
    return (idx[:, None] == idx[None, :]).astype(BF16)


def kernel(x, c, ctx, c_ctx, norm1_g, norm2_g, mod_w, mod_b, mlp_up, mlp_down,
           gqa_w_qkv, gqa_q_g, gqa_k_g, gqa_w_o,
           conv_w_pw1, conv_b_pw1, conv_w_dw, conv_b_dw, conv_ln_g, conv_ln_b, conv_w_pw2, conv_b_pw2,
           diff_w_qkv, diff_q_g, diff_k_g, diff_lam_q1, diff_lam_k1, diff_lam_q2, diff_lam_k2,
           diff_subln_g, diff_w_o,
           swa_w_qkv, swa_q_g, swa_k_g, swa_sink, swa_w_o):
    b, t, d = x.shape
    n_ctx = ctx.shape[1]
    depth = norm1_g.shape[0]
    assert depth == N_MIXERS and t % GRID_W == 0 and t % n_ctx == 0

    tm = _pick(t, (512, 256, 128))
    tq_win = _pick(t, (256, 128))
    tq_gqa = _pick(t, (1024, 512, 256, 128))
    tq_diff = _pick(t, (2048, 1024, 512, 256, 128))
    t_all = n_ctx + t
    tk = _pick(t_all, (768, 512, 256, 128))
    tkc = _pick(n_ctx, (768, 512, 256, 128))

    cv = jnp.concatenate([c, c_ctx[None], jnp.zeros((8 - b - 1, d), F32)], axis=0)
    mod_all = _mod_call(cv, mod_w, mod_b).reshape(depth, 8, 6, d)
    rope = _rope_tables(t)
    ones_bd = _ones_block_diag()
    row = lambda v: v.reshape(1, -1).astype(F32)

    xc = ctx
    for i in range(depth):
        m, j = i % N_MIXERS, i // N_MIXERS
        need_ctx = i < depth - 1
        mod_l = mod_all[i, :b]
        mod_c = jnp.broadcast_to(mod_all[i, b], (b, 6, d))
        g1, g2 = row(norm1_g[i]), row(norm2_g[i])
        w_up, w_down = mlp_up[i].astype(BF16), mlp_down[i].astype(BF16)
        conv_params = None
        b_o = None
        if m == 1:
            w1, b1 = conv_w_pw1[j].astype(BF16), row(conv_b_pw1[j])
            mixed = _glu_call(x, mod_l, g1, w1, b1, tm=tm)
            mixed_c = _glu_call(xc, mod_c, g1, w1, b1, tm=n_ctx) if need_ctx else None
            conv_params = (conv_w_dw[j].astype(F32), row(conv_b_dw[j]), row(conv_ln_g[j]), row(conv_ln_b[j]))
            w_o, b_o = conv_w_pw2[j].astype(BF16), row(conv_b_pw2[j])
        else:
            if m == 2:
                w_qkv, q_g, k_g, w_o = diff_w_qkv[j], diff_q_g[j], diff_k_g[j], diff_w_o[j]
                n_q = n_k = w_qkv.shape[1] // 3
                q_half = tuple(h % 2 for h in range(n_q // HEAD_DIM))
            else:
                w_qkv, q_g, k_g, w_o = ((gqa_w_qkv[j], gqa_q_g[j], gqa_k_g[j], gqa_w_o[j]) if m == 0 else
                                        (swa_w_qkv[j], swa_q_g[j], swa_k_g[j], swa_w_o[j]))
                n_q = w_o.shape[0]
                n_k = (w_qkv.shape[1] - n_q) // 2
                group = n_q // n_k
                q_half = tuple((h // group) % 2 for h in range(n_q // HEAD_DIM))
            w_o = w_o.astype(BF16)
            w_p, wvt, head_gain = _prep_qkv(w_qkv, q_g, k_g, n_q, n_k)
            proj = functools.partial(_proj_call, gain=g1, w=w_p, wvt=wvt, head_gain=head_gain,
                                     ones_bd=ones_bd, n_q=n_q, n_k=n_k, q_half=q_half)
            kv_buffers = (jnp.zeros((b, t_all, n_k), BF16), jnp.zeros((b, wvt.shape[0], t_all), BF16))
            q, *kv_buffers = proj(x, mod_l, rope=rope, tm=tm, kv_buffers=kv_buffers, first_row=0)
            qc, k_all, vt_all = proj(xc, mod_c, rope=None, tm=n_ctx, kv_buffers=kv_buffers, first_row=t)
            lat_keys = dict(tk=tk, key_rows=t_all, key_block=0)
            ctx_keys = dict(tk=tkc, key_rows=n_ctx, key_block=t // n_ctx)
            mixed_c = None
            if m == 0:
                mixed = _gqa_call(q, k_all, vt_all, group=group, tq=tq_gqa, **lat_keys)
                if need_ctx:
                    mixed_c = _gqa_call(qc, k_all, vt_all, group=group, tq=n_ctx, **ctx_keys)
            elif m == 2:
                lam_init = 0.8 - 0.6 * math.exp(-0.3 * i)
                lam_vecs = jnp.stack([diff_lam_q1[j], diff_lam_k1[j], diff_lam_q2[j], diff_lam_k2[j]]).astype(F32)
                sg = diff_subln_g[j].reshape(-1, 1).astype(F32)
                mixed = _diff_call(q, k_all, vt_all, lam_vecs, sg, lam_init=lam_init, tq=tq_diff, **lat_keys)
                if need_ctx:
                    mixed_c = _diff_call(qc, k_all, vt_all, lam_vecs, sg, lam_init=lam_init, tq=n_ctx, **ctx_keys)
            else:
                assert not need_ctx, "windowed layer with a context update is not supported"
                sink_rows = jnp.repeat(swa_sink[j].astype(F32).reshape(n_k // HEAD_DIM, 1, group), tq_win, axis=2)
                mixed = _window_call(q, k_all, vt_all, sink_rows, group=group, n_ctx=n_ctx, tq=tq_win)
        x = _post_call(x, mixed, mod_l, w_o, b_o, g2, w_up, w_down, conv_params, tm=tm)
        if need_ctx:
            xc = _post_call(xc, mixed_c, mod_c, w_o, b_o, g2, w_up, w_down, conv_params, tm=n_ctx)
    return x
```

```python
import functools
import math

import jax
import jax.numpy as jnp
from jax import lax
from jax.experimental import pallas as pl
from jax.experimental.pallas import tpu as pltpu

F32 = jnp.float32
BF16 = jnp.bfloat16

GRID_W = 64
HEAD_DIM = 64
N_MIXERS = 4
CONV_WIDTH = 31
WINDOW = 128
ROPE_THETA = 10000.0
EPS = 1e-6
NEG_INF = -1e30
LOG2E = math.log2(math.e)

LANES = 128
MXU_DIM = 256
VMEM_LIMIT_BYTES = 56 * 1024 * 1024

CONV_HALO = 16
CONV_ROWS = 128


def _params(n_axes):
    return pltpu.CompilerParams(dimension_semantics=("parallel",) * n_axes,
                                vmem_limit_bytes=VMEM_LIMIT_BYTES)


def _resident(shape):
    nd = len(shape)
    return pl.BlockSpec(shape, lambda *_: (0,) * nd, pipeline_mode=pl.Buffered(1))


def _pick(n, candidates):
    for cand in candidates:
        if n % cand == 0:
            return cand
    return n


def _modulate(x, gain, shift, scale):
    ms = jnp.mean(x * x, axis=-1, keepdims=True)
    return (x * lax.rsqrt(ms + EPS) * gain) * (1.0 + scale) + shift


def _dot(a, b):
    return jnp.dot(a, b, preferred_element_type=F32)


def _dot_nt(a, b):
    return lax.dot_general(a, b, (((1,), (1,)), ((), ())), preferred_element_type=F32)


def _mod_kernel(cv_ref, w_ref, b_ref, o_ref):
    cv = cv_ref[...]
    s = cv * jax.nn.sigmoid(cv)
    o_ref[0] = _dot(s.astype(BF16), w_ref[0].astype(BF16)) + b_ref[0]


def _mod_call(cv, mod_w, mod_b):
    depth, d, n = mod_w.shape
    rows = cv.shape[0]
    tn = _pick(n, (3072, 1536, 1024, 512))
    return pl.pallas_call(
        _mod_kernel,
        out_shape=jax.ShapeDtypeStruct((depth, rows, n), F32),
        grid=(depth, n // tn),
        in_specs=[pl.BlockSpec((rows, d), lambda i, j: (0, 0)),
                  pl.BlockSpec((1, d, tn), lambda i, j: (i, 0, j)),
                  pl.BlockSpec((1, 1, tn), lambda i, j: (i, 0, j))],
        out_specs=pl.BlockSpec((1, rows, tn), lambda i, j: (i, 0, j)),
        compiler_params=_params(2),
        name="modulation",
    )(cv, mod_w, mod_b.reshape(depth, 1, n))


def _proj_kernel(*refs, n_q, n_k, q_half, use_rope, n_alias):
    q_ref, k_ref, vt_ref = refs[-3:]
    refs = refs[:len(refs) - 3 - n_alias]
    if use_rope:
        (x_ref, mod_ref, g_ref, w_ref, wvt_ref, hg_ref, e_ref, cos_ref, sin_ref) = refs
    else:
        (x_ref, mod_ref, g_ref, w_ref, wvt_ref, hg_ref, e_ref) = refs
    mod = mod_ref[0]
    tm_full = x_ref.shape[1]
    n_qk = n_q + n_k
    ones_bd = e_ref[...]
    n_parts = tm_full // MXU_DIM if tm_full % MXU_DIM == 0 else 1
    tm = tm_full // n_parts
    lane = lax.broadcasted_iota(jnp.int32, (tm, LANES), 1)
    even = (lane % 2) == 0
    low = lane < HEAD_DIM
    for part in range(n_parts):
        rows = slice(part * tm, (part + 1) * tm)
        h = _modulate(x_ref[0, rows, :], g_ref[...], mod[0:1], mod[1:2]).astype(BF16)
        qk = _dot(h, w_ref[...])
        vt_ref[0, :, rows] = _dot_nt(wvt_ref[...], h).astype(BF16)
        sq = qk * qk
        hi = sq.astype(BF16)
        lo = (sq - hi.astype(F32)).astype(BF16)
        ss = jnp.concatenate(
            [_dot(hi[:, t:t + MXU_DIM], ones_bd) + _dot(lo[:, t:t + MXU_DIM], ones_bd)
             for t in range(0, n_qk, MXU_DIM)], axis=1)
        qk = qk * lax.rsqrt(ss * (1.0 / HEAD_DIM) + EPS) * hg_ref[...]
        tiles = [qk[:, t:t + LANES] for t in range(0, n_qk, LANES)]
        if use_rope:
            cos = cos_ref[rows, :]
            sin = sin_ref[rows, :]
            tiles = [t * cos + jnp.where(even, pltpu.roll(t, LANES - 1, axis=1), pltpu.roll(t, 1, axis=1)) * sin
                     for t in tiles]
        for j in range(n_q // HEAD_DIM):
            t = tiles[j // 2]
            if j % 2 != q_half[j]:
                t = pltpu.roll(t, HEAD_DIM, axis=1)
            keep = low if q_half[j] == 0 else jnp.logical_not(low)
            q_ref[0, j, rows, :] = jnp.where(keep, t, 0.0).astype(BF16)
        k_ref[0, rows, :] = jnp.concatenate(tiles[n_q // LANES:], axis=1).astype(BF16)


def _proj_call(x, mod, gain, w, wvt, head_gain, ones_bd, rope, *, n_q, n_k, q_half, tm, kv_buffers,
               first_row):
    b, t, d = x.shape
    n_v = wvt.shape[0]
    t_keys = kv_buffers[0].shape[1]
    hq = n_q // HEAD_DIM
    use_rope = rope is not None
    in_specs = [pl.BlockSpec((1, tm, d), lambda bi, j: (bi, j, 0)),
                pl.BlockSpec((1, 6, d), lambda bi, j: (bi, 0, 0)),
                _resident((1, d)),
                _resident((d, n_q + n_k)),
                _resident((n_v, d)),
                _resident((1, n_q + n_k)),
                _resident((MXU_DIM, MXU_DIM))]
    args = [x, mod, gain, w, wvt, head_gain, ones_bd]
    if use_rope:
        in_specs += [pl.BlockSpec((tm, LANES), lambda bi, j: (j, 0)),
                     pl.BlockSpec((tm, LANES), lambda bi, j: (j, 0))]
        args += list(rope)
    off = first_row // tm
    assert off * tm == first_row and first_row + t <= t_keys
    aliases = {len(args): 1, len(args) + 1: 2}
    in_specs += [pl.BlockSpec(memory_space=pl.ANY), pl.BlockSpec(memory_space=pl.ANY)]
    args += list(kv_buffers)
    return pl.pallas_call(
        functools.partial(_proj_kernel, n_q=n_q, n_k=n_k, q_half=q_half, use_rope=use_rope,
                          n_alias=len(aliases)),
        out_shape=(jax.ShapeDtypeStruct((b, hq, t, LANES), BF16),
                   jax.ShapeDtypeStruct((b, t_keys, n_k), BF16),
                   jax.ShapeDtypeStruct((b, n_v, t_keys), BF16)),
        grid=(b, t // tm),
        in_specs=in_specs,
        out_specs=(pl.BlockSpec((1, hq, tm, LANES), lambda bi, j: (bi, 0, j, 0)),
                   pl.BlockSpec((1, tm, n_k), lambda bi, j: (bi, j + off, 0)),
                   pl.BlockSpec((1, n_v, tm), lambda bi, j: (bi, 0, j + off))),
        input_output_aliases=aliases,
        compiler_params=_params(2),
        name="attn_proj",
    )(*args)


ONES_ROWS = 16
FLASH_UNROLL = 4


def _chunk_rows(j, tk):
    return pl.ds(j * tk if isinstance(j, int) else pl.multiple_of(j * tk, tk), tk)


def _flash_loop(q_stacked, load_k, load_vt, qt_ref, s_ref, cm_ref, m_ref, acc_ref, *, n_kv, tk,
                mask_fn=None, sink=None):
    v_rows = acc_ref.shape[0] - ONES_ROWS
    qt_ref[...] = q_stacked.astype(F32).T.astype(BF16)
    acc_ref[0:v_rows, :] = jnp.zeros((v_rows, acc_ref.shape[1]), F32)
    if sink is None:
        m_ref[...] = jnp.full(m_ref.shape, NEG_INF, F32)
        acc_ref[v_rows:, :] = jnp.zeros((ONES_ROWS, acc_ref.shape[1]), F32)
    else:
        m_ref[...] = sink
        acc_ref[v_rows:, :] = jnp.ones((ONES_ROWS, acc_ref.shape[1]), F32)
    ones = jnp.ones((ONES_ROWS, tk), BF16)
    m_cols = s_ref.shape[2]
    col_blocks = [slice(c0, c0 + MXU_DIM) for c0 in range(0, m_cols, MXU_DIM)]

    def produce(kc, keep, slot, cols):
        s = _dot(kc, qt_ref[:, cols])
        if keep is not None:
            s = jnp.where(keep, s, NEG_INF)
        s_ref[slot, :, cols] = s
        cm_ref[slot, :, cols] = jnp.max(s.reshape(tk // 8, 8, MXU_DIM), axis=0)

    def consume(vt1, slot, cols):
        m_prev = m_ref[:, cols]
        m_new = jnp.maximum(m_prev, jnp.max(cm_ref[slot, :, cols], axis=0, keepdims=True))
        alpha = jnp.exp2(m_prev - m_new)
        p = jnp.exp2(s_ref[slot, :, cols] - m_new).astype(BF16)
        acc_ref[:, cols] = alpha * acc_ref[:, cols] + _dot(vt1, p)
        m_ref[:, cols] = m_new

    def step(j_prod, slot_prod, j_cons, slot_cons):
        if j_prod is not None:
            kc = load_k(j_prod)
            keep = mask_fn(j_prod) if mask_fn is not None else None
        if j_cons is not None:
            vt1 = jnp.concatenate([load_vt(j_cons), ones], axis=0)
        for cols in col_blocks:
            if j_prod is not None:
                produce(kc, keep, slot_prod, cols)
            if j_cons is not None:
                consume(vt1, slot_cons, cols)

    step(0, 0, None, None)
    trips = (n_kv - 1) // FLASH_UNROLL if n_kv - 1 > FLASH_UNROLL else 0

    def body(trip, carry):
        j0 = trip * FLASH_UNROLL
        for u in range(FLASH_UNROLL):
            step(j0 + u + 1, (u + 1) % 2, j0 + u, u % 2)
        return carry

    if trips:
        lax.fori_loop(0, trips, body, 0)
    for j in range(trips * FLASH_UNROLL, n_kv - 1):
        step(j + 1, (j + 1) % 2, j, j % 2)
    step(None, None, n_kv - 1, (n_kv - 1) % 2)


def _store_heads_t(o_ref, ot, tq, group):
    for gg in range(group // 2):
        slab = jnp.concatenate([ot[:, (2 * gg) * tq:(2 * gg + 1) * tq],
                                ot[:, (2 * gg + 1) * tq:(2 * gg + 2) * tq]], axis=0)
        o_ref[0, :, gg * LANES:(gg + 1) * LANES] = slab.T.astype(o_ref.dtype)


def _kv_specs(key_rows, key_block, v_rows, kv_per_tile):
    return [pl.BlockSpec((1, key_rows, LANES), lambda bi, h, i: (bi, key_block, h // kv_per_tile)),
            pl.BlockSpec((1, v_rows, key_rows), lambda bi, h, i: (bi, h, key_block))]


def _flash_scratch(m_cols, tk, v_rows):
    return [pltpu.VMEM((LANES, m_cols), BF16),
            pltpu.VMEM((2, tk, m_cols), F32),
            pltpu.VMEM((2, 8, m_cols), F32),
            pltpu.VMEM((1, m_cols), F32),
            pltpu.VMEM((v_rows + ONES_ROWS, m_cols), F32)]


def _gqa_kernel(q_ref, k_ref, vt_ref, o_ref, qt_ref, s_ref, cm_ref, m_ref, acc_ref, *, n_kv, tk):
    group, tq = q_ref.shape[1:3]
    q_stacked = q_ref[0].reshape(group * tq, LANES)
    load_k = lambda j: k_ref[0, _chunk_rows(j, tk), :]
    load_vt = lambda j: vt_ref[0, :, _chunk_rows(j, tk)]
    _flash_loop(q_stacked, load_k, load_vt, qt_ref, s_ref, cm_ref, m_ref, acc_ref, n_kv=n_kv, tk=tk)
    ot = acc_ref[0:HEAD_DIM, :] / acc_ref[HEAD_DIM:HEAD_DIM + 1, :]
    _store_heads_t(o_ref, ot, tq, group)


def _gqa_call(q, k_all, vt_all, *, group, tq, tk, key_rows, key_block):
    b, hq, t, _ = q.shape
    return pl.pallas_call(
        functools.partial(_gqa_kernel, n_kv=key_rows // tk, tk=tk),
        out_shape=jax.ShapeDtypeStruct((b, t, hq * HEAD_DIM), BF16),
        grid=(b, hq // group, t // tq),
        in_specs=[pl.BlockSpec((1, group, tq, LANES), lambda bi, h, i: (bi, h, i, 0))]
                 + _kv_specs(key_rows, key_block, HEAD_DIM, LANES // HEAD_DIM),
        out_specs=pl.BlockSpec((1, tq, group * HEAD_DIM), lambda bi, h, i: (bi, i, h)),
        scratch_shapes=_flash_scratch(group * tq, tk, HEAD_DIM),
        compiler_params=_params(3),
        name="gqa_attention",
    )(q, k_all, vt_all)


def _diff_kernel(q_ref, k_ref, vt_ref, lam_ref, sg_ref, o_ref, qt_ref, s_ref, cm_ref, m_ref, acc_ref, *,
                 lam_init, n_kv, tk):
    tq = q_ref.shape[2]
    dv = vt_ref.shape[1]
    q_stacked = q_ref[0].reshape(2 * tq, LANES)
    load_k = lambda j: k_ref[0, _chunk_rows(j, tk), :]
    load_vt = lambda j: vt_ref[0, :, _chunk_rows(j, tk)]
    _flash_loop(q_stacked, load_k, load_vt, qt_ref, s_ref, cm_ref, m_ref, acc_ref, n_kv=n_kv, tk=tk)
    ot = acc_ref[0:dv, :] / acc_ref[dv:dv + 1, :]
    lv = lam_ref[...]
    lam = (jnp.exp(jnp.sum(lv[0:1] * lv[1:2], axis=1, keepdims=True))
           - jnp.exp(jnp.sum(lv[2:3] * lv[3:4], axis=1, keepdims=True)) + lam_init)
    ot = ot[:, :tq] - lam * ot[:, tq:]
    ms = jnp.mean(ot * ot, axis=0, keepdims=True)
    ot = (ot * lax.rsqrt(ms + EPS) * sg_ref[...]) * (1.0 - lam_init)
    o_ref[0] = ot.T.astype(o_ref.dtype)


def _diff_call(q, k_all, vt_all, lam_vecs, subln_g, *, lam_init, tq, tk, key_rows, key_block):
    b, hq, t, _ = q.shape
    heads = k_all.shape[2] // LANES
    dv = vt_all.shape[1] // heads
    return pl.pallas_call(
        functools.partial(_diff_kernel, lam_init=lam_init, n_kv=key_rows // tk, tk=tk),
        out_shape=jax.ShapeDtypeStruct((b, t, heads * dv), BF16),
        grid=(b, heads, t // tq),
        in_specs=[pl.BlockSpec((1, 2, tq, LANES), lambda bi, h, i: (bi, h, i, 0))]
                 + _kv_specs(key_rows, key_block, dv, 1)
                 + [pl.BlockSpec((4, HEAD_DIM), lambda bi, h, i: (0, 0)),
                    pl.BlockSpec((dv, 1), lambda bi, h, i: (0, 0))],
        out_specs=pl.BlockSpec((1, tq, dv), lambda bi, h, i: (bi, i, h)),
        scratch_shapes=_flash_scratch(2 * tq, tk, dv),
        compiler_params=_params(3),
        name="diff_attention",
    )(q, k_all, vt_all, lam_vecs, subln_g)


def _window_kernel(q_ref, k_ref, vt_ref, sink_ref, o_ref, qt_ref, s_ref, cm_ref, m_ref, acc_ref, *,
                   n_ctx, t_lat, tk):
    group, tq = q_ref.shape[1:3]
    assert tq == MXU_DIM
    span = tq + 2 * WINDOW
    n_lat = span // tk
    i = pl.program_id(2)
    start = pl.multiple_of(jnp.clip(i * tq - WINDOW, 0, t_lat - span), WINDOW)
    first_row = lambda j: (pl.multiple_of(start + j * tk, WINDOW) if j < n_lat
                           else t_lat + (j - n_lat) * tk)
    load_k = lambda j: k_ref[0, pl.ds(first_row(j), tk), :]
    load_vt = lambda j: vt_ref[0, :, pl.ds(first_row(j), tk)]
    rel = (lax.broadcasted_iota(jnp.int32, (tk, tq), 0) - lax.broadcasted_iota(jnp.int32, (tk, tq), 1)
           + (start - i * tq))

    def mask_fn(j):
        return jnp.abs(rel + j * tk) <= WINDOW if j < n_lat else None

    q_stacked = q_ref[0].reshape(group * tq, LANES)
    _flash_loop(q_stacked, load_k, load_vt, qt_ref, s_ref, cm_ref, m_ref, acc_ref,
                n_kv=n_lat + n_ctx // tk, tk=tk, mask_fn=mask_fn, sink=sink_ref[0] * LOG2E)
    ot = acc_ref[0:HEAD_DIM, :] / acc_ref[HEAD_DIM:HEAD_DIM + 1, :]
    _store_heads_t(o_ref, ot, tq, group)


def _window_call(q, k_all, vt_all, sink_rows, *, group, n_ctx, tq):
    b, hq, t, _ = q.shape
    t_all = k_all.shape[1]
    tk = MXU_DIM
    n_chunks = (tq + 2 * WINDOW) // tk + n_ctx // tk
    assert (tq + 2 * WINDOW) % tk == 0 and n_ctx % tk == 0 and n_chunks - 1 <= FLASH_UNROLL
    return pl.pallas_call(
        functools.partial(_window_kernel, n_ctx=n_ctx, t_lat=t, tk=tk),
        out_shape=jax.ShapeDtypeStruct((b, t, hq * HEAD_DIM), BF16),
        grid=(b, hq // group, t // tq),
        in_specs=[pl.BlockSpec((1, group, tq, LANES), lambda bi, h, i: (bi, h, i, 0))]
                 + _kv_specs(t_all, 0, HEAD_DIM, LANES // HEAD_DIM)
                 + [pl.BlockSpec((1, 1, group * tq), lambda bi, h, i: (h, 0, 0))],
        out_specs=pl.BlockSpec((1, tq, group * HEAD_DIM), lambda bi, h, i: (bi, i, h)),
        scratch_shapes=_flash_scratch(group * tq, tk, HEAD_DIM),
        compiler_params=_params(3),
        name="window_attention",
    )(q, k_all, vt_all, sink_rows)


def _glu_kernel(x_ref, mod_ref, g_ref, w_ref, b_ref, u_ref):
    mod = mod_ref[0]
    tm_full, d = x_ref.shape[1:]
    n_parts = tm_full // MXU_DIM if tm_full % MXU_DIM == 0 else 1
    tm = tm_full // n_parts
    for part in range(n_parts):
        rows = slice(part * tm, (part + 1) * tm)
        h = _modulate(x_ref[0, rows, :], g_ref[...], mod[0:1], mod[1:2]).astype(BF16)
        ag = _dot(h, w_ref[...]) + b_ref[...]
        u_ref[0, rows, :] = (ag[:, :d] * jax.nn.sigmoid(ag[:, d:])).astype(u_ref.dtype)


def _glu_call(x, mod, gain, w, bias, *, tm):
    b, t, d = x.shape
    return pl.pallas_call(
        _glu_kernel,
        out_shape=jax.ShapeDtypeStruct((b, t, d), BF16),
        grid=(b, t // tm),
        in_specs=[pl.BlockSpec((1, tm, d), lambda bi, j: (bi, j, 0)),
                  pl.BlockSpec((1, 6, d), lambda bi, j: (bi, 0, 0)),
                  _resident((1, d)),
                  _resident((d, 2 * d)),
                  _resident((1, 2 * d))],
        out_specs=pl.BlockSpec((1, tm, d), lambda bi, j: (bi, j, 0)),
        compiler_params=_params(2),
        name="conv_glu",
    )(x, mod, gain, w, bias)


def _depthwise_conv(ubuf_ref, cbuf_ref, wdw_ref, bdw_ref, tm, d):
    win_rows = CONV_ROWS + 2 * CONV_HALO
    base = CONV_HALO - CONV_WIDTH // 2
    lane_chunk = LANES

    def step(r, carry):
        r0 = pl.multiple_of(r * CONV_ROWS, CONV_ROWS)
        for c0 in range(0, d, lane_chunk):
            win = ubuf_ref[pl.ds(r0, win_rows), c0:c0 + lane_chunk]
            acc = jnp.zeros((CONV_ROWS, lane_chunk), F32) + bdw_ref[:, c0:c0 + lane_chunk]
            for sub in range(8):
                shifted = win if sub == 0 else pltpu.roll(win, win_rows - sub, axis=0)
                for al in range(0, win_rows - CONV_ROWS + 1, 8):
                    k = al + sub - base
                    if 0 <= k < CONV_WIDTH:
                        acc = acc + wdw_ref[k:k + 1, c0:c0 + lane_chunk] * shifted[al:al + CONV_ROWS]
            cbuf_ref[pl.ds(r0, CONV_ROWS), c0:c0 + lane_chunk] = acc
        return carry

    lax.fori_loop(0, tm // CONV_ROWS, step, 0)


def _post_kernel(*refs, conv, has_bias, d_ff_chunk):
    it = iter(refs)
    x_ref = next(it)
    if conv:
        up_ref, uc_ref, un_ref = next(it), next(it), next(it)
        wdw_ref, bdw_ref, lng_ref, lnb_ref = next(it), next(it), next(it), next(it)
    else:
        o_ref = next(it)
    mod_ref, wo_ref = next(it), next(it)
    bo_ref = next(it) if has_bias else None
    g2_ref, wup_ref, wdn_ref, out_ref = next(it), next(it), next(it), next(it)
    if conv:
        ubuf_ref, cbuf_ref = next(it), next(it)

    x = x_ref[0]
    mod = mod_ref[0]
    tm, d = x.shape
    if conv:
        j = pl.program_id(1)
        prev = jnp.where(j > 0, up_ref[0].astype(F32), 0.0)
        nxt = jnp.where(j < pl.num_programs(1) - 1, un_ref[0].astype(F32), 0.0)
        ubuf_ref[0:CONV_HALO] = prev
        ubuf_ref[CONV_HALO:CONV_HALO + tm] = uc_ref[0].astype(F32)
        ubuf_ref[CONV_HALO + tm:] = nxt
        _depthwise_conv(ubuf_ref, cbuf_ref, wdw_ref, bdw_ref, tm, d)
        cv = cbuf_ref[...]
        mu = jnp.mean(cv, axis=-1, keepdims=True)
        cc = cv - mu
        var = jnp.mean(cc * cc, axis=-1, keepdims=True)
        yn = cc * lax.rsqrt(var + EPS) * lng_ref[...] + lnb_ref[...]
        o = (yn * jax.nn.sigmoid(yn)).astype(BF16)
    else:
        o = o_ref[0]
    y = _dot(o, wo_ref[...])
    if has_bias:
        y = y + bo_ref[...]
    x1 = x + mod[2:3] * y
    h2 = _modulate(x1, g2_ref[...], mod[3:4], mod[4:5]).astype(BF16)
    acc = jnp.zeros((tm, d), F32)
    d_ff = wup_ref.shape[1]
    for c0 in range(0, d_ff, d_ff_chunk):
        up = jnp.maximum(_dot(h2, wup_ref[:, c0:c0 + d_ff_chunk]), 0.0)
        acc = acc + _dot((up * up).astype(BF16), wdn_ref[c0:c0 + d_ff_chunk, :])
    out_ref[0] = x1 + mod[5:6] * acc


def _post_call(x, mixed, mod, w_o, b_o, gain2, w_up, w_down, conv_params, *, tm):
    b, t, d = x.shape
    d_in = w_o.shape[0]
    d_ff = w_up.shape[1]
    conv = conv_params is not None
    has_bias = b_o is not None
    tok = lambda bi, j: (bi, j, 0)
    in_specs = [pl.BlockSpec((1, tm, d), tok)]
    args = [x]
    scratch = []
    if conv:
        hb = tm // CONV_HALO
        last = t // CONV_HALO - 1
        in_specs += [pl.BlockSpec((1, CONV_HALO, d), lambda bi, j: (bi, jnp.maximum(j * hb - 1, 0), 0)),
                     pl.BlockSpec((1, tm, d), tok),
                     pl.BlockSpec((1, CONV_HALO, d), lambda bi, j: (bi, jnp.minimum((j + 1) * hb, last), 0)),
                     _resident((CONV_WIDTH, d)), _resident((1, d)), _resident((1, d)), _resident((1, d))]
        args += [mixed, mixed, mixed] + list(conv_params)
        scratch = [pltpu.VMEM((tm + 2 * CONV_HALO, d), F32), pltpu.VMEM((tm, d), F32)]
    else:
        in_specs += [pl.BlockSpec((1, tm, d_in), tok)]
        args += [mixed]
    in_specs += [pl.BlockSpec((1, 6, d), lambda bi, j: (bi, 0, 0)), _resident((d_in, d))]
    args += [mod, w_o]
    if has_bias:
        in_specs += [_resident((1, d))]
        args += [b_o]
    in_specs += [_resident((1, d)), _resident((d, d_ff)), _resident((d_ff, d))]
    args += [gain2, w_up, w_down]
    return pl.pallas_call(
        functools.partial(_post_kernel, conv=conv, has_bias=has_bias, d_ff_chunk=min(d_ff, 1024)),
        out_shape=jax.ShapeDtypeStruct((b, t, d), F32),
        grid=(b, t // tm),
        in_specs=in_specs,
        out_specs=pl.BlockSpec((1, tm, d), tok),
        scratch_shapes=scratch,
        compiler_params=_params(2),
        name="post_mlp",
    )(*args)


def _rope_tables(t):
    rows = t // GRID_W
    row = jnp.repeat(jnp.arange(rows, dtype=F32), GRID_W)
    col = jnp.tile(jnp.arange(GRID_W, dtype=F32), rows)
    half = HEAD_DIM // 2
    inv = 1.0 / jnp.power(ROPE_THETA, jnp.arange(0, half, 2, dtype=F32) / half)
    ang = jnp.concatenate([row[:, None] * inv, col[:, None] * inv], axis=-1)
    cos = jnp.repeat(jnp.cos(ang), 2, axis=1)
    sin = jnp.repeat(jnp.sin(ang), 2, axis=1) * jnp.tile(jnp.array([-1.0, 1.0], F32), half)
    reps = LANES // HEAD_DIM
    return jnp.tile(cos, (1, reps)), jnp.tile(sin, (1, reps))


def _prep_qkv(w_qkv, q_g, k_g, n_q, n_k):
    hq, hk = n_q // HEAD_DIM, n_k // HEAD_DIM
    w = w_qkv[:, :n_q + n_k].astype(BF16)
    wvt = w_qkv[:, n_q + n_k:].T.astype(BF16)
    scale = LOG2E / math.sqrt(HEAD_DIM)
    head_gain = jnp.concatenate([jnp.tile(q_g * scale, hq), jnp.tile(k_g, hk)])
    return w, wvt, head_gain.reshape(1, n_q + n_k).astype(F32)


def _ones_block_diag():
    idx = jnp.arange(MXU_DIM) // HEAD_DIM
    return (idx[:, None] == idx[None, :]).astype(BF16)


def kernel(x, c, ctx, c_ctx, norm1_g, norm2_g, mod_w, mod_b, mlp_up, mlp_down,
           gqa_w_qkv, gqa_q_g, gqa_k_g, gqa_w_o,
           conv_w_pw1, conv_b_pw1, conv_w_dw, conv_b_dw, conv_ln_g, conv_ln_b, conv_w_pw2, conv_b_pw2,
           diff_w_qkv, diff_q_g, diff_k_g, diff_lam_q1, diff_lam_k1, diff_lam_q2, diff_lam_k2,
           diff_subln_g, diff_w_o,
           swa_w_qkv, swa_q_g, swa_k_g, swa_sink, swa_w_o):
    b, t, d = x.shape
    n_ctx = ctx.shape[1]
    depth = norm1_g.shape[0]
    assert depth == N_MIXERS and t % GRID_W == 0 and t % n_ctx == 0

    tm = _pick(t, (512, 256, 128))
    tm_proj = _pick(t, (1024, 512, 256, 128))
    tq_win = _pick(t, (256, 128))
    tq_gqa = _pick(t, (1024, 512, 256, 128))
    tq_diff = _pick(t, (2048, 1024, 512, 256, 128))
    t_all = n_ctx + t
    tk = _pick(t_all, (768, 512, 256, 128))
    tkc = _pick(n_ctx, (768, 512, 256, 128))

    cv = jnp.concatenate([c, c_ctx[None], jnp.zeros((8 - b - 1, d), F32)], axis=0)
    mod_all = _mod_call(cv, mod_w, mod_b).reshape(depth, 8, 6, d)
    rope = _rope_tables(t)
    ones_bd = _ones_block_diag()
    row = lambda v: v.reshape(1, -1).astype(F32)

    xc = ctx
    for i in range(depth):
        m, j = i % N_MIXERS, i // N_MIXERS
        need_ctx = i < depth - 1
        mod_l = mod_all[i, :b]
        mod_c = jnp.broadcast_to(mod_all[i, b], (b, 6, d))
        g1, g2 = row(norm1_g[i]), row(norm2_g[i])
        w_up, w_down = mlp_up[i].astype(BF16), mlp_down[i].astype(BF16)
        conv_params = None
        b_o = None
        if m == 1:
            w1, b1 = conv_w_pw1[j].astype(BF16), row(conv_b_pw1[j])
            mixed = _glu_call(x, mod_l, g1, w1, b1, tm=tm_proj)
            mixed_c = _glu_call(xc, mod_c, g1, w1, b1, tm=n_ctx) if need_ctx else None
            conv_params = (conv_w_dw[j].astype(F32), row(conv_b_dw[j]), row(conv_ln_g[j]), row(conv_ln_b[j]))
            w_o, b_o = conv_w_pw2[j].astype(BF16), row(conv_b_pw2[j])
        else:
            if m == 2:
                w_qkv, q_g, k_g, w_o = diff_w_qkv[j], diff_q_g[j], diff_k_g[j], diff_w_o[j]
                n_q = n_k = w_qkv.shape[1] // 3
                q_half = tuple(h % 2 for h in range(n_q // HEAD_DIM))
            else:
                w_qkv, q_g, k_g, w_o = ((gqa_w_qkv[j], gqa_q_g[j], gqa_k_g[j], gqa_w_o[j]) if m == 0 else
                                        (swa_w_qkv[j], swa_q_g[j], swa_k_g[j], swa_w_o[j]))
                n_q = w_o.shape[0]
                n_k = (w_qkv.shape[1] - n_q) // 2
                group = n_q // n_k
                q_half = tuple((h // group) % 2 for h in range(n_q // HEAD_DIM))
            w_o = w_o.astype(BF16)
            w_p, wvt, head_gain = _prep_qkv(w_qkv, q_g, k_g, n_q, n_k)
            proj = functools.partial(_proj_call, gain=g1, w=w_p, wvt=wvt, head_gain=head_gain,
                                     ones_bd=ones_bd, n_q=n_q, n_k=n_k, q_half=q_half)
            kv_buffers = (jnp.zeros((b, t_all, n_k), BF16), jnp.zeros((b, wvt.shape[0], t_all), BF16))
            q, *kv_buffers = proj(x, mod_l, rope=rope, tm=tm_proj, kv_buffers=kv_buffers, first_row=0)
            qc, k_all, vt_all = proj(xc, mod_c, rope=None, tm=n_ctx, kv_buffers=kv_buffers, first_row=t)
            lat_keys = dict(tk=tk, key_rows=t_all, key_block=0)
            ctx_keys = dict(tk=tkc, key_rows=n_ctx, key_block=t // n_ctx)
            mixed_c = None
            if m == 0:
                mixed = _gqa_call(q, k_all, vt_all, group=group, tq=tq_gqa, **lat_keys)
                if need_ctx:
                    mixed_c = _gqa_call(qc, k_all, vt_all, group=group, tq=n_ctx, **ctx_keys)
            elif m == 2:
                lam_init = 0.8 - 0.6 * math.exp(-0.3 * i)
                lam_vecs = jnp.stack([diff_lam_q1[j], diff_lam_k1[j], diff_lam_q2[j], diff_lam_k2[j]]).astype(F32)
                sg = diff_subln_g[j].reshape(-1, 1).astype(F32)
                mixed = _diff_call(q, k_all, vt_all, lam_vecs, sg, lam_init=lam_init, tq=tq_diff, **lat_keys)
                if need_ctx:
                    mixed_c = _diff_call(qc, k_all, vt_all, lam_vecs, sg, lam_init=lam_init, tq=n_ctx, **ctx_keys)
            else:
                assert not need_ctx, "windowed layer with a context update is not supported"
                sink_rows = jnp.repeat(swa_sink[j].astype(F32).reshape(n_k // HEAD_DIM, 1, group), tq_win, axis=2)
                mixed = _window_call(q, k_all, vt_all, sink_rows, group=group, n_ctx=n_ctx, tq=tq_win)
        x = _post_call(x, mixed, mod_l, w_o, b_o, g2, w_up, w_down, conv_params, tm=tm)
        if need_ctx:
            xc = _post_call(xc, mixed_c, mod_c, w_o, b_o, g2, w_up, w_down, conv_params, tm=n_ctx)
    return x
```

```python
import functools
import math

import jax
import jax.numpy as jnp
from jax import lax
from jax.experimental import pallas as pl
from jax.experimental.pallas import tpu as pltpu

F32 = jnp.float32
BF16 = jnp.bfloat16

GRID_W = 64
HEAD_DIM = 64
N_MIXERS = 4
CONV_WIDTH = 31
WINDOW = 128
ROPE_THETA = 10000.0
EPS = 1e-6
NEG_INF = -1e30
LOG2E = math.log2(math.e)

LANES = 128
MXU_DIM = 256
VMEM_LIMIT_BYTES = 56 * 1024 * 1024

CONV_HALO = 16
CONV_ROWS = 128


def _params(n_axes):
    return pltpu.CompilerParams(dimension_semantics=("parallel",) * n_axes,
                                vmem_limit_bytes=VMEM_LIMIT_BYTES)


def _resident(shape):
    nd = len(shape)
    return pl.BlockSpec(shape, lambda *_: (0,) * nd, pipeline_mode=pl.Buffered(1))


def _pick(n, candidates):
    for cand in candidates:
        if n % cand == 0:
            return cand
    return n


def _modulate(x, gain, shift, scale):
    ms = jnp.mean(x * x, axis=-1, keepdims=True)
    return (x * lax.rsqrt(ms + EPS) * gain) * (1.0 + scale) + shift


def _dot(a, b):
    return jnp.dot(a, b, preferred_element_type=F32)


def _dot_nt(a, b):
    return lax.dot_general(a, b, (((1,), (1,)), ((), ())), preferred_element_type=F32)


def _mod_kernel(cv_ref, w_ref, b_ref, o_ref):
    cv = cv_ref[...]
    s = cv * jax.nn.sigmoid(cv)
    o_ref[0] = _dot(s.astype(BF16), w_ref[0].astype(BF16)) + b_ref[0]


def _mod_call(cv, mod_w, mod_b):
    depth, d, n = mod_w.shape
    rows = cv.shape[0]
    tn = _pick(n, (3072, 1536, 1024, 512))
    return pl.pallas_call(
        _mod_kernel,
        out_shape=jax.ShapeDtypeStruct((depth, rows, n), F32),
        grid=(depth, n // tn),
        in_specs=[pl.BlockSpec((rows, d), lambda i, j: (0, 0)),
                  pl.BlockSpec((1, d, tn), lambda i, j: (i, 0, j)),
                  pl.BlockSpec((1, 1, tn), lambda i, j: (i, 0, j))],
        out_specs=pl.BlockSpec((1, rows, tn), lambda i, j: (i, 0, j)),
        compiler_params=_params(2),
        name="modulation",
    )(cv, mod_w, mod_b.reshape(depth, 1, n))


def _proj_kernel(*refs, n_q, n_k, q_half, use_rope, n_alias):
    q_ref, k_ref, vt_ref = refs[-3:]
    refs = refs[:len(refs) - 3 - n_alias]
    if use_rope:
        (x_ref, mod_ref, g_ref, w_ref, wvt_ref, hg_ref, e_ref, cos_ref, sin_ref) = refs
    else:
        (x_ref, mod_ref, g_ref, w_ref, wvt_ref, hg_ref, e_ref) = refs
    mod = mod_ref[0]
    tm_full = x_ref.shape[1]
    n_qk = n_q + n_k
    ones_bd = e_ref[...]
    n_parts = tm_full // MXU_DIM if tm_full % MXU_DIM == 0 else 1
    tm = tm_full // n_parts
    lane = lax.broadcasted_iota(jnp.int32, (tm, LANES), 1)
    even = (lane % 2) == 0
    low = lane < HEAD_DIM
    for part in range(n_parts):
        rows = slice(part * tm, (part + 1) * tm)
        h = _modulate(x_ref[0, rows, :], g_ref[...], mod[0:1], mod[1:2]).astype(BF16)
        qk = _dot(h, w_ref[...])
        vt_ref[0, :, rows] = _dot_nt(wvt_ref[...], h).astype(BF16)
        sq = qk * qk
        hi = sq.astype(BF16)
        lo = (sq - hi.astype(F32)).astype(BF16)
        ss = jnp.concatenate(
            [_dot(hi[:, t:t + MXU_DIM], ones_bd) + _dot(lo[:, t:t + MXU_DIM], ones_bd)
             for t in range(0, n_qk, MXU_DIM)], axis=1)
        qk = qk * lax.rsqrt(ss * (1.0 / HEAD_DIM) + EPS) * hg_ref[...]
        tiles = [qk[:, t:t + LANES] for t in range(0, n_qk, LANES)]
        if use_rope:
            cos = cos_ref[rows, :]
            sin = sin_ref[rows, :]
            tiles = [t * cos + jnp.where(even, pltpu.roll(t, LANES - 1, axis=1), pltpu.roll(t, 1, axis=1)) * sin
                     for t in tiles]
        for j in range(n_q // HEAD_DIM):
            t = tiles[j // 2]
            if j % 2 != q_half[j]:
                t = pltpu.roll(t, HEAD_DIM, axis=1)
            keep = low if q_half[j] == 0 else jnp.logical_not(low)
            q_ref[0, j, rows, :] = jnp.where(keep, t, 0.0).astype(BF16)
        k_ref[0, rows, :] = jnp.concatenate(tiles[n_q // LANES:], axis=1).astype(BF16)


def _proj_call(x, mod, gain, w, wvt, head_gain, ones_bd, rope, *, n_q, n_k, q_half, tm, kv_buffers,
               first_row):
    b, t, d = x.shape
    n_v = wvt.shape[0]
    t_keys = kv_buffers[0].shape[1]
    hq = n_q // HEAD_DIM
    use_rope = rope is not None
    in_specs = [pl.BlockSpec((1, tm, d), lambda bi, j: (bi, j, 0)),
                pl.BlockSpec((1, 6, d), lambda bi, j: (bi, 0, 0)),
                _resident((1, d)),
                _resident((d, n_q + n_k)),
                _resident((n_v, d)),
                _resident((1, n_q + n_k)),
                _resident((MXU_DIM, MXU_DIM))]
    args = [x, mod, gain, w, wvt, head_gain, ones_bd]
    if use_rope:
        in_specs += [pl.BlockSpec((tm, LANES), lambda bi, j: (j, 0)),
                     pl.BlockSpec((tm, LANES), lambda bi, j: (j, 0))]
        args += list(rope)
    off = first_row // tm
    assert off * tm == first_row and first_row + t <= t_keys
    aliases = {len(args): 1, len(args) + 1: 2}
    in_specs += [pl.BlockSpec(memory_space=pl.ANY), pl.BlockSpec(memory_space=pl.ANY)]
    args += list(kv_buffers)
    return pl.pallas_call(
        functools.partial(_proj_kernel, n_q=n_q, n_k=n_k, q_half=q_half, use_rope=use_rope,
                          n_alias=len(aliases)),
        out_shape=(jax.ShapeDtypeStruct((b, hq, t, LANES), BF16),
                   jax.ShapeDtypeStruct((b, t_keys, n_k), BF16),
                   jax.ShapeDtypeStruct((b, n_v, t_keys), BF16)),
        grid=(b, t // tm),
        in_specs=in_specs,
        out_specs=(pl.BlockSpec((1, hq, tm, LANES), lambda bi, j: (bi, 0, j, 0)),
                   pl.BlockSpec((1, tm, n_k), lambda bi, j: (bi, j + off, 0)),
                   pl.BlockSpec((1, n_v, tm), lambda bi, j: (bi, 0, j + off))),
        input_output_aliases=aliases,
        compiler_params=_params(2),
        name="attn_proj",
    )(*args)


ONES_ROWS = 16
FLASH_UNROLL = 4


def _chunk_rows(j, tk):
    return pl.ds(j * tk if isinstance(j, int) else pl.multiple_of(j * tk, tk), tk)


def _flash_loop(q_stacked, load_k, load_vt, qt_ref, s_ref, cm_ref, m_ref, acc_ref, *, n_kv, tk,
                mask_fn=None, sink=None, vt_rows_fn=None):
    v_rows = acc_ref.shape[0] - ONES_ROWS
    qt_ref[...] = q_stacked.astype(F32).T.astype(BF16)
    acc_ref[0:v_rows, :] = jnp.zeros((v_rows, acc_ref.shape[1]), F32)
    if sink is None:
        m_ref[...] = jnp.full(m_ref.shape, NEG_INF, F32)
        acc_ref[v_rows:, :] = jnp.zeros((ONES_ROWS, acc_ref.shape[1]), F32)
    else:
        m_ref[...] = sink
        acc_ref[v_rows:, :] = jnp.ones((ONES_ROWS, acc_ref.shape[1]), F32)
    ones = jnp.ones((ONES_ROWS, tk), BF16)
    m_cols = s_ref.shape[2]
    col_blocks = [slice(c0, c0 + MXU_DIM) for c0 in range(0, m_cols, MXU_DIM)]

    def produce(kc, keep, slot, cols):
        s = _dot(kc, qt_ref[:, cols])
        if keep is not None:
            s = jnp.where(keep, s, NEG_INF)
        s_ref[slot, :, cols] = s
        cm_ref[slot, :, cols] = jnp.max(s.reshape(tk // 8, 8, MXU_DIM), axis=0)

    def consume(vt1, slot, cols):
        m_prev = m_ref[:, cols]
        m_new = jnp.maximum(m_prev, jnp.max(cm_ref[slot, :, cols], axis=0, keepdims=True))
        alpha = jnp.exp2(m_prev - m_new)
        p = jnp.exp2(s_ref[slot, :, cols] - m_new).astype(BF16)
        acc_ref[:, cols] = alpha * acc_ref[:, cols] + _dot(vt1, p)
        m_ref[:, cols] = m_new

    def step(j_prod, slot_prod, j_cons, slot_cons):
        if j_prod is not None:
            kc = load_k(j_prod)
            keep = mask_fn(j_prod) if mask_fn is not None else None
        if j_cons is not None:
            vt_chunk = load_vt(j_cons)
            with_ones = lambda cols: jnp.concatenate(
                [vt_chunk if vt_rows_fn is None else vt_chunk[vt_rows_fn(cols)], ones], axis=0)
        for cols in col_blocks:
            if j_prod is not None:
                produce(kc, keep, slot_prod, cols)
            if j_cons is not None:
                consume(with_ones(cols), slot_cons, cols)

    step(0, 0, None, None)
    trips = (n_kv - 1) // FLASH_UNROLL if n_kv - 1 > FLASH_UNROLL else 0

    def body(trip, carry):
        j0 = trip * FLASH_UNROLL
        for u in range(FLASH_UNROLL):
            step(j0 + u + 1, (u + 1) % 2, j0 + u, u % 2)
        return carry

    if trips:
        lax.fori_loop(0, trips, body, 0)
    for j in range(trips * FLASH_UNROLL, n_kv - 1):
        step(j + 1, (j + 1) % 2, j, j % 2)
    step(None, None, n_kv - 1, (n_kv - 1) % 2)


def _store_heads_t(o_ref, ot, tq, group):
    for gg in range(group // 2):
        slab = jnp.concatenate([ot[:, (2 * gg) * tq:(2 * gg + 1) * tq],
                                ot[:, (2 * gg + 1) * tq:(2 * gg + 2) * tq]], axis=0)
        o_ref[0, :, gg * LANES:(gg + 1) * LANES] = slab.T.astype(o_ref.dtype)


def _kv_specs(key_rows, key_block, v_rows, kv_per_tile):
    return [pl.BlockSpec((1, key_rows, LANES), lambda bi, h, i: (bi, key_block, h // kv_per_tile)),
            pl.BlockSpec((1, v_rows, key_rows), lambda bi, h, i: (bi, h, key_block))]


def _flash_scratch(m_cols, tk, v_rows):
    return [pltpu.VMEM((LANES, m_cols), BF16),
            pltpu.VMEM((2, tk, m_cols), F32),
            pltpu.VMEM((2, 8, m_cols), F32),
            pltpu.VMEM((1, m_cols), F32),
            pltpu.VMEM((v_rows + ONES_ROWS, m_cols), F32)]


def _gqa_kernel(q_ref, k_ref, vt_ref, o_ref, qt_ref, s_ref, cm_ref, m_ref, acc_ref, *, n_kv, tk):
    group, tq = q_ref.shape[1:3]
    q_stacked = q_ref[0].reshape(group * tq, LANES)
    load_k = lambda j: k_ref[0, _chunk_rows(j, tk), :]
    load_vt = lambda j: vt_ref[0, :, _chunk_rows(j, tk)]
    _flash_loop(q_stacked, load_k, load_vt, qt_ref, s_ref, cm_ref, m_ref, acc_ref, n_kv=n_kv, tk=tk)
    ot = acc_ref[0:HEAD_DIM, :] / acc_ref[HEAD_DIM:HEAD_DIM + 1, :]
    _store_heads_t(o_ref, ot, tq, group)


def _gqa_call(q, k_all, vt_all, *, group, tq, tk, key_rows, key_block):
    b, hq, t, _ = q.shape
    return pl.pallas_call(
        functools.partial(_gqa_kernel, n_kv=key_rows // tk, tk=tk),
        out_shape=jax.ShapeDtypeStruct((b, t, hq * HEAD_DIM), BF16),
        grid=(b, hq // group, t // tq),
        in_specs=[pl.BlockSpec((1, group, tq, LANES), lambda bi, h, i: (bi, h, i, 0))]
                 + _kv_specs(key_rows, key_block, HEAD_DIM, LANES // HEAD_DIM),
        out_specs=pl.BlockSpec((1, tq, group * HEAD_DIM), lambda bi, h, i: (bi, i, h)),
        scratch_shapes=_flash_scratch(group * tq, tk, HEAD_DIM),
        compiler_params=_params(3),
        name="gqa_attention",
    )(q, k_all, vt_all)


def _diff_kernel(q_ref, k_ref, vt_ref, lam_ref, sg_ref, o_ref, qt_ref, s_ref, cm_ref, m_ref, acc_ref, *,
                 lam_init, n_kv, tk):
    tq = q_ref.shape[2]
    dv = vt_ref.shape[1]
    q_stacked = q_ref[0].reshape(2 * tq, LANES)
    load_k = lambda j: k_ref[0, _chunk_rows(j, tk), :]
    load_vt = lambda j: vt_ref[0, :, _chunk_rows(j, tk)]
    _flash_loop(q_stacked, load_k, load_vt, qt_ref, s_ref, cm_ref, m_ref, acc_ref, n_kv=n_kv, tk=tk)
    ot = acc_ref[0:dv, :] / acc_ref[dv:dv + 1, :]
    lv = lam_ref[...]
    lam = (jnp.exp(jnp.sum(lv[0:1] * lv[1:2], axis=1, keepdims=True))
           - jnp.exp(jnp.sum(lv[2:3] * lv[3:4], axis=1, keepdims=True)) + lam_init)
    ot = ot[:, :tq] - lam * ot[:, tq:]
    ms = jnp.mean(ot * ot, axis=0, keepdims=True)
    ot = (ot * lax.rsqrt(ms + EPS) * sg_ref[...]) * (1.0 - lam_init)
    o_ref[0] = ot.T.astype(o_ref.dtype)


def _diff_call(q, k_all, vt_all, lam_vecs, subln_g, *, lam_init, tq, tk, key_rows, key_block):
    b, hq, t, _ = q.shape
    heads = k_all.shape[2] // LANES
    dv = vt_all.shape[1] // heads
    return pl.pallas_call(
        functools.partial(_diff_kernel, lam_init=lam_init, n_kv=key_rows // tk, tk=tk),
        out_shape=jax.ShapeDtypeStruct((b, t, heads * dv), BF16),
        grid=(b, heads, t // tq),
        in_specs=[pl.BlockSpec((1, 2, tq, LANES), lambda bi, h, i: (bi, h, i, 0))]
                 + _kv_specs(key_rows, key_block, dv, 1)
                 + [pl.BlockSpec((4, HEAD_DIM), lambda bi, h, i: (0, 0)),
                    pl.BlockSpec((dv, 1), lambda bi, h, i: (0, 0))],
        out_specs=pl.BlockSpec((1, tq, dv), lambda bi, h, i: (bi, i, h)),
        scratch_shapes=_flash_scratch(2 * tq, tk, dv),
        compiler_params=_params(3),
        name="diff_attention",
    )(q, k_all, vt_all, lam_vecs, subln_g)


def _window_kernel(q_ref, k_ref, vt_ref, sink_ref, o_ref, qt_ref, s_ref, cm_ref, m_ref, acc_ref, *,
                   n_ctx, t_lat, tk):
    heads, tq = q_ref.shape[1:3]
    group = heads // 2
    assert tq == MXU_DIM
    span = tq + 2 * WINDOW
    n_lat = span // tk
    i = pl.program_id(2)
    start = pl.multiple_of(jnp.clip(i * tq - WINDOW, 0, t_lat - span), WINDOW)
    first_row = lambda j: (pl.multiple_of(start + j * tk, WINDOW) if j < n_lat
                           else t_lat + (j - n_lat) * tk)
    load_k = lambda j: k_ref[0, pl.ds(first_row(j), tk), :]
    load_vt = lambda j: vt_ref[0, :, pl.ds(first_row(j), tk)]
    rel = (lax.broadcasted_iota(jnp.int32, (tk, tq), 0) - lax.broadcasted_iota(jnp.int32, (tk, tq), 1)
           + (start - i * tq))

    def mask_fn(j):
        return jnp.abs(rel + j * tk) <= WINDOW if j < n_lat else None

    q_stacked = q_ref[0].reshape(heads * tq, LANES)
    sink = jnp.concatenate([sink_ref[0], sink_ref[1]], axis=1) * LOG2E

    def vt_rows_fn(cols):
        kv = cols.start // (group * tq)
        return slice(kv * HEAD_DIM, (kv + 1) * HEAD_DIM)

    _flash_loop(q_stacked, load_k, load_vt, qt_ref, s_ref, cm_ref, m_ref, acc_ref,
                n_kv=n_lat + n_ctx // tk, tk=tk, mask_fn=mask_fn, sink=sink, vt_rows_fn=vt_rows_fn)
    ot = acc_ref[0:HEAD_DIM, :] / acc_ref[HEAD_DIM:HEAD_DIM + 1, :]
    _store_heads_t(o_ref, ot, tq, heads)


def _window_call(q, k_all, vt_all, sink_rows, *, group, n_ctx, tq):
    b, hq, t, _ = q.shape
    t_all = k_all.shape[1]
    tk = MXU_DIM
    n_chunks = (tq + 2 * WINDOW) // tk + n_ctx // tk
    assert (tq + 2 * WINDOW) % tk == 0 and n_ctx % tk == 0 and n_chunks - 1 <= FLASH_UNROLL
    return pl.pallas_call(
        functools.partial(_window_kernel, n_ctx=n_ctx, t_lat=t, tk=tk),
        out_shape=jax.ShapeDtypeStruct((b, t, hq * HEAD_DIM), BF16),
        grid=(b, hq // (2 * group), t // tq),
        in_specs=[pl.BlockSpec((1, 2 * group, tq, LANES), lambda bi, h, i: (bi, h, i, 0))]
                 + _kv_specs(t_all, 0, 2 * HEAD_DIM, 1)
                 + [pl.BlockSpec((2, 1, group * tq), lambda bi, h, i: (h, 0, 0))],
        out_specs=pl.BlockSpec((1, tq, 2 * group * HEAD_DIM), lambda bi, h, i: (bi, i, h)),
        scratch_shapes=_flash_scratch(2 * group * tq, tk, HEAD_DIM),
        compiler_params=_params(3),
        name="window_attention",
    )(q, k_all, vt_all, sink_rows)


def _glu_kernel(x_ref, mod_ref, g_ref, w_ref, b_ref, u_ref):
    mod = mod_ref[0]
    tm_full, d = x_ref.shape[1:]
    n_parts = tm_full // MXU_DIM if tm_full % MXU_DIM == 0 else 1
    tm = tm_full // n_parts
    for part in range(n_parts):
        rows = slice(part * tm, (part + 1) * tm)
        h = _modulate(x_ref[0, rows, :], g_ref[...], mod[0:1], mod[1:2]).astype(BF16)
        ag = _dot(h, w_ref[...]) + b_ref[...]
        u_ref[0, rows, :] = (ag[:, :d] * jax.nn.sigmoid(ag[:, d:])).astype(u_ref.dtype)


def _glu_call(x, mod, gain, w, bias, *, tm):
    b, t, d = x.shape
    return pl.pallas_call(
        _glu_kernel,
        out_shape=jax.ShapeDtypeStruct((b, t, d), BF16),
        grid=(b, t // tm),
        in_specs=[pl.BlockSpec((1, tm, d), lambda bi, j: (bi, j, 0)),
                  pl.BlockSpec((1, 6, d), lambda bi, j: (bi, 0, 0)),
                  _resident((1, d)),
                  _resident((d, 2 * d)),
                  _resident((1, 2 * d))],
        out_specs=pl.BlockSpec((1, tm, d), lambda bi, j: (bi, j, 0)),
        compiler_params=_params(2),
        name="conv_glu",
    )(x, mod, gain, w, bias)


def _depthwise_conv(ubuf_ref, cbuf_ref, wdw_ref, bdw_ref, tm, d):
    win_rows = CONV_ROWS + 2 * CONV_HALO
    base = CONV_HALO - CONV_WIDTH // 2
    lane_chunk = LANES

    def step(r, carry):
        r0 = pl.multiple_of(r * CONV_ROWS, CONV_ROWS)
        for c0 in range(0, d, lane_chunk):
            win = ubuf_ref[pl.ds(r0, win_rows), c0:c0 + lane_chunk]
            acc = jnp.zeros((CONV_ROWS, lane_chunk), F32) + bdw_ref[:, c0:c0 + lane_chunk]
            for sub in range(8):
                shifted = win if sub == 0 else pltpu.roll(win, win_rows - sub, axis=0)
                for al in range(0, win_rows - CONV_ROWS + 1, 8):
                    k = al + sub - base
                    if 0 <= k < CONV_WIDTH:
                        acc = acc + wdw_ref[k:k + 1, c0:c0 + lane_chunk] * shifted[al:al + CONV_ROWS]
            cbuf_ref[pl.ds(r0, CONV_ROWS), c0:c0 + lane_chunk] = acc
        return carry

    lax.fori_loop(0, tm // CONV_ROWS, step, 0)


def _post_kernel(*refs, conv, has_bias, d_ff_chunk):
    it = iter(refs)
    x_ref = next(it)
    if conv:
        up_ref, uc_ref, un_ref = next(it), next(it), next(it)
        wdw_ref, bdw_ref, lng_ref, lnb_ref = next(it), next(it), next(it), next(it)
    else:
        o_ref = next(it)
    mod_ref, wo_ref = next(it), next(it)
    bo_ref = next(it) if has_bias else None
    g2_ref, wup_ref, wdn_ref, out_ref = next(it), next(it), next(it), next(it)
    if conv:
        ubuf_ref, cbuf_ref = next(it), next(it)

    x = x_ref[0]
    mod = mod_ref[0]
    tm, d = x.shape
    if conv:
        j = pl.program_id(1)
        prev = jnp.where(j > 0, up_ref[0].astype(F32), 0.0)
        nxt = jnp.where(j < pl.num_programs(1) - 1, un_ref[0].astype(F32), 0.0)
        ubuf_ref[0:CONV_HALO] = prev
        ubuf_ref[CONV_HALO:CONV_HALO + tm] = uc_ref[0].astype(F32)
        ubuf_ref[CONV_HALO + tm:] = nxt
        _depthwise_conv(ubuf_ref, cbuf_ref, wdw_ref, bdw_ref, tm, d)
        cv = cbuf_ref[...]
        mu = jnp.mean(cv, axis=-1, keepdims=True)
        cc = cv - mu
        var = jnp.mean(cc * cc, axis=-1, keepdims=True)
        yn = cc * lax.rsqrt(var + EPS) * lng_ref[...] + lnb_ref[...]
        o = (yn * jax.nn.sigmoid(yn)).astype(BF16)
    else:
        o = o_ref[0]
    y = _dot(o, wo_ref[...])
    if has_bias:
        y = y + bo_ref[...]
    x1 = x + mod[2:3] * y
    h2 = _modulate(x1, g2_ref[...], mod[3:4], mod[4:5]).astype(BF16)
    acc = jnp.zeros((tm, d), F32)
    d_ff = wup_ref.shape[1]
    for c0 in range(0, d_ff, d_ff_chunk):
        up = jnp.maximum(_dot(h2, wup_ref[:, c0:c0 + d_ff_chunk]), 0.0)
        acc = acc + _dot((up * up).astype(BF16), wdn_ref[c0:c0 + d_ff_chunk, :])
    out_ref[0] = x1 + mod[5:6] * acc


def _post_call(x, mixed, mod, w_o, b_o, gain2, w_up, w_down, conv_params, *, tm):
    b, t, d = x.shape
    d_in = w_o.shape[0]
    d_ff = w_up.shape[1]
    conv = conv_params is not None
    has_bias = b_o is not None
    tok = lambda bi, j: (bi, j, 0)
    in_specs = [pl.BlockSpec((1, tm, d), tok)]
    args = [x]
    scratch = []
    if conv:
        hb = tm // CONV_HALO
        last = t // CONV_HALO - 1
        in_specs += [pl.BlockSpec((1, CONV_HALO, d), lambda bi, j: (bi, jnp.maximum(j * hb - 1, 0), 0)),
                     pl.BlockSpec((1, tm, d), tok),
                     pl.BlockSpec((1, CONV_HALO, d), lambda bi, j: (bi, jnp.minimum((j + 1) * hb, last), 0)),
                     _resident((CONV_WIDTH, d)), _resident((1, d)), _resident((1, d)), _resident((1, d))]
        args += [mixed, mixed, mixed] + list(conv_params)
        scratch = [pltpu.VMEM((tm + 2 * CONV_HALO, d), F32), pltpu.VMEM((tm, d), F32)]
    else:
        in_specs += [pl.BlockSpec((1, tm, d_in), tok)]
        args += [mixed]
    in_specs += [pl.BlockSpec((1, 6, d), lambda bi, j: (bi, 0, 0)), _resident((d_in, d))]
    args += [mod, w_o]
    if has_bias:
        in_specs += [_resident((1, d))]
        args += [b_o]
    in_specs += [_resident((1, d)), _resident((d, d_ff)), _resident((d_ff, d))]
    args += [gain2, w_up, w_down]
    return pl.pallas_call(
        functools.partial(_post_kernel, conv=conv, has_bias=has_bias, d_ff_chunk=min(d_ff, 1024)),
        out_shape=jax.ShapeDtypeStruct((b, t, d), F32),
        grid=(b, t // tm),
        in_specs=in_specs,
        out_specs=pl.BlockSpec((1, tm, d), tok),
        scratch_shapes=scratch,
        compiler_params=_params(2),
        name="post_mlp",
    )(*args)


def _rope_tables(t):
    rows = t // GRID_W
    row = jnp.repeat(jnp.arange(rows, dtype=F32), GRID_W)
    col = jnp.tile(jnp.arange(GRID_W, dtype=F32), rows)
    half = HEAD_DIM // 2
    inv = 1.0 / jnp.power(ROPE_THETA, jnp.arange(0, half, 2, dtype=F32) / half)
    ang = jnp.concatenate([row[:, None] * inv, col[:, None] * inv], axis=-1)
    cos = jnp.repeat(jnp.cos(ang), 2, axis=1)
    sin = jnp.repeat(jnp.sin(ang), 2, axis=1) * jnp.tile(jnp.array([-1.0, 1.0], F32), half)
    reps = LANES // HEAD_DIM
    return jnp.tile(cos, (1, reps)), jnp.tile(sin, (1, reps))


def _prep_qkv(w_qkv, q_g, k_g, n_q, n_k):
    hq, hk = n_q // HEAD_DIM, n_k // HEAD_DIM
    w = w_qkv[:, :n_q + n_k].astype(BF16)
    wvt = w_qkv[:, n_q + n_k:].T.astype(BF16)
    scale = LOG2E / math.sqrt(HEAD_DIM)
    head_gain = jnp.concatenate([jnp.tile(q_g * scale, hq), jnp.tile(k_g, hk)])
    return w, wvt, head_gain.reshape(1, n_q + n_k).astype(F32)


def _ones_block_diag():
    idx = jnp.arange(MXU_DIM) // HEAD_DIM
    return (idx[:, None] == idx[None, :]).astype(BF16)


def kernel(x, c, ctx, c_ctx, norm1_g, norm2_g, mod_w, mod_b, mlp_up, mlp_down,
           gqa_w_qkv, gqa_q_g, gqa_k_g, gqa_w_o,
           conv_w_pw1, conv_b_pw1, conv_w_dw, conv_b_dw, conv_ln_g, conv_ln_b, conv_w_pw2, conv_b_pw2,
           diff_w_qkv, diff_q_g, diff_k_g, diff_lam_q1, diff_lam_k1, diff_lam_q2, diff_lam_k2,
           diff_subln_g, diff_w_o,
           swa_w_qkv, swa_q_g, swa_k_g, swa_sink, swa_w_o):
    b, t, d = x.shape
    n_ctx = ctx.shape[1]
    depth = norm1_g.shape[0]
    assert depth == N_MIXERS and t % GRID_W == 0 and t % n_ctx == 0

    tm = _pick(t, (512, 256, 128))
    tm_proj = _pick(t, (1024, 512, 256, 128))
    tq_win = _pick(t, (256, 128))
    tq_gqa = _pick(t, (1024, 512, 256, 128))
    tq_diff = _pick(t, (2048, 1024, 512, 256, 128))
    t_all = n_ctx + t
    tk = _pick(t_all, (768, 512, 256, 128))
    tkc = _pick(n_ctx, (768, 512, 256, 128))

    cv = jnp.concatenate([c, c_ctx[None], jnp.zeros((8 - b - 1, d), F32)], axis=0)
    mod_all = _mod_call(cv, mod_w, mod_b).reshape(depth, 8, 6, d)
    rope = _rope_tables(t)
    ones_bd = _ones_block_diag()
    row = lambda v: v.reshape(1, -1).astype(F32)

    xc = ctx
    for i in range(depth):
        m, j = i % N_MIXERS, i // N_MIXERS
        need_ctx = i < depth - 1
        mod_l = mod_all[i, :b]
        mod_c = jnp.broadcast_to(mod_all[i, b], (b, 6, d))
        g1, g2 = row(norm1_g[i]), row(norm2_g[i])
        w_up, w_down = mlp_up[i].astype(BF16), mlp_down[i].astype(BF16)
        conv_params = None
        b_o = None
        if m == 1:
            w1, b1 = conv_w_pw1[j].astype(BF16), row(conv_b_pw1[j])
            mixed = _glu_call(x, mod_l, g1, w1, b1, tm=tm_proj)
            mixed_c = _glu_call(xc, mod_c, g1, w1, b1, tm=n_ctx) if need_ctx else None
            conv_params = (conv_w_dw[j].astype(F32), row(conv_b_dw[j]), row(conv_ln_g[j]), row(conv_ln_b[j]))
            w_o, b_o = conv_w_pw2[j].astype(BF16), row(conv_b_pw2[j])
        else:
            if m == 2:
                w_qkv, q_g, k_g, w_o = diff_w_qkv[j], diff_q_g[j], diff_k_g[j], diff_w_o[j]
                n_q = n_k = w_qkv.shape[1] // 3
                q_half = tuple(h % 2 for h in range(n_q // HEAD_DIM))
            else:
                w_qkv, q_g, k_g, w_o = ((gqa_w_qkv[j], gqa_q_g[j], gqa_k_g[j], gqa_w_o[j]) if m == 0 else
                                        (swa_w_qkv[j], swa_q_g[j], swa_k_g[j], swa_w_o[j]))
                n_q = w_o.shape[0]
                n_k = (w_qkv.shape[1] - n_q) // 2
                group = n_q // n_k
                q_half = tuple((h // group) % 2 for h in range(n_q // HEAD_DIM))
            w_o = w_o.astype(BF16)
            w_p, wvt, head_gain = _prep_qkv(w_qkv, q_g, k_g, n_q, n_k)
            proj = functools.partial(_proj_call, gain=g1, w=w_p, wvt=wvt, head_gain=head_gain,
                                     ones_bd=ones_bd, n_q=n_q, n_k=n_k, q_half=q_half)
            kv_buffers = (jnp.zeros((b, t_all, n_k), BF16), jnp.zeros((b, wvt.shape[0], t_all), BF16))
            q, *kv_buffers = proj(x, mod_l, rope=rope, tm=tm_proj, kv_buffers=kv_buffers, first_row=0)
            qc, k_all, vt_all = proj(xc, mod_c, rope=None, tm=n_ctx, kv_buffers=kv_buffers, first_row=t)
            lat_keys = dict(tk=tk, key_rows=t_all, key_block=0)
            ctx_keys = dict(tk=tkc, key_rows=n_ctx, key_block=t // n_ctx)
            mixed_c = None
            if m == 0:
                mixed = _gqa_call(q, k_all, vt_all, group=group, tq=tq_gqa, **lat_keys)
                if need_ctx:
                    mixed_c = _gqa_call(qc, k_all, vt_all, group=group, tq=n_ctx, **ctx_keys)
            elif m == 2:
                lam_init = 0.8 - 0.6 * math.exp(-0.3 * i)
                lam_vecs = jnp.stack([diff_lam_q1[j], diff_lam_k1[j], diff_lam_q2[j], diff_lam_k2[j]]).astype(F32)
                sg = diff_subln_g[j].reshape(-1, 1).astype(F32)
                mixed = _diff_call(q, k_all, vt_all, lam_vecs, sg, lam_init=lam_init, tq=tq_diff, **lat_keys)
                if need_ctx:
                    mixed_c = _diff_call(qc, k_all, vt_all, lam_vecs, sg, lam_init=lam_init, tq=n_ctx, **ctx_keys)
            else:
                assert not need_ctx, "windowed layer with a context update is not supported"
                sink_rows = jnp.repeat(swa_sink[j].astype(F32).reshape(n_k // HEAD_DIM, 1, group), tq_win, axis=2)
                mixed = _window_call(q, k_all, vt_all, sink_rows, group=group, n_ctx=n_ctx, tq=tq_win)
        x = _post_call(x, mixed, mod_l, w_o, b_o, g2, w_up, w_down, conv_params, tm=tm)
        if need_ctx:
            xc = _post_call(xc, mixed_c, mod_c, w_o, b_o, g2, w_up, w_down, conv_params, tm=n_ctx)
    return x
```

```python
import functools
import math

import jax
import jax.numpy as jnp
from jax import lax
from jax.experimental import pallas as pl
from jax.experimental.pallas import tpu as pltpu

F32 = jnp.float32
BF16 = jnp.bfloat16

GRID_W = 64
HEAD_DIM = 64
N_MIXERS = 4
CONV_WIDTH = 31
WINDOW = 128
ROPE_THETA = 10000.0
EPS = 1e-6
NEG_INF = -1e30
LOG2E = math.log2(math.e)

LANES = 128
MXU_DIM = 256
VMEM_LIMIT_BYTES = 56 * 1024 * 1024

CONV_HALO = 16
CONV_ROWS = 128


def _params(n_axes):
    return pltpu.CompilerParams(dimension_semantics=("parallel",) * n_axes,
                                vmem_limit_bytes=VMEM_LIMIT_BYTES)


def _resident(shape):
    nd = len(shape)
    return pl.BlockSpec(shape, lambda *_: (0,) * nd, pipeline_mode=pl.Buffered(1))


def _pick(n, candidates):
    for cand in candidates:
        if n % cand == 0:
            return cand
    return n


def _modulate(x, gain, shift, scale):
    ms = jnp.mean(x * x, axis=-1, keepdims=True)
    return (x * lax.rsqrt(ms + EPS) * gain) * (1.0 + scale) + shift


def _dot(a, b):
    return jnp.dot(a, b, preferred_element_type=F32)


def _dot_nt(a, b):
    return lax.dot_general(a, b, (((1,), (1,)), ((), ())), preferred_element_type=F32)


def _mod_kernel(cv_ref, w_ref, b_ref, o_ref):
    cv = cv_ref[...]
    s = cv * jax.nn.sigmoid(cv)
    o_ref[0] = _dot(s.astype(BF16), w_ref[0].astype(BF16)) + b_ref[0]


def _mod_call(cv, mod_w, mod_b):
    depth, d, n = mod_w.shape
    rows = cv.shape[0]
    tn = _pick(n, (3072, 1536, 1024, 512))
    return pl.pallas_call(
        _mod_kernel,
        out_shape=jax.ShapeDtypeStruct((depth, rows, n), F32),
        grid=(depth, n // tn),
        in_specs=[pl.BlockSpec((rows, d), lambda i, j: (0, 0)),
                  pl.BlockSpec((1, d, tn), lambda i, j: (i, 0, j)),
                  pl.BlockSpec((1, 1, tn), lambda i, j: (i, 0, j))],
        out_specs=pl.BlockSpec((1, rows, tn), lambda i, j: (i, 0, j)),
        compiler_params=_params(2),
        name="modulation",
    )(cv, mod_w, mod_b.reshape(depth, 1, n))


def _proj_kernel(*refs, n_q, n_k, q_half, use_rope, n_alias):
    q_ref, k_ref, vt_ref = refs[-3:]
    refs = refs[:len(refs) - 3 - n_alias]
    if use_rope:
        (x_ref, mod_ref, g_ref, w_ref, wvt_ref, hg_ref, e_ref, cos_ref, sin_ref) = refs
    else:
        (x_ref, mod_ref, g_ref, w_ref, wvt_ref, hg_ref, e_ref) = refs
    mod = mod_ref[0]
    tm_full = x_ref.shape[1]
    n_qk = n_q + n_k
    ones_bd = e_ref[...]
    n_parts = tm_full // MXU_DIM if tm_full % MXU_DIM == 0 else 1
    tm = tm_full // n_parts
    lane = lax.broadcasted_iota(jnp.int32, (tm, LANES), 1)
    even = (lane % 2) == 0
    low = lane < HEAD_DIM
    for part in range(n_parts):
        rows = slice(part * tm, (part + 1) * tm)
        h = _modulate(x_ref[0, rows, :], g_ref[...], mod[0:1], mod[1:2]).astype(BF16)
        qk = _dot(h, w_ref[...])
        vt_ref[0, :, rows] = _dot_nt(wvt_ref[...], h).astype(BF16)
        sq = qk * qk
        hi = sq.astype(BF16)
        lo = (sq - hi.astype(F32)).astype(BF16)
        ss = jnp.concatenate(
            [_dot(hi[:, t:t + MXU_DIM], ones_bd) + _dot(lo[:, t:t + MXU_DIM], ones_bd)
             for t in range(0, n_qk, MXU_DIM)], axis=1)
        qk = qk * lax.rsqrt(ss * (1.0 / HEAD_DIM) + EPS) * hg_ref[...]
        tiles = [qk[:, t:t + LANES] for t in range(0, n_qk, LANES)]
        if use_rope:
            cos = cos_ref[rows, :]
            sin = sin_ref[rows, :]
            tiles = [t * cos + jnp.where(even, pltpu.roll(t, LANES - 1, axis=1), pltpu.roll(t, 1, axis=1)) * sin
                     for t in tiles]
        for j in range(n_q // HEAD_DIM):
            t = tiles[j // 2]
            if j % 2 != q_half[j]:
                t = pltpu.roll(t, HEAD_DIM, axis=1)
            keep = low if q_half[j] == 0 else jnp.logical_not(low)
            q_ref[0, j, rows, :] = jnp.where(keep, t, 0.0).astype(BF16)
        k_ref[0, rows, :] = jnp.concatenate(tiles[n_q // LANES:], axis=1).astype(BF16)


def _proj_call(x, mod, gain, w, wvt, head_gain, ones_bd, rope, *, n_q, n_k, q_half, tm, kv_buffers,
               first_row):
    b, t, d = x.shape
    n_v = wvt.shape[0]
    t_keys = kv_buffers[0].shape[1]
    hq = n_q // HEAD_DIM
    use_rope = rope is not None
    in_specs = [pl.BlockSpec((1, tm, d), lambda bi, j: (bi, j, 0)),
                pl.BlockSpec((1, 6, d), lambda bi, j: (bi, 0, 0)),
                _resident((1, d)),
                _resident((d, n_q + n_k)),
                _resident((n_v, d)),
                _resident((1, n_q + n_k)),
                _resident((MXU_DIM, MXU_DIM))]
    args = [x, mod, gain, w, wvt, head_gain, ones_bd]
    if use_rope:
        in_specs += [pl.BlockSpec((tm, LANES), lambda bi, j: (j, 0)),
                     pl.BlockSpec((tm, LANES), lambda bi, j: (j, 0))]
        args += list(rope)
    off = first_row // tm
    assert off * tm == first_row and first_row + t <= t_keys
    aliases = {len(args): 1, len(args) + 1: 2}
    in_specs += [pl.BlockSpec(memory_space=pl.ANY), pl.BlockSpec(memory_space=pl.ANY)]
    args += list(kv_buffers)
    return pl.pallas_call(
        functools.partial(_proj_kernel, n_q=n_q, n_k=n_k, q_half=q_half, use_rope=use_rope,
                          n_alias=len(aliases)),
        out_shape=(jax.ShapeDtypeStruct((b, hq, t, LANES), BF16),
                   jax.ShapeDtypeStruct((b, t_keys, n_k), BF16),
                   jax.ShapeDtypeStruct((b, n_v, t_keys), BF16)),
        grid=(b, t // tm),
        in_specs=in_specs,
        out_specs=(pl.BlockSpec((1, hq, tm, LANES), lambda bi, j: (bi, 0, j, 0)),
                   pl.BlockSpec((1, tm, n_k), lambda bi, j: (bi, j + off, 0)),
                   pl.BlockSpec((1, n_v, tm), lambda bi, j: (bi, 0, j + off))),
        input_output_aliases=aliases,
        compiler_params=_params(2),
        name="attn_proj",
    )(*args)


ONES_ROWS = 16
FLASH_UNROLL = 4


def _chunk_rows(j, tk):
    return pl.ds(j * tk if isinstance(j, int) else pl.multiple_of(j * tk, tk), tk)


def _flash_loop(q_stacked, load_k, load_vt, qt_ref, s_ref, cm_ref, m_ref, acc_ref, *, n_kv, tk,
                mask_fn=None, sink=None, vt_rows_fn=None):
    v_rows = acc_ref.shape[0] - ONES_ROWS
    qt_ref[...] = q_stacked.astype(F32).T.astype(BF16)
    acc_ref[0:v_rows, :] = jnp.zeros((v_rows, acc_ref.shape[1]), F32)
    if sink is None:
        m_ref[...] = jnp.full(m_ref.shape, NEG_INF, F32)
        acc_ref[v_rows:, :] = jnp.zeros((ONES_ROWS, acc_ref.shape[1]), F32)
    else:
        m_ref[...] = sink
        acc_ref[v_rows:, :] = jnp.ones((ONES_ROWS, acc_ref.shape[1]), F32)
    ones = jnp.ones((ONES_ROWS, tk), BF16)
    m_cols = s_ref.shape[2]
    col_blocks = [slice(c0, c0 + MXU_DIM) for c0 in range(0, m_cols, MXU_DIM)]

    def produce(kc, keep, slot, cols):
        s = _dot(kc, qt_ref[:, cols])
        if keep is not None:
            s = jnp.where(keep, s, NEG_INF)
        s_ref[slot, :, cols] = s
        cm_ref[slot, :, cols] = jnp.max(s.reshape(tk // 8, 8, MXU_DIM), axis=0)

    def consume(vt1, slot, cols):
        m_prev = m_ref[:, cols]
        m_new = jnp.maximum(m_prev, jnp.max(cm_ref[slot, :, cols], axis=0, keepdims=True))
        alpha = jnp.exp2(m_prev - m_new)
        p = jnp.exp2(s_ref[slot, :, cols] - m_new).astype(BF16)
        acc_ref[:, cols] = alpha * acc_ref[:, cols] + _dot(vt1, p)
        m_ref[:, cols] = m_new

    def step(j_prod, slot_prod, j_cons, slot_cons):
        if j_prod is not None:
            kc = load_k(j_prod)
            keep = mask_fn(j_prod) if mask_fn is not None else None
        if j_cons is not None:
            vt_chunk = load_vt(j_cons)
            with_ones = lambda cols: jnp.concatenate(
                [vt_chunk if vt_rows_fn is None else vt_chunk[vt_rows_fn(cols)], ones], axis=0)
        for cols in col_blocks:
            if j_prod is not None:
                produce(kc, keep, slot_prod, cols)
            if j_cons is not None:
                consume(with_ones(cols), slot_cons, cols)

    step(0, 0, None, None)
    trips = (n_kv - 1) // FLASH_UNROLL if n_kv - 1 > FLASH_UNROLL else 0

    def body(trip, carry):
        j0 = trip * FLASH_UNROLL
        for u in range(FLASH_UNROLL):
            step(j0 + u + 1, (u + 1) % 2, j0 + u, u % 2)
        return carry

    if trips:
        lax.fori_loop(0, trips, body, 0)
    for j in range(trips * FLASH_UNROLL, n_kv - 1):
        step(j + 1, (j + 1) % 2, j, j % 2)
    step(None, None, n_kv - 1, (n_kv - 1) % 2)


def _store_heads_t(o_ref, ot, tq, group):
    for gg in range(group // 2):
        slab = jnp.concatenate([ot[:, (2 * gg) * tq:(2 * gg + 1) * tq],
                                ot[:, (2 * gg + 1) * tq:(2 * gg + 2) * tq]], axis=0)
        o_ref[0, :, gg * LANES:(gg + 1) * LANES] = slab.T.astype(o_ref.dtype)


def _kv_specs(key_rows, key_block, v_rows, kv_per_tile):
    return [pl.BlockSpec((1, key_rows, LANES), lambda bi, h, i: (bi, key_block, h // kv_per_tile)),
            pl.BlockSpec((1, v_rows, key_rows), lambda bi, h, i: (bi, h, key_block))]


def _flash_scratch(m_cols, tk, v_rows):
    return [pltpu.VMEM((LANES, m_cols), BF16),
            pltpu.VMEM((2, tk, m_cols), F32),
            pltpu.VMEM((2, 8, m_cols), F32),
            pltpu.VMEM((1, m_cols), F32),
            pltpu.VMEM((v_rows + ONES_ROWS, m_cols), F32)]


def _gqa_kernel(q_ref, k_ref, vt_ref, o_ref, qt_ref, s_ref, cm_ref, m_ref, acc_ref, *, n_kv, tk):
    group, tq = q_ref.shape[1:3]
    q_stacked = q_ref[0].reshape(group * tq, LANES)
    load_k = lambda j: k_ref[0, _chunk_rows(j, tk), :]
    load_vt = lambda j: vt_ref[0, :, _chunk_rows(j, tk)]
    _flash_loop(q_stacked, load_k, load_vt, qt_ref, s_ref, cm_ref, m_ref, acc_ref, n_kv=n_kv, tk=tk)
    ot = acc_ref[0:HEAD_DIM, :] / acc_ref[HEAD_DIM:HEAD_DIM + 1, :]
    _store_heads_t(o_ref, ot, tq, group)


def _gqa_call(q, k_all, vt_all, *, group, tq, tk, key_rows, key_block):
    b, hq, t, _ = q.shape
    return pl.pallas_call(
        functools.partial(_gqa_kernel, n_kv=key_rows // tk, tk=tk),
        out_shape=jax.ShapeDtypeStruct((b, t, hq * HEAD_DIM), BF16),
        grid=(b, hq // group, t // tq),
        in_specs=[pl.BlockSpec((1, group, tq, LANES), lambda bi, h, i: (bi, h, i, 0))]
                 + _kv_specs(key_rows, key_block, HEAD_DIM, LANES // HEAD_DIM),
        out_specs=pl.BlockSpec((1, tq, group * HEAD_DIM), lambda bi, h, i: (bi, i, h)),
        scratch_shapes=_flash_scratch(group * tq, tk, HEAD_DIM),
        compiler_params=_params(3),
        name="gqa_attention",
    )(q, k_all, vt_all)


def _diff_kernel(q_ref, k_ref, vt_ref, lam_ref, sg_ref, o_ref, qt_ref, s_ref, cm_ref, m_ref, acc_ref, *,
                 lam_init, n_kv, tk):
    tq = q_ref.shape[2]
    dv = vt_ref.shape[1]
    q_stacked = q_ref[0].reshape(2 * tq, LANES)
    load_k = lambda j: k_ref[0, _chunk_rows(j, tk), :]
    load_vt = lambda j: vt_ref[0, :, _chunk_rows(j, tk)]
    _flash_loop(q_stacked, load_k, load_vt, qt_ref, s_ref, cm_ref, m_ref, acc_ref, n_kv=n_kv, tk=tk)
    ot = acc_ref[0:dv, :] / acc_ref[dv:dv + 1, :]
    lv = lam_ref[...]
    lam = (jnp.exp(jnp.sum(lv[0:1] * lv[1:2], axis=1, keepdims=True))
           - jnp.exp(jnp.sum(lv[2:3] * lv[3:4], axis=1, keepdims=True)) + lam_init)
    ot = ot[:, :tq] - lam * ot[:, tq:]
    ms = jnp.mean(ot * ot, axis=0, keepdims=True)
    ot = (ot * lax.rsqrt(ms + EPS) * sg_ref[...]) * (1.0 - lam_init)
    o_ref[0] = ot.T.astype(o_ref.dtype)


def _diff_call(q, k_all, vt_all, lam_vecs, subln_g, *, lam_init, tq, tk, key_rows, key_block):
    b, hq, t, _ = q.shape
    heads = k_all.shape[2] // LANES
    dv = vt_all.shape[1] // heads
    return pl.pallas_call(
        functools.partial(_diff_kernel, lam_init=lam_init, n_kv=key_rows // tk, tk=tk),
        out_shape=jax.ShapeDtypeStruct((b, t, heads * dv), BF16),
        grid=(b, heads, t // tq),
        in_specs=[pl.BlockSpec((1, 2, tq, LANES), lambda bi, h, i: (bi, h, i, 0))]
                 + _kv_specs(key_rows, key_block, dv, 1)
                 + [pl.BlockSpec((4, HEAD_DIM), lambda bi, h, i: (0, 0)),
                    pl.BlockSpec((dv, 1), lambda bi, h, i: (0, 0))],
        out_specs=pl.BlockSpec((1, tq, dv), lambda bi, h, i: (bi, i, h)),
        scratch_shapes=_flash_scratch(2 * tq, tk, dv),
        compiler_params=_params(3),
        name="diff_attention",
    )(q, k_all, vt_all, lam_vecs, subln_g)


def _window_kernel(q_ref, k_ref, vt_ref, sink_ref, o_ref, qt_ref, s_ref, cm_ref, m_ref, acc_ref, *,
                   n_ctx, t_lat, tk):
    heads, tq = q_ref.shape[1:3]
    group = heads // 2
    assert tq == MXU_DIM
    span = tq + 2 * WINDOW
    n_lat = span // tk
    i = pl.program_id(2)
    start = pl.multiple_of(jnp.clip(i * tq - WINDOW, 0, t_lat - span), WINDOW)
    first_row = lambda j: (pl.multiple_of(start + j * tk, WINDOW) if j < n_lat
                           else t_lat + (j - n_lat) * tk)
    load_k = lambda j: k_ref[0, pl.ds(first_row(j), tk), :]
    load_vt = lambda j: vt_ref[0, :, pl.ds(first_row(j), tk)]
    rel = (lax.broadcasted_iota(jnp.int32, (tk, tq), 0) - lax.broadcasted_iota(jnp.int32, (tk, tq), 1)
           + (start - i * tq))

    def mask_fn(j):
        return jnp.abs(rel + j * tk) <= WINDOW if j < n_lat else None

    q_stacked = q_ref[0].reshape(heads * tq, LANES)
    sink = jnp.concatenate([sink_ref[0], sink_ref[1]], axis=1) * LOG2E

    def vt_rows_fn(cols):
        kv = cols.start // (group * tq)
        return slice(kv * HEAD_DIM, (kv + 1) * HEAD_DIM)

    _flash_loop(q_stacked, load_k, load_vt, qt_ref, s_ref, cm_ref, m_ref, acc_ref,
                n_kv=n_lat + n_ctx // tk, tk=tk, mask_fn=mask_fn, sink=sink, vt_rows_fn=vt_rows_fn)
    ot = acc_ref[0:HEAD_DIM, :] / acc_ref[HEAD_DIM:HEAD_DIM + 1, :]
    _store_heads_t(o_ref, ot, tq, heads)


def _window_call(q, k_all, vt_all, sink_rows, *, group, n_ctx, tq):
    b, hq, t, _ = q.shape
    t_all = k_all.shape[1]
    tk = MXU_DIM
    n_chunks = (tq + 2 * WINDOW) // tk + n_ctx // tk
    assert (tq + 2 * WINDOW) % tk == 0 and n_ctx % tk == 0 and n_chunks - 1 <= FLASH_UNROLL
    return pl.pallas_call(
        functools.partial(_window_kernel, n_ctx=n_ctx, t_lat=t, tk=tk),
        out_shape=jax.ShapeDtypeStruct((b, t, hq * HEAD_DIM), BF16),
        grid=(b, hq // (2 * group), t // tq),
        in_specs=[pl.BlockSpec((1, 2 * group, tq, LANES), lambda bi, h, i: (bi, h, i, 0))]
                 + _kv_specs(t_all, 0, 2 * HEAD_DIM, 1)
                 + [pl.BlockSpec((2, 1, group * tq), lambda bi, h, i: (h, 0, 0))],
        out_specs=pl.BlockSpec((1, tq, 2 * group * HEAD_DIM), lambda bi, h, i: (bi, i, h)),
        scratch_shapes=_flash_scratch(2 * group * tq, tk, HEAD_DIM),
        compiler_params=_params(3),
        name="window_attention",
    )(q, k_all, vt_all, sink_rows)


def _glu_kernel(x_ref, mod_ref, g_ref, w_ref, b_ref, u_ref):
    mod = mod_ref[0]
    tm_full, d = x_ref.shape[1:]
    n_parts = tm_full // MXU_DIM if tm_full % MXU_DIM == 0 else 1
    tm = tm_full // n_parts
    for part in range(n_parts):
        rows = slice(part * tm, (part + 1) * tm)
        h = _modulate(x_ref[0, rows, :], g_ref[...], mod[0:1], mod[1:2]).astype(BF16)
        ag = _dot(h, w_ref[...]) + b_ref[...]
        u_ref[0, rows, :] = (ag[:, :d] * jax.nn.sigmoid(ag[:, d:])).astype(u_ref.dtype)


def _glu_call(x, mod, gain, w, bias, *, tm):
    b, t, d = x.shape
    return pl.pallas_call(
        _glu_kernel,
        out_shape=jax.ShapeDtypeStruct((b, t, d), BF16),
        grid=(b, t // tm),
        in_specs=[pl.BlockSpec((1, tm, d), lambda bi, j: (bi, j, 0)),
                  pl.BlockSpec((1, 6, d), lambda bi, j: (bi, 0, 0)),
                  _resident((1, d)),
                  _resident((d, 2 * d)),
                  _resident((1, 2 * d))],
        out_specs=pl.BlockSpec((1, tm, d), lambda bi, j: (bi, j, 0)),
        compiler_params=_params(2),
        name="conv_glu",
    )(x, mod, gain, w, bias)


def _depthwise_conv(ubuf_ref, cbuf_ref, wdw_ref, bdw_ref, tm, d):
    win_rows = CONV_ROWS + 2 * CONV_HALO
    base = CONV_HALO - CONV_WIDTH // 2
    lane_chunk = LANES

    def step(r, carry):
        r0 = pl.multiple_of(r * CONV_ROWS, CONV_ROWS)
        for c0 in range(0, d, lane_chunk):
            win = ubuf_ref[pl.ds(r0, win_rows), c0:c0 + lane_chunk]
            acc = jnp.zeros((CONV_ROWS, lane_chunk), F32) + bdw_ref[:, c0:c0 + lane_chunk]
            for sub in range(8):
                shifted = win if sub == 0 else pltpu.roll(win, win_rows - sub, axis=0)
                for al in range(0, win_rows - CONV_ROWS + 1, 8):
                    k = al + sub - base
                    if 0 <= k < CONV_WIDTH:
                        acc = acc + wdw_ref[k:k + 1, c0:c0 + lane_chunk] * shifted[al:al + CONV_ROWS]
            cbuf_ref[pl.ds(r0, CONV_ROWS), c0:c0 + lane_chunk] = acc
        return carry

    lax.fori_loop(0, tm // CONV_ROWS, step, 0)


def _post_kernel(*refs, conv, has_bias, d_ff_chunk):
    it = iter(refs)
    x_ref = next(it)
    if conv:
        up_ref, uc_ref, un_ref = next(it), next(it), next(it)
        wdw_ref, bdw_ref, lng_ref, lnb_ref = next(it), next(it), next(it), next(it)
    else:
        o_ref = next(it)
    mod_ref, wo_ref = next(it), next(it)
    bo_ref = next(it) if has_bias else None
    g2_ref, wup_ref, wdn_ref, out_ref = next(it), next(it), next(it), next(it)
    if conv:
        ubuf_ref, cbuf_ref = next(it), next(it)

    x = x_ref[0]
    mod = mod_ref[0]
    tm, d = x.shape
    if conv:
        j = pl.program_id(1)
        prev = jnp.where(j > 0, up_ref[0].astype(F32), 0.0)
        nxt = jnp.where(j < pl.num_programs(1) - 1, un_ref[0].astype(F32), 0.0)
        ubuf_ref[0:CONV_HALO] = prev
        ubuf_ref[CONV_HALO:CONV_HALO + tm] = uc_ref[0].astype(F32)
        ubuf_ref[CONV_HALO + tm:] = nxt
        _depthwise_conv(ubuf_ref, cbuf_ref, wdw_ref, bdw_ref, tm, d)
        cv = cbuf_ref[...]
        mu = jnp.mean(cv, axis=-1, keepdims=True)
        cc = cv - mu
        var = jnp.mean(cc * cc, axis=-1, keepdims=True)
        yn = cc * lax.rsqrt(var + EPS) * lng_ref[...] + lnb_ref[...]
        o = (yn * jax.nn.sigmoid(yn)).astype(BF16)
    else:
        o = o_ref[0]
    y = _dot(o, wo_ref[...])
    if has_bias:
        y = y + bo_ref[...]
    x1 = x + mod[2:3] * y
    h2 = _modulate(x1, g2_ref[...], mod[3:4], mod[4:5]).astype(BF16)
    acc = jnp.zeros((tm, d), F32)
    d_ff = wup_ref.shape[1]
    for c0 in range(0, d_ff, d_ff_chunk):
        up = jnp.maximum(_dot(h2, wup_ref[:, c0:c0 + d_ff_chunk]), 0.0)
        acc = acc + _dot((up * up).astype(BF16), wdn_ref[c0:c0 + d_ff_chunk, :])
    out_ref[0] = x1 + mod[5:6] * acc


def _post_call(x, mixed, mod, w_o, b_o, gain2, w_up, w_down, conv_params, *, tm):
    b, t, d = x.shape
    d_in = w_o.shape[0]
    d_ff = w_up.shape[1]
    conv = conv_params is not None
    has_bias = b_o is not None
    tok = lambda bi, j: (bi, j, 0)
    in_specs = [pl.BlockSpec((1, tm, d), tok)]
    args = [x]
    scratch = []
    if conv:
        hb = tm // CONV_HALO
        last = t // CONV_HALO - 1
        in_specs += [pl.BlockSpec((1, CONV_HALO, d), lambda bi, j: (bi, jnp.maximum(j * hb - 1, 0), 0)),
                     pl.BlockSpec((1, tm, d), tok),
                     pl.BlockSpec((1, CONV_HALO, d), lambda bi, j: (bi, jnp.minimum((j + 1) * hb, last), 0)),
                     _resident((CONV_WIDTH, d)), _resident((1, d)), _resident((1, d)), _resident((1, d))]
        args += [mixed, mixed, mixed] + list(conv_params)
        scratch = [pltpu.VMEM((tm + 2 * CONV_HALO, d), F32), pltpu.VMEM((tm, d), F32)]
    else:
        in_specs += [pl.BlockSpec((1, tm, d_in), tok)]
        args += [mixed]
    in_specs += [pl.BlockSpec((1, 6, d), lambda bi, j: (bi, 0, 0)), _resident((d_in, d))]
    args += [mod, w_o]
    if has_bias:
        in_specs += [_resident((1, d))]
        args += [b_o]
    in_specs += [_resident((1, d)), _resident((d, d_ff)), _resident((d_ff, d))]
    args += [gain2, w_up, w_down]
    return pl.pallas_call(
        functools.partial(_post_kernel, conv=conv, has_bias=has_bias, d_ff_chunk=min(d_ff, 1024)),
        out_shape=jax.ShapeDtypeStruct((b, t, d), F32),
        grid=(b, t // tm),
        in_specs=in_specs,
        out_specs=pl.BlockSpec((1, tm, d), tok),
        scratch_shapes=scratch,
        compiler_params=_params(2),
        name="post_mlp",
    )(*args)


def _rope_tables(t):
    rows = t // GRID_W
    row = jnp.repeat(jnp.arange(rows, dtype=F32), GRID_W)
    col = jnp.tile(jnp.arange(GRID_W, dtype=F32), rows)
    half = HEAD_DIM // 2
    inv = 1.0 / jnp.power(ROPE_THETA, jnp.arange(0, half, 2, dtype=F32) / half)
    ang = jnp.concatenate([row[:, None] * inv, col[:, None] * inv], axis=-1)
    cos = jnp.repeat(jnp.cos(ang), 2, axis=1)
    sin = jnp.repeat(jnp.sin(ang), 2, axis=1) * jnp.tile(jnp.array([-1.0, 1.0], F32), half)
    reps = LANES // HEAD_DIM
    return jnp.tile(cos, (1, reps)), jnp.tile(sin, (1, reps))


def _prep_qkv(w_qkv, q_g, k_g, n_q, n_k):
    hq, hk = n_q // HEAD_DIM, n_k // HEAD_DIM
    w = w_qkv[:, :n_q + n_k].astype(BF16)
    wvt = w_qkv[:, n_q + n_k:].T.astype(BF16)
    scale = LOG2E / math.sqrt(HEAD_DIM)
    head_gain = jnp.concatenate([jnp.tile(q_g * scale, hq), jnp.tile(k_g, hk)])
    return w, wvt, head_gain.reshape(1, n_q + n_k).astype(F32)


def _ones_block_diag():
    idx = jnp.arange(MXU_DIM) // HEAD_DIM
    return (idx[:, None] == idx[None, :]).astype(BF16)


def kernel(x, c, ctx, c_ctx, norm1_g, norm2_g, mod_w, mod_b, mlp_up, mlp_down,
           gqa_w_qkv, gqa_q_g, gqa_k_g, gqa_w_o,
           conv_w_pw1, conv_b_pw1, conv_w_dw, conv_b_dw, conv_ln_g, conv_ln_b, conv_w_pw2, conv_b_pw2,
           diff_w_qkv, diff_q_g, diff_k_g, diff_lam_q1, diff_lam_k1, diff_lam_q2, diff_lam_k2,
           diff_subln_g, diff_w_o,
           swa_w_qkv, swa_q_g, swa_k_g, swa_sink, swa_w_o):
    b, t, d = x.shape
    n_ctx = ctx.shape[1]
    depth = norm1_g.shape[0]
    assert depth == N_MIXERS and t % GRID_W == 0 and t % n_ctx == 0

    tm = _pick(t, (512, 256, 128))
    tm_proj = _pick(t, (1024, 512, 256, 128))
    tq_win = _pick(t, (256, 128))
    tq_gqa = _pick(t, (1024, 512, 256, 128))
    tq_diff = _pick(t, (2048, 1024, 512, 256, 128))
    t_all = n_ctx + t
    tk = _pick(t_all, (768, 512, 256, 128))
    tkc = _pick(n_ctx, (768, 512, 256, 128))

    cv = jnp.concatenate([c, c_ctx[None], jnp.zeros((8 - b - 1, d), F32)], axis=0)
    mod_all = _mod_call(cv, mod_w, mod_b).reshape(depth, 8, 6, d)
    rope = _rope_tables(t)
    ones_bd = _ones_block_diag()
    row = lambda v: v.reshape(1, -1).astype(F32)

    xc = ctx
    for i in range(depth):
        m, j = i % N_MIXERS, i // N_MIXERS
        need_ctx = i < depth - 1
        mod_l = mod_all[i, :b]
        mod_c = jnp.broadcast_to(mod_all[i, b], (b, 6, d))
        g1, g2 = row(norm1_g[i]), row(norm2_g[i])
        w_up, w_down = mlp_up[i].astype(BF16), mlp_down[i].astype(BF16)
        conv_params = None
        b_o = None
        if m == 1:
            w1, b1 = conv_w_pw1[j].astype(BF16), row(conv_b_pw1[j])
            mixed = _glu_call(x, mod_l, g1, w1, b1, tm=tm_proj)
            mixed_c = _glu_call(xc, mod_c, g1, w1, b1, tm=n_ctx) if need_ctx else None
            conv_params = (conv_w_dw[j].astype(F32), row(conv_b_dw[j]), row(conv_ln_g[j]), row(conv_ln_b[j]))
            w_o, b_o = conv_w_pw2[j].astype(BF16), row(conv_b_pw2[j])
        else:
            if m == 2:
                w_qkv, q_g, k_g, w_o = diff_w_qkv[j], diff_q_g[j], diff_k_g[j], diff_w_o[j]
                n_q = n_k = w_qkv.shape[1] // 3
                q_half = tuple(h % 2 for h in range(n_q // HEAD_DIM))
            else:
                w_qkv, q_g, k_g, w_o = ((gqa_w_qkv[j], gqa_q_g[j], gqa_k_g[j], gqa_w_o[j]) if m == 0 else
                                        (swa_w_qkv[j], swa_q_g[j], swa_k_g[j], swa_w_o[j]))
                n_q = w_o.shape[0]
                n_k = (w_qkv.shape[1] - n_q) // 2
                group = n_q // n_k
                q_half = tuple((h // group) % 2 for h in range(n_q // HEAD_DIM))
            w_o = w_o.astype(BF16)
            w_p, wvt, head_gain = _prep_qkv(w_qkv, q_g, k_g, n_q, n_k)
            proj = functools.partial(_proj_call, gain=g1, w=w_p, wvt=wvt, head_gain=head_gain,
                                     ones_bd=ones_bd, n_q=n_q, n_k=n_k, q_half=q_half)
            kv_buffers = (jnp.zeros((b, t_all, n_k), BF16), jnp.zeros((b, wvt.shape[0], t_all), BF16))
            q, *kv_buffers = proj(x, mod_l, rope=rope, tm=tm_proj, kv_buffers=kv_buffers, first_row=0)
            qc, k_all, vt_all = proj(xc, mod_c, rope=None, tm=n_ctx, kv_buffers=kv_buffers, first_row=t)
            lat_keys = dict(tk=tk, key_rows=t_all, key_block=0)
            ctx_keys = dict(tk=tkc, key_rows=n_ctx, key_block=t // n_ctx)
            mixed_c = None
            if m == 0:
                mixed = _gqa_call(q, k_all, vt_all, group=group, tq=tq_gqa, **lat_keys)
                if need_ctx:
                    mixed_c = _gqa_call(qc, k_all, vt_all, group=group, tq=n_ctx, **ctx_keys)
            elif m == 2:
                lam_init = 0.8 - 0.6 * math.exp(-0.3 * i)
                lam_vecs = jnp.stack([diff_lam_q1[j], diff_lam_k1[j], diff_lam_q2[j], diff_lam_k2[j]]).astype(F32)
                sg = diff_subln_g[j].reshape(-1, 1).astype(F32)
                mixed = _diff_call(q, k_all, vt_all, lam_vecs, sg, lam_init=lam_init, tq=tq_diff, **lat_keys)
                if need_ctx:
                    mixed_c = _diff_call(qc, k_all, vt_all, lam_vecs, sg, lam_init=lam_init, tq=n_ctx, **ctx_keys)
            else:
                assert not need_ctx, "windowed layer with a context update is not supported"
                sink_rows = jnp.repeat(swa_sink[j].astype(F32).reshape(n_k // HEAD_DIM, 1, group), tq_win, axis=2)
                mixed = _window_call(q, k_all, vt_all, sink_rows, group=group, n_ctx=n_ctx, tq=tq_win)
        x = _post_call(x, mixed, mod_l, w_o, b_o, g2, w_up, w_down, conv_params, tm=tm)
        if need_ctx and conv_params is None:
            xc = _post_call(xc.reshape(1, b * n_ctx, d), mixed_c.reshape(1, b * n_ctx, -1), mod_c[:1],
                            w_o, b_o, g2, w_up, w_down, None, tm=b * n_ctx).reshape(b, n_ctx, d)
        elif need_ctx:
            xc = _post_call(xc, mixed_c, mod_c, w_o, b_o, g2, w_up, w_down, conv_params, tm=n_ctx)
    return x
```

```python
import functools
import math

import jax
import jax.numpy as jnp
from jax import lax
from jax.experimental import pallas as pl
from jax.experimental.pallas import tpu as pltpu

F32 = jnp.float32
BF16 = jnp.bfloat16

GRID_W = 64
HEAD_DIM = 64
N_MIXERS = 4
CONV_WIDTH = 31
WINDOW = 128
ROPE_THETA = 10000.0
EPS = 1e-6
NEG_INF = -1e30
LOG2E = math.log2(math.e)

LANES = 128
MXU_DIM = 256
VMEM_LIMIT_BYTES = 56 * 1024 * 1024

CONV_HALO = 16
CONV_ROWS = 128


def _params(n_axes, fuse_inputs=None):
    return pltpu.CompilerParams(dimension_semantics=("parallel",) * n_axes,
                                vmem_limit_bytes=VMEM_LIMIT_BYTES, allow_input_fusion=fuse_inputs)


def _resident(shape):
    nd = len(shape)
    return pl.BlockSpec(shape, lambda *_: (0,) * nd, pipeline_mode=pl.Buffered(1))


def _pick(n, candidates):
    for cand in candidates:
        if n % cand == 0:
            return cand
    return n


def _modulate(x, gain, shift, scale):
    ms = jnp.mean(x * x, axis=-1, keepdims=True)
    return (x * lax.rsqrt(ms + EPS) * gain) * (1.0 + scale) + shift


def _dot(a, b):
    return jnp.dot(a, b, preferred_element_type=F32)


def _dot_nt(a, b):
    return lax.dot_general(a, b, (((1,), (1,)), ((), ())), preferred_element_type=F32)


def _mod_kernel(cv_ref, w_ref, b_ref, o_ref):
    cv = cv_ref[...]
    s = cv * jax.nn.sigmoid(cv)
    o_ref[0] = _dot(s.astype(BF16), w_ref[0].astype(BF16)) + b_ref[0]


def _mod_call(cv, mod_w, mod_b):
    depth, d, n = mod_w.shape
    rows = cv.shape[0]
    tn = _pick(n, (3072, 1536, 1024, 512))
    return pl.pallas_call(
        _mod_kernel,
        out_shape=jax.ShapeDtypeStruct((depth, rows, n), F32),
        grid=(depth, n // tn),
        in_specs=[pl.BlockSpec((rows, d), lambda i, j: (0, 0)),
                  pl.BlockSpec((1, d, tn), lambda i, j: (i, 0, j)),
                  pl.BlockSpec((1, 1, tn), lambda i, j: (i, 0, j))],
        out_specs=pl.BlockSpec((1, rows, tn), lambda i, j: (i, 0, j)),
        compiler_params=_params(2),
        name="modulation",
    )(cv, mod_w, mod_b.reshape(depth, 1, n))


def _proj_kernel(*refs, n_q, n_k, q_half, use_rope, n_alias):
    q_ref, k_ref, vt_ref = refs[-3:]
    refs = refs[:len(refs) - 3 - n_alias]
    if use_rope:
        (x_ref, mod_ref, g_ref, w_ref, wvt_ref, hg_ref, e_ref, cos_ref, sin_ref) = refs
    else:
        (x_ref, mod_ref, g_ref, w_ref, wvt_ref, hg_ref, e_ref) = refs
    mod = mod_ref[0]
    tm_full = x_ref.shape[1]
    n_qk = n_q + n_k
    ones_bd = e_ref[...]
    n_parts = tm_full // MXU_DIM if tm_full % MXU_DIM == 0 else 1
    tm = tm_full // n_parts
    lane = lax.broadcasted_iota(jnp.int32, (tm, LANES), 1)
    even = (lane % 2) == 0
    low = lane < HEAD_DIM
    for part in range(n_parts):
        rows = slice(part * tm, (part + 1) * tm)
        h = _modulate(x_ref[0, rows, :], g_ref[...], mod[0:1], mod[1:2]).astype(BF16)
        qk = _dot(h, w_ref[...])
        vt_ref[0, :, rows] = _dot_nt(wvt_ref[...], h).astype(BF16)
        sq = qk * qk
        hi = sq.astype(BF16)
        lo = (sq - hi.astype(F32)).astype(BF16)
        ss = jnp.concatenate(
            [_dot(hi[:, t:t + MXU_DIM], ones_bd) + _dot(lo[:, t:t + MXU_DIM], ones_bd)
             for t in range(0, n_qk, MXU_DIM)], axis=1)
        qk = qk * lax.rsqrt(ss * (1.0 / HEAD_DIM) + EPS) * hg_ref[...]
        tiles = [qk[:, t:t + LANES] for t in range(0, n_qk, LANES)]
        if use_rope:
            cos = cos_ref[rows, :]
            sin = sin_ref[rows, :]
            tiles = [t * cos + jnp.where(even, pltpu.roll(t, LANES - 1, axis=1), pltpu.roll(t, 1, axis=1)) * sin
                     for t in tiles]
        for j in range(n_q // HEAD_DIM):
            t = tiles[j // 2]
            if j % 2 != q_half[j]:
                t = pltpu.roll(t, HEAD_DIM, axis=1)
            keep = low if q_half[j] == 0 else jnp.logical_not(low)
            q_ref[0, j, rows, :] = jnp.where(keep, t, 0.0).astype(BF16)
        k_ref[0, rows, :] = jnp.concatenate(tiles[n_q // LANES:], axis=1).astype(BF16)


def _proj_call(x, mod, gain, w, wvt, head_gain, ones_bd, rope, *, n_q, n_k, q_half, tm, kv_buffers,
               first_row):
    b, t, d = x.shape
    n_v = wvt.shape[0]
    t_keys = kv_buffers[0].shape[1]
    hq = n_q // HEAD_DIM
    use_rope = rope is not None
    in_specs = [pl.BlockSpec((1, tm, d), lambda bi, j: (bi, j, 0)),
                pl.BlockSpec((1, 6, d), lambda bi, j: (bi, 0, 0)),
                _resident((1, d)),
                _resident((d, n_q + n_k)),
                _resident((n_v, d)),
                _resident((1, n_q + n_k)),
                _resident((MXU_DIM, MXU_DIM))]
    args = [x, mod, gain, w, wvt, head_gain, ones_bd]
    if use_rope:
        in_specs += [pl.BlockSpec((tm, LANES), lambda bi, j: (j, 0)),
                     pl.BlockSpec((tm, LANES), lambda bi, j: (j, 0))]
        args += list(rope)
    off = first_row // tm
    assert off * tm == first_row and first_row + t <= t_keys
    aliases = {len(args): 1, len(args) + 1: 2}
    in_specs += [pl.BlockSpec(memory_space=pl.ANY), pl.BlockSpec(memory_space=pl.ANY)]
    args += list(kv_buffers)
    return pl.pallas_call(
        functools.partial(_proj_kernel, n_q=n_q, n_k=n_k, q_half=q_half, use_rope=use_rope,
                          n_alias=len(aliases)),
        out_shape=(jax.ShapeDtypeStruct((b, hq, t, LANES), BF16),
                   jax.ShapeDtypeStruct((b, t_keys, n_k), BF16),
                   jax.ShapeDtypeStruct((b, n_v, t_keys), BF16)),
        grid=(b, t // tm),
        in_specs=in_specs,
        out_specs=(pl.BlockSpec((1, hq, tm, LANES), lambda bi, j: (bi, 0, j, 0)),
                   pl.BlockSpec((1, tm, n_k), lambda bi, j: (bi, j + off, 0)),
                   pl.BlockSpec((1, n_v, tm), lambda bi, j: (bi, 0, j + off))),
        input_output_aliases=aliases,
        compiler_params=_params(2),
        name="attn_proj",
    )(*args)


ONES_ROWS = 16
FLASH_UNROLL = 4


def _chunk_rows(j, tk):
    return pl.ds(j * tk if isinstance(j, int) else pl.multiple_of(j * tk, tk), tk)


def _flash_loop(q_stacked, load_k, load_vt, qt_ref, s_ref, cm_ref, m_ref, acc_ref, *, n_kv, tk,
                mask_fn=None, sink=None, vt_rows_fn=None):
    v_rows = acc_ref.shape[0] - ONES_ROWS
    qt_ref[...] = q_stacked.astype(F32).T.astype(BF16)
    acc_ref[0:v_rows, :] = jnp.zeros((v_rows, acc_ref.shape[1]), F32)
    if sink is None:
        m_ref[...] = jnp.full(m_ref.shape, NEG_INF, F32)
        acc_ref[v_rows:, :] = jnp.zeros((ONES_ROWS, acc_ref.shape[1]), F32)
    else:
        m_ref[...] = sink
        acc_ref[v_rows:, :] = jnp.ones((ONES_ROWS, acc_ref.shape[1]), F32)
    ones = jnp.ones((ONES_ROWS, tk), BF16)
    m_cols = s_ref.shape[2]
    col_blocks = [slice(c0, c0 + MXU_DIM) for c0 in range(0, m_cols, MXU_DIM)]

    def produce(kc, keep, slot, cols):
        s = _dot(kc, qt_ref[:, cols])
        if keep is not None:
            s = jnp.where(keep, s, NEG_INF)
        s_ref[slot, :, cols] = s
        cm_ref[slot, :, cols] = jnp.max(s.reshape(tk // 8, 8, MXU_DIM), axis=0)

    def consume(vt1, slot, cols):
        m_prev = m_ref[:, cols]
        m_new = jnp.maximum(m_prev, jnp.max(cm_ref[slot, :, cols], axis=0, keepdims=True))
        alpha = jnp.exp2(m_prev - m_new)
        p = jnp.exp2(s_ref[slot, :, cols] - m_new).astype(BF16)
        acc_ref[:, cols] = alpha * acc_ref[:, cols] + _dot(vt1, p)
        m_ref[:, cols] = m_new

    def step(j_prod, slot_prod, j_cons, slot_cons):
        if j_prod is not None:
            kc = load_k(j_prod)
            keep = mask_fn(j_prod) if mask_fn is not None else None
        if j_cons is not None:
            vt_chunk = load_vt(j_cons)
            with_ones = lambda cols: jnp.concatenate(
                [vt_chunk if vt_rows_fn is None else vt_chunk[vt_rows_fn(cols)], ones], axis=0)
        for cols in col_blocks:
            if j_prod is not None:
                produce(kc, keep, slot_prod, cols)
            if j_cons is not None:
                consume(with_ones(cols), slot_cons, cols)

    step(0, 0, None, None)
    trips = (n_kv - 1) // FLASH_UNROLL if n_kv - 1 > FLASH_UNROLL else 0

    def body(trip, carry):
        j0 = trip * FLASH_UNROLL
        for u in range(FLASH_UNROLL):
            step(j0 + u + 1, (u + 1) % 2, j0 + u, u % 2)
        return carry

    if trips:
        lax.fori_loop(0, trips, body, 0)
    for j in range(trips * FLASH_UNROLL, n_kv - 1):
        step(j + 1, (j + 1) % 2, j, j % 2)
    step(None, None, n_kv - 1, (n_kv - 1) % 2)


def _store_heads_t(o_ref, ot, tq, group):
    for gg in range(group // 2):
        slab = jnp.concatenate([ot[:, (2 * gg) * tq:(2 * gg + 1) * tq],
                                ot[:, (2 * gg + 1) * tq:(2 * gg + 2) * tq]], axis=0)
        o_ref[0, :, gg * LANES:(gg + 1) * LANES] = slab.T.astype(o_ref.dtype)


def _kv_specs(key_rows, key_block, v_rows, kv_per_tile):
    return [pl.BlockSpec((1, key_rows, LANES), lambda bi, h, i: (bi, key_block, h // kv_per_tile)),
            pl.BlockSpec((1, v_rows, key_rows), lambda bi, h, i: (bi, h, key_block))]


def _flash_scratch(m_cols, tk, v_rows):
    return [pltpu.VMEM((LANES, m_cols), BF16),
            pltpu.VMEM((2, tk, m_cols), F32),
            pltpu.VMEM((2, 8, m_cols), F32),
            pltpu.VMEM((1, m_cols), F32),
            pltpu.VMEM((v_rows + ONES_ROWS, m_cols), F32)]


def _gqa_kernel(q_ref, k_ref, vt_ref, o_ref, qt_ref, s_ref, cm_ref, m_ref, acc_ref, *, n_kv, tk):
    group, tq = q_ref.shape[1:3]
    q_stacked = q_ref[0].reshape(group * tq, LANES)
    load_k = lambda j: k_ref[0, _chunk_rows(j, tk), :]
    load_vt = lambda j: vt_ref[0, :, _chunk_rows(j, tk)]
    _flash_loop(q_stacked, load_k, load_vt, qt_ref, s_ref, cm_ref, m_ref, acc_ref, n_kv=n_kv, tk=tk)
    ot = acc_ref[0:HEAD_DIM, :] / acc_ref[HEAD_DIM:HEAD_DIM + 1, :]
    _store_heads_t(o_ref, ot, tq, group)


def _gqa_call(q, k_all, vt_all, *, group, tq, tk, key_rows, key_block):
    b, hq, t, _ = q.shape
    return pl.pallas_call(
        functools.partial(_gqa_kernel, n_kv=key_rows // tk, tk=tk),
        out_shape=jax.ShapeDtypeStruct((b, t, hq * HEAD_DIM), BF16),
        grid=(b, hq // group, t // tq),
        in_specs=[pl.BlockSpec((1, group, tq, LANES), lambda bi, h, i: (bi, h, i, 0))]
                 + _kv_specs(key_rows, key_block, HEAD_DIM, LANES // HEAD_DIM),
        out_specs=pl.BlockSpec((1, tq, group * HEAD_DIM), lambda bi, h, i: (bi, i, h)),
        scratch_shapes=_flash_scratch(group * tq, tk, HEAD_DIM),
        compiler_params=_params(3),
        name="gqa_attention",
    )(q, k_all, vt_all)


def _diff_kernel(q_ref, k_ref, vt_ref, lam_ref, sg_ref, o_ref, qt_ref, s_ref, cm_ref, m_ref, acc_ref, *,
                 lam_init, n_kv, tk):
    tq = q_ref.shape[2]
    dv = vt_ref.shape[1]
    q_stacked = q_ref[0].reshape(2 * tq, LANES)
    load_k = lambda j: k_ref[0, _chunk_rows(j, tk), :]
    load_vt = lambda j: vt_ref[0, :, _chunk_rows(j, tk)]
    _flash_loop(q_stacked, load_k, load_vt, qt_ref, s_ref, cm_ref, m_ref, acc_ref, n_kv=n_kv, tk=tk)
    ot = acc_ref[0:dv, :] / acc_ref[dv:dv + 1, :]
    lv = lam_ref[...]
    lam = (jnp.exp(jnp.sum(lv[0:1] * lv[1:2], axis=1, keepdims=True))
           - jnp.exp(jnp.sum(lv[2:3] * lv[3:4], axis=1, keepdims=True)) + lam_init)
    ot = ot[:, :tq] - lam * ot[:, tq:]
    ms = jnp.mean(ot * ot, axis=0, keepdims=True)
    ot = (ot * lax.rsqrt(ms + EPS) * sg_ref[...]) * (1.0 - lam_init)
    o_ref[0] = ot.T.astype(o_ref.dtype)


def _diff_call(q, k_all, vt_all, lam_vecs, subln_g, *, lam_init, tq, tk, key_rows, key_block):
    b, hq, t, _ = q.shape
    heads = k_all.shape[2] // LANES
    dv = vt_all.shape[1] // heads
    return pl.pallas_call(
        functools.partial(_diff_kernel, lam_init=lam_init, n_kv=key_rows // tk, tk=tk),
        out_shape=jax.ShapeDtypeStruct((b, t, heads * dv), BF16),
        grid=(b, heads, t // tq),
        in_specs=[pl.BlockSpec((1, 2, tq, LANES), lambda bi, h, i: (bi, h, i, 0))]
                 + _kv_specs(key_rows, key_block, dv, 1)
                 + [pl.BlockSpec((4, HEAD_DIM), lambda bi, h, i: (0, 0)),
                    pl.BlockSpec((dv, 1), lambda bi, h, i: (0, 0))],
        out_specs=pl.BlockSpec((1, tq, dv), lambda bi, h, i: (bi, i, h)),
        scratch_shapes=_flash_scratch(2 * tq, tk, dv),
        compiler_params=_params(3),
        name="diff_attention",
    )(q, k_all, vt_all, lam_vecs, subln_g)


def _window_kernel(q_ref, k_ref, vt_ref, sink_ref, o_ref, qt_ref, s_ref, cm_ref, m_ref, acc_ref, *,
                   n_ctx, t_lat, tk):
    heads, tq = q_ref.shape[1:3]
    group = heads // 2
    assert tq == MXU_DIM
    span = tq + 2 * WINDOW
    n_lat = span // tk
    i = pl.program_id(2)
    start = pl.multiple_of(jnp.clip(i * tq - WINDOW, 0, t_lat - span), WINDOW)
    first_row = lambda j: (pl.multiple_of(start + j * tk, WINDOW) if j < n_lat
                           else t_lat + (j - n_lat) * tk)
    load_k = lambda j: k_ref[0, pl.ds(first_row(j), tk), :]
    load_vt = lambda j: vt_ref[0, :, pl.ds(first_row(j), tk)]
    rel = (lax.broadcasted_iota(jnp.int32, (tk, tq), 0) - lax.broadcasted_iota(jnp.int32, (tk, tq), 1)
           + (start - i * tq))

    def mask_fn(j):
        return jnp.abs(rel + j * tk) <= WINDOW if j < n_lat else None

    q_stacked = q_ref[0].reshape(heads * tq, LANES)
    sink = jnp.concatenate([sink_ref[0], sink_ref[1]], axis=1) * LOG2E

    def vt_rows_fn(cols):
        kv = cols.start // (group * tq)
        return slice(kv * HEAD_DIM, (kv + 1) * HEAD_DIM)

    _flash_loop(q_stacked, load_k, load_vt, qt_ref, s_ref, cm_ref, m_ref, acc_ref,
                n_kv=n_lat + n_ctx // tk, tk=tk, mask_fn=mask_fn, sink=sink, vt_rows_fn=vt_rows_fn)
    ot = acc_ref[0:HEAD_DIM, :] / acc_ref[HEAD_DIM:HEAD_DIM + 1, :]
    _store_heads_t(o_ref, ot, tq, heads)


def _window_call(q, k_all, vt_all, sink_rows, *, group, n_ctx, tq):
    b, hq, t, _ = q.shape
    t_all = k_all.shape[1]
    tk = MXU_DIM
    n_chunks = (tq + 2 * WINDOW) // tk + n_ctx // tk
    assert (tq + 2 * WINDOW) % tk == 0 and n_ctx % tk == 0 and n_chunks - 1 <= FLASH_UNROLL
    return pl.pallas_call(
        functools.partial(_window_kernel, n_ctx=n_ctx, t_lat=t, tk=tk),
        out_shape=jax.ShapeDtypeStruct((b, t, hq * HEAD_DIM), BF16),
        grid=(b, hq // (2 * group), t // tq),
        in_specs=[pl.BlockSpec((1, 2 * group, tq, LANES), lambda bi, h, i: (bi, h, i, 0))]
                 + _kv_specs(t_all, 0, 2 * HEAD_DIM, 1)
                 + [pl.BlockSpec((2, 1, group * tq), lambda bi, h, i: (h, 0, 0))],
        out_specs=pl.BlockSpec((1, tq, 2 * group * HEAD_DIM), lambda bi, h, i: (bi, i, h)),
        scratch_shapes=_flash_scratch(2 * group * tq, tk, HEAD_DIM),
        compiler_params=_params(3),
        name="window_attention",
    )(q, k_all, vt_all, sink_rows)


def _glu_kernel(x_ref, mod_ref, g_ref, w_ref, b_ref, u_ref):
    mod = mod_ref[0]
    tm_full, d = x_ref.shape[1:]
    n_parts = tm_full // MXU_DIM if tm_full % MXU_DIM == 0 else 1
    tm = tm_full // n_parts
    for part in range(n_parts):
        rows = slice(part * tm, (part + 1) * tm)
        h = _modulate(x_ref[0, rows, :], g_ref[...], mod[0:1], mod[1:2]).astype(BF16)
        ag = _dot(h, w_ref[...]) + b_ref[...]
        u_ref[0, rows, :] = (ag[:, :d] * jax.nn.sigmoid(ag[:, d:])).astype(u_ref.dtype)


def _glu_call(x, mod, gain, w, bias, *, tm):
    b, t, d = x.shape
    return pl.pallas_call(
        _glu_kernel,
        out_shape=jax.ShapeDtypeStruct((b, t, d), BF16),
        grid=(b, t // tm),
        in_specs=[pl.BlockSpec((1, tm, d), lambda bi, j: (bi, j, 0)),
                  pl.BlockSpec((1, 6, d), lambda bi, j: (bi, 0, 0)),
                  _resident((1, d)),
                  _resident((d, 2 * d)),
                  _resident((1, 2 * d))],
        out_specs=pl.BlockSpec((1, tm, d), lambda bi, j: (bi, j, 0)),
        compiler_params=_params(2),
        name="conv_glu",
    )(x, mod, gain, w, bias)


def _depthwise_conv(ubuf_ref, cbuf_ref, wdw_ref, bdw_ref, tm, d):
    win_rows = CONV_ROWS + 2 * CONV_HALO
    base = CONV_HALO - CONV_WIDTH // 2
    lane_chunk = LANES

    def step(r, carry):
        r0 = pl.multiple_of(r * CONV_ROWS, CONV_ROWS)
        for c0 in range(0, d, lane_chunk):
            win = ubuf_ref[pl.ds(r0, win_rows), c0:c0 + lane_chunk]
            acc = jnp.zeros((CONV_ROWS, lane_chunk), F32) + bdw_ref[:, c0:c0 + lane_chunk]
            for sub in range(8):
                shifted = win if sub == 0 else pltpu.roll(win, win_rows - sub, axis=0)
                for al in range(0, win_rows - CONV_ROWS + 1, 8):
                    k = al + sub - base
                    if 0 <= k < CONV_WIDTH:
                        acc = acc + wdw_ref[k:k + 1, c0:c0 + lane_chunk] * shifted[al:al + CONV_ROWS]
            cbuf_ref[pl.ds(r0, CONV_ROWS), c0:c0 + lane_chunk] = acc
        return carry

    lax.fori_loop(0, tm // CONV_ROWS, step, 0)


def _post_kernel(*refs, conv, has_bias, d_ff_chunk):
    it = iter(refs)
    x_ref = next(it)
    if conv:
        up_ref, uc_ref, un_ref = next(it), next(it), next(it)
        wdw_ref, bdw_ref, lng_ref, lnb_ref = next(it), next(it), next(it), next(it)
    else:
        o_ref = next(it)
    mod_ref, wo_ref = next(it), next(it)
    bo_ref = next(it) if has_bias else None
    g2_ref, wup_ref, wdn_ref, out_ref = next(it), next(it), next(it), next(it)
    if conv:
        ubuf_ref, cbuf_ref = next(it), next(it)

    x = x_ref[0]
    mod = mod_ref[0]
    tm, d = x.shape
    if conv:
        j = pl.program_id(1)
        prev = jnp.where(j > 0, up_ref[0].astype(F32), 0.0)
        nxt = jnp.where(j < pl.num_programs(1) - 1, un_ref[0].astype(F32), 0.0)
        ubuf_ref[0:CONV_HALO] = prev
        ubuf_ref[CONV_HALO:CONV_HALO + tm] = uc_ref[0].astype(F32)
        ubuf_ref[CONV_HALO + tm:] = nxt
        _depthwise_conv(ubuf_ref, cbuf_ref, wdw_ref, bdw_ref, tm, d)
        cv = cbuf_ref[...]
        mu = jnp.mean(cv, axis=-1, keepdims=True)
        cc = cv - mu
        var = jnp.mean(cc * cc, axis=-1, keepdims=True)
        yn = cc * lax.rsqrt(var + EPS) * lng_ref[...] + lnb_ref[...]
        o = (yn * jax.nn.sigmoid(yn)).astype(BF16)
    else:
        o = o_ref[0]
    y = _dot(o, wo_ref[...])
    if has_bias:
        y = y + bo_ref[...]
    x1 = x + mod[2:3] * y
    h2 = _modulate(x1, g2_ref[...], mod[3:4], mod[4:5]).astype(BF16)
    acc = jnp.zeros((tm, d), F32)
    d_ff = wup_ref.shape[1]
    for c0 in range(0, d_ff, d_ff_chunk):
        up = jnp.maximum(_dot(h2, wup_ref[:, c0:c0 + d_ff_chunk]), 0.0)
        acc = acc + _dot((up * up).astype(BF16), wdn_ref[c0:c0 + d_ff_chunk, :])
    out_ref[0] = x1 + mod[5:6] * acc


def _post_call(x, mixed, mod, w_o, b_o, gain2, w_up, w_down, conv_params, *, tm):
    b, t, d = x.shape
    d_in = w_o.shape[0]
    d_ff = w_up.shape[1]
    conv = conv_params is not None
    has_bias = b_o is not None
    tok = lambda bi, j: (bi, j, 0)
    in_specs = [pl.BlockSpec((1, tm, d), tok)]
    args = [x]
    scratch = []
    if conv:
        hb = tm // CONV_HALO
        last = t // CONV_HALO - 1
        in_specs += [pl.BlockSpec((1, CONV_HALO, d), lambda bi, j: (bi, jnp.maximum(j * hb - 1, 0), 0)),
                     pl.BlockSpec((1, tm, d), tok),
                     pl.BlockSpec((1, CONV_HALO, d), lambda bi, j: (bi, jnp.minimum((j + 1) * hb, last), 0)),
                     _resident((CONV_WIDTH, d)), _resident((1, d)), _resident((1, d)), _resident((1, d))]
        args += [mixed, mixed, mixed] + list(conv_params)
        scratch = [pltpu.VMEM((tm + 2 * CONV_HALO, d), F32), pltpu.VMEM((tm, d), F32)]
    else:
        in_specs += [pl.BlockSpec((1, tm, d_in), tok)]
        args += [mixed]
    in_specs += [pl.BlockSpec((1, 6, d), lambda bi, j: (bi, 0, 0)), _resident((d_in, d))]
    args += [mod, w_o]
    if has_bias:
        in_specs += [_resident((1, d))]
        args += [b_o]
    in_specs += [_resident((1, d)), _resident((d, d_ff)), _resident((d_ff, d))]
    args += [gain2, w_up, w_down]
    return pl.pallas_call(
        functools.partial(_post_kernel, conv=conv, has_bias=has_bias, d_ff_chunk=min(d_ff, 1024)),
        out_shape=jax.ShapeDtypeStruct((b, t, d), F32),
        grid=(b, t // tm),
        in_specs=in_specs,
        out_specs=pl.BlockSpec((1, tm, d), tok),
        scratch_shapes=scratch,
        compiler_params=_params(2, fuse_inputs=[False] * (len(args) - 2) + [True, True]),
        name="post_mlp",
    )(*args)


def _rope_tables(t):
    rows = t // GRID_W
    row = jnp.repeat(jnp.arange(rows, dtype=F32), GRID_W)
    col = jnp.tile(jnp.arange(GRID_W, dtype=F32), rows)
    half = HEAD_DIM // 2
    inv = 1.0 / jnp.power(ROPE_THETA, jnp.arange(0, half, 2, dtype=F32) / half)
    ang = jnp.concatenate([row[:, None] * inv, col[:, None] * inv], axis=-1)
    cos = jnp.repeat(jnp.cos(ang), 2, axis=1)
    sin = jnp.repeat(jnp.sin(ang), 2, axis=1) * jnp.tile(jnp.array([-1.0, 1.0], F32), half)
    reps = LANES // HEAD_DIM
    return jnp.tile(cos, (1, reps)), jnp.tile(sin, (1, reps))


def _prep_qkv(w_qkv, q_g, k_g, n_q, n_k):
    hq, hk = n_q // HEAD_DIM, n_k // HEAD_DIM
    w = w_qkv[:, :n_q + n_k].astype(BF16)
    wvt = w_qkv[:, n_q + n_k:].T.astype(BF16)
    scale = LOG2E / math.sqrt(HEAD_DIM)
    head_gain = jnp.concatenate([jnp.tile(q_g * scale, hq), jnp.tile(k_g, hk)])
    return w, wvt, head_gain.reshape(1, n_q + n_k).astype(F32)


def _ones_block_diag():
    idx = jnp.arange(MXU_DIM) // HEAD_DIM
    return (idx[:, None] == idx[None, :]).astype(BF16)


def kernel(x, c, ctx, c_ctx, norm1_g, norm2_g, mod_w, mod_b, mlp_up, mlp_down,
           gqa_w_qkv, gqa_q_g, gqa_k_g, gqa_w_o,
           conv_w_pw1, conv_b_pw1, conv_w_dw, conv_b_dw, conv_ln_g, conv_ln_b, conv_w_pw2, conv_b_pw2,
           diff_w_qkv, diff_q_g, diff_k_g, diff_lam_q1, diff_lam_k1, diff_lam_q2, diff_lam_k2,
           diff_subln_g, diff_w_o,
           swa_w_qkv, swa_q_g, swa_k_g, swa_sink, swa_w_o):
    b, t, d = x.shape
    n_ctx = ctx.shape[1]
    depth = norm1_g.shape[0]
    assert depth == N_MIXERS and t % GRID_W == 0 and t % n_ctx == 0

    tm = _pick(t, (512, 256, 128))
    tm_proj = _pick(t, (1024, 512, 256, 128))
    tq_win = _pick(t, (256, 128))
    tq_gqa = _pick(t, (1024, 512, 256, 128))
    tq_diff = _pick(t, (2048, 1024, 512, 256, 128))
    t_all = n_ctx + t
    tk = _pick(t_all, (768, 512, 256, 128))
    tkc = _pick(n_ctx, (768, 512, 256, 128))

    cv = jnp.concatenate([c, c_ctx[None], jnp.zeros((8 - b - 1, d), F32)], axis=0)
    mod_all = _mod_call(cv, mod_w, mod_b).reshape(depth, 8, 6, d)
    rope = _rope_tables(t)
    ones_bd = _ones_block_diag()
    row = lambda v: v.reshape(1, -1).astype(F32)

    xc = ctx
    for i in range(depth):
        m, j = i % N_MIXERS, i // N_MIXERS
        need_ctx = i < depth - 1
        mod_l = mod_all[i, :b]
        mod_c = jnp.broadcast_to(mod_all[i, b], (b, 6, d))
        g1, g2 = row(norm1_g[i]), row(norm2_g[i])
        w_up, w_down = mlp_up[i].astype(BF16), mlp_down[i].astype(BF16)
        conv_params = None
        b_o = None
        if m == 1:
            w1, b1 = conv_w_pw1[j].astype(BF16), row(conv_b_pw1[j])
            mixed = _glu_call(x, mod_l, g1, w1, b1, tm=tm_proj)
            mixed_c = _glu_call(xc, mod_c, g1, w1, b1, tm=n_ctx) if need_ctx else None
            conv_params = (conv_w_dw[j].astype(F32), row(conv_b_dw[j]), row(conv_ln_g[j]), row(conv_ln_b[j]))
            w_o, b_o = conv_w_pw2[j].astype(BF16), row(conv_b_pw2[j])
        else:
            if m == 2:
                w_qkv, q_g, k_g, w_o = diff_w_qkv[j], diff_q_g[j], diff_k_g[j], diff_w_o[j]
                n_q = n_k = w_qkv.shape[1] // 3
                q_half = tuple(h % 2 for h in range(n_q // HEAD_DIM))
            else:
                w_qkv, q_g, k_g, w_o = ((gqa_w_qkv[j], gqa_q_g[j], gqa_k_g[j], gqa_w_o[j]) if m == 0 else
                                        (swa_w_qkv[j], swa_q_g[j], swa_k_g[j], swa_w_o[j]))
                n_q = w_o.shape[0]
                n_k = (w_qkv.shape[1] - n_q) // 2
                group = n_q // n_k
                q_half = tuple((h // group) % 2 for h in range(n_q // HEAD_DIM))
            w_o = w_o.astype(BF16)
            w_p, wvt, head_gain = _prep_qkv(w_qkv, q_g, k_g, n_q, n_k)
            proj = functools.partial(_proj_call, gain=g1, w=w_p, wvt=wvt, head_gain=head_gain,
                                     ones_bd=ones_bd, n_q=n_q, n_k=n_k, q_half=q_half)
            kv_buffers = (jnp.zeros((b, t_all, n_k), BF16), jnp.zeros((b, wvt.shape[0], t_all), BF16))
            q, *kv_buffers = proj(x, mod_l, rope=rope, tm=tm_proj, kv_buffers=kv_buffers, first_row=0)
            qc, k_all, vt_all = proj(xc, mod_c, rope=None, tm=n_ctx, kv_buffers=kv_buffers, first_row=t)
            lat_keys = dict(tk=tk, key_rows=t_all, key_block=0)
            ctx_keys = dict(tk=tkc, key_rows=n_ctx, key_block=t // n_ctx)
            mixed_c = None
            if m == 0:
                mixed = _gqa_call(q, k_all, vt_all, group=group, tq=tq_gqa, **lat_keys)
                if need_ctx:
                    mixed_c = _gqa_call(qc, k_all, vt_all, group=group, tq=n_ctx, **ctx_keys)
            elif m == 2:
                lam_init = 0.8 - 0.6 * math.exp(-0.3 * i)
                lam_vecs = jnp.stack([diff_lam_q1[j], diff_lam_k1[j], diff_lam_q2[j], diff_lam_k2[j]]).astype(F32)
                sg = diff_subln_g[j].reshape(-1, 1).astype(F32)
                mixed = _diff_call(q, k_all, vt_all, lam_vecs, sg, lam_init=lam_init, tq=tq_diff, **lat_keys)
                if need_ctx:
                    mixed_c = _diff_call(qc, k_all, vt_all, lam_vecs, sg, lam_init=lam_init, tq=n_ctx, **ctx_keys)
            else:
                assert not need_ctx, "windowed layer with a context update is not supported"
                sink_rows = jnp.repeat(swa_sink[j].astype(F32).reshape(n_k // HEAD_DIM, 1, group), tq_win, axis=2)
                mixed = _window_call(q, k_all, vt_all, sink_rows, group=group, n_ctx=n_ctx, tq=tq_win)
        x = _post_call(x, mixed, mod_l, w_o, b_o, g2, w_up, w_down, conv_params, tm=tm)
        if need_ctx:
            xc = _post_call(xc, mixed_c, mod_c, w_o, b_o, g2, w_up, w_down, conv_params, tm=n_ctx)
    return x
```
